```python
import math
import jax
import jax.numpy as jnp
from jax import lax
import numpy as np

D_MODEL = 2048
BATCH = 1
SEQ = 8192
DEPTH = 2

EPS = 1e-6
D_FF = 5632
N_EVEN = (DEPTH + 1) // 2
N_ODD = DEPTH // 2

A_HEADS = 8
A_HEAD_DIM = 128
A_WIDTH = A_HEADS * A_HEAD_DIM
A_CHUNK = 128
B_HEADS = 4
B_DK = 128
B_DV = 256
B_KEY = B_HEADS * B_DK
B_VAL = B_HEADS * B_DV
B_GATE_RANK = 16
B_GATE_TAU = 16.0
B_CHUNK = 64
EVEN_SPLITS = (2 * A_WIDTH, 2 * A_WIDTH + B_KEY, 2 * A_WIDTH + 2 * B_KEY,
               2 * A_WIDTH + 2 * B_KEY + B_VAL, 2 * A_WIDTH + 2 * B_KEY + 2 * B_VAL)
EVEN_IN = 2 * A_WIDTH + 2 * B_KEY + 2 * B_VAL + B_GATE_RANK
EVEN_MIX = A_WIDTH + B_VAL
C_WIDTH = 1024
C_GROUP = 16
C_GROUPS = C_WIDTH // C_GROUP
C_STATE = 64

kernel_name = 'hybrid_gmlp_gla_s5_macaron'


def rmsnorm(x, g):
    xf = x.astype(jnp.float32)
    y = xf * lax.rsqrt(jnp.mean(xf * xf, axis=-1, keepdims=True) + EPS) * g.astype(jnp.float32)
    return y.astype(x.dtype)


def layernorm(x, g, b):
    xf = x.astype(jnp.float32)
    mu = jnp.mean(xf, axis=-1, keepdims=True)
    xc = xf - mu
    y = xc * lax.rsqrt(jnp.mean(xc * xc, axis=-1, keepdims=True) + EPS)
    return (y * g.astype(jnp.float32) + b.astype(jnp.float32)).astype(x.dtype)


def swiglu(x, w_gate, w_up, w_down):
    return (jax.nn.silu(x @ w_gate) * (x @ w_up)) @ w_down


def spatial_gating(z, ln_g, ln_b, w_s, b_s):
    u, v = jnp.split(z, 2, axis=-1)
    v = layernorm(v, ln_g, ln_b)
    bsz, seq, _ = v.shape
    nc = seq // A_CHUNK
    v = v.reshape(bsz, nc, A_CHUNK, A_HEADS, A_HEAD_DIM)
    causal = jnp.tril(jnp.ones((A_CHUNK, A_CHUNK), dtype=bool))
    w = jnp.where(causal, w_s, jnp.zeros_like(w_s))
    s = jnp.einsum('hts,bcshd->bcthd', w, v) + b_s.T[:, :, None]
    return u * s.reshape(bsz, seq, A_WIDTH)


def gla(q, k, v, gk, r, norm_g):
    f32 = jnp.float32
    bsz, seq, _ = q.shape
    nc = seq // B_CHUNK

    def chunks(t, d):
        return t.astype(f32).reshape(bsz, nc, B_CHUNK, B_HEADS, d)

    q = chunks(q, B_DK) * (B_DK ** -0.5)
    k = chunks(k, B_DK)
    v = chunks(v, B_DV)
    g = chunks(gk, B_DK)
    bcum = jnp.cumsum(g, axis=2)
    b_last = bcum[:, :, -1]
    q_dec = q * jnp.exp(bcum)
    k_inv = k * jnp.exp(-bcum)
    k_end = k * jnp.exp(b_last[:, :, None] - bcum)
    causal = jnp.tril(jnp.ones((B_CHUNK, B_CHUNK), dtype=bool))
    scores = jnp.einsum('bcihk,bcjhk->bchij', q_dec, k_inv)
    scores = jnp.where(causal, scores, jnp.zeros_like(scores))
    o_intra = jnp.einsum('bchij,bcjhv->bcihv', scores, v)
    ds = jnp.einsum('bcjhk,bcjhv->bchkv', k_end, v)
    decay = jnp.exp(b_last)

    def step(state, inp):
        d, d_s = inp
        return d[..., None] * state + d_s, state

    init = jnp.zeros((bsz, B_HEADS, B_DK, B_DV), f32)
    _, s_prev = lax.scan(step, init, (jnp.moveaxis(decay, 1, 0), jnp.moveaxis(ds, 1, 0)))
    s_prev = jnp.moveaxis(s_prev, 0, 1)
    o = o_intra + jnp.einsum('bcihk,bchkv->bcihv', q_dec, s_prev)
    o = o * lax.rsqrt(jnp.mean(o * o, axis=-1, keepdims=True) + EPS) * norm_g.astype(f32)
    o = o.reshape(bsz, seq, B_VAL)
    return (o * jax.nn.silu(r.astype(f32))).astype(r.dtype)


def s5_ssm(u, lam_re, lam_im, log_dt, b_re, b_im, c_re, c_im, d_skip):
    f32 = jnp.float32
    bsz, seq, _ = u.shape
    uf = u.astype(f32).reshape(bsz, seq, C_GROUPS, C_GROUP)
    lr = jnp.minimum(lam_re.astype(f32), -1e-4)
    li = lam_im.astype(f32)
    dt = jnp.exp(log_dt.astype(f32))[:, None]
    mag = jnp.exp(lr * dt)
    ar = mag * jnp.cos(li * dt)
    ai = mag * jnp.sin(li * dt)
    den = lr * lr + li * li
    nr = ar - 1.0
    cr = (nr * lr + ai * li) / den
    ci = (ai * lr - nr * li) / den
    br = b_re.astype(f32)
    bi = b_im.astype(f32)
    bbr = cr[..., None] * br - ci[..., None] * bi
    bbi = cr[..., None] * bi + ci[..., None] * br
    bu_r = jnp.einsum('gpc,bsgc->bsgp', bbr, uf)
    bu_i = jnp.einsum('gpc,bsgc->bsgp', bbi, uf)
    a_r = jnp.broadcast_to(ar, bu_r.shape)
    a_i = jnp.broadcast_to(ai, bu_i.shape)

    def combine(e1, e2):
        a1r, a1i, b1r, b1i = e1
        a2r, a2i, b2r, b2i = e2
        return (a2r * a1r - a2i * a1i, a2r * a1i + a2i * a1r,
                a2r * b1r - a2i * b1i + b2r, a2r * b1i + a2i * b1r + b2i)

    _, _, xr, xi = lax.associative_scan(combine, (a_r, a_i, bu_r, bu_i), axis=1)
    y = (jnp.einsum('gcp,bsgp->bsgc', c_re.astype(f32), xr)
         - jnp.einsum('gcp,bsgp->bsgc', c_im.astype(f32), xi))
    y = y + d_skip.astype(f32) * uf
    return y.reshape(bsz, seq, C_WIDTH).astype(u.dtype)


def even_mixer(h, w_in, ln_g, ln_b, w_s, b_s, w_gate2, b_gate, gla_norm_g, w_out):
    p = h @ w_in
    za, q, k, v, r, glr = jnp.split(p, EVEN_SPLITS, axis=-1)
    y_a = spatial_gating(jax.nn.gelu(za), ln_g, ln_b, w_s, b_s)
    gk = jax.nn.log_sigmoid((glr @ w_gate2 + b_gate).astype(jnp.float32)) / B_GATE_TAU
    y_b = gla(q, k, v, gk, r, gla_norm_g)
    return jnp.concatenate([y_a, y_b.astype(y_a.dtype)], axis=-1) @ w_out


def odd_mixer(h, w_in, lam_re, lam_im, log_dt, b_re, b_im, c_re, c_im, d_skip, w_glu, b_glu, w_out):
    y = s5_ssm(h @ w_in, lam_re, lam_im, log_dt, b_re, b_im, c_re, c_im, d_skip)
    z = jax.nn.gelu(y)
    z = z * jax.nn.sigmoid(z @ w_glu + b_glu)
    return z @ w_out


def setup_inputs(seed: int = 0) -> dict:
    key = jax.random.key(seed)
    ks = jax.random.split(key, 32)
    f32 = jnp.float32

    def nrm(k, shape, scale):
        return jax.random.normal(k, shape, f32) * scale

    n_idx = jnp.arange(C_STATE, dtype=f32)
    return {
        'x': nrm(ks[0], (BATCH, SEQ, D_MODEL), 1.0),
        'norm_g': 1.0 + nrm(ks[1], (DEPTH, 6, D_MODEL), 0.02),
        'ffn_w_gate': nrm(ks[2], (DEPTH, 2, D_MODEL, D_FF), D_MODEL ** -0.5),
        'ffn_w_up': nrm(ks[3], (DEPTH, 2, D_MODEL, D_FF), D_MODEL ** -0.5),
        'ffn_w_down': nrm(ks[4], (DEPTH, 2, D_FF, D_MODEL), D_FF ** -0.5),
        'ev_w_in': nrm(ks[5], (N_EVEN, D_MODEL, EVEN_IN), D_MODEL ** -0.5),
        'ev_ln_g': 1.0 + nrm(ks[6], (N_EVEN, A_WIDTH), 0.02),
        'ev_ln_b': nrm(ks[7], (N_EVEN, A_WIDTH), 0.02),
        'ev_w_s': nrm(ks[8], (N_EVEN, A_HEADS, A_CHUNK, A_CHUNK), A_CHUNK ** -0.5),
        'ev_b_s': 1.0 + nrm(ks[9], (N_EVEN, A_HEADS, A_CHUNK), 0.02),
        'ev_w_gate2': nrm(ks[10], (N_EVEN, B_GATE_RANK, B_KEY), B_GATE_RANK ** -0.5),
        'ev_b_gate': nrm(ks[11], (N_EVEN, B_KEY), 0.1),
        'ev_gla_norm_g': 1.0 + nrm(ks[12], (N_EVEN, B_DV), 0.02),
        'ev_w_out': nrm(ks[13], (N_EVEN, EVEN_MIX, D_MODEL), EVEN_MIX ** -0.5),
        'od_w_in': nrm(ks[14], (N_ODD, D_MODEL, C_WIDTH), D_MODEL ** -0.5),
        'od_lam_re': -0.5 + nrm(ks[15], (N_ODD, C_GROUPS, C_STATE), 0.01),
        'od_lam_im': math.pi * n_idx + nrm(ks[16], (N_ODD, C_GROUPS, C_STATE), 0.01),
        'od_log_dt': jax.random.uniform(ks[17], (N_ODD, C_GROUPS), f32, math.log(1e-3), math.log(1e-1)),
        'od_b_re': nrm(ks[18], (N_ODD, C_GROUPS, C_STATE, C_GROUP), (2 * C_GROUP) ** -0.5),
        'od_b_im': nrm(ks[19], (N_ODD, C_GROUPS, C_STATE, C_GROUP), (2 * C_GROUP) ** -0.5),
        'od_c_re': nrm(ks[20], (N_ODD, C_GROUPS, C_GROUP, C_STATE), C_STATE ** -0.5),
        'od_c_im': nrm(ks[21], (N_ODD, C_GROUPS, C_GROUP, C_STATE), C_STATE ** -0.5),
        'od_d': nrm(ks[22], (N_ODD, C_GROUPS, C_GROUP), 1.0),
        'od_w_glu': nrm(ks[23], (N_ODD, C_WIDTH, C_WIDTH), C_WIDTH ** -0.5),
        'od_b_glu': nrm(ks[24], (N_ODD, C_WIDTH), 0.02),
        'od_w_out': nrm(ks[25], (N_ODD, C_WIDTH, D_MODEL), C_WIDTH ** -0.5),
    }


def reference(x, norm_g, ffn_w_gate, ffn_w_up, ffn_w_down,
              ev_w_in, ev_ln_g, ev_ln_b, ev_w_s, ev_b_s, ev_w_gate2, ev_b_gate, ev_gla_norm_g, ev_w_out,
              od_w_in, od_lam_re, od_lam_im, od_log_dt, od_b_re, od_b_im, od_c_re, od_c_im, od_d,
              od_w_glu, od_b_glu, od_w_out):
    h = x
    for l in range(DEPTH):
        i = l // 2
        g = norm_g[l]
        f = swiglu(rmsnorm(h, g[0]), ffn_w_gate[l, 0], ffn_w_up[l, 0], ffn_w_down[l, 0])
        h = h + 0.5 * rmsnorm(f, g[1])
        m_in = rmsnorm(h, g[2])
        if l % 2 == 0:
            m = even_mixer(m_in, ev_w_in[i], ev_ln_g[i], ev_ln_b[i], ev_w_s[i], ev_b_s[i],
                           ev_w_gate2[i], ev_b_gate[i], ev_gla_norm_g[i], ev_w_out[i])
        else:
            m = odd_mixer(m_in, od_w_in[i], od_lam_re[i], od_lam_im[i], od_log_dt[i],
                          od_b_re[i], od_b_im[i], od_c_re[i], od_c_im[i], od_d[i],
                          od_w_glu[i], od_b_glu[i], od_w_out[i])
        h = h + rmsnorm(m, g[3])
        f = swiglu(rmsnorm(h, g[4]), ffn_w_gate[l, 1], ffn_w_up[l, 1], ffn_w_down[l, 1])
        h = h + 0.5 * rmsnorm(f, g[5])
    return h
```

```python
import functools
import math

import jax
import jax.numpy as jnp
from jax import lax
from jax.experimental import pallas as pl
from jax.experimental.pallas import tpu as pltpu

F32 = jnp.float32
BF16 = jnp.bfloat16

D_MODEL = 2048
SEQ = 8192
D_FF = 5632
EPS = 1e-6

A_HEADS = 8
A_HEAD_DIM = 128
A_WIDTH = 1024
A_CHUNK = 128
B_HEADS = 4
B_DK = 128
B_DV = 256
B_KEY = 512
B_VAL = 1024
B_GATE_RANK = 16
B_GATE_TAU = 16.0
B_CHUNK = 64
EVEN_MAIN = 2 * A_WIDTH + 2 * B_KEY + 2 * B_VAL
C_WIDTH = 1024
C_GROUP = 16
C_GROUPS = 64
C_STATE = 64

LANES = 128
VMEM_LIMIT = 56 * 1024 * 1024

ROW_TILE = 512
FF_TILE = 512
S5_CHUNK = 32
S5_NC = SEQ // S5_CHUNK
S5_COLS = C_GROUP * S5_CHUNK
S5_GB = 8


def _rms(x, g):
    return x * lax.rsqrt(jnp.mean(x * x, axis=-1, keepdims=True) + EPS) * g


def _dot(a, b):
    return jnp.dot(a, b, preferred_element_type=F32)


def _dot_nt(a, b):
    return lax.dot_general(a, b, (((1,), (1,)), ((), ())), preferred_element_type=F32)


def _dot_tn(a, b):
    return lax.dot_general(a, b, (((0,), (0,)), ((), ())), preferred_element_type=F32)


def _params(*sem):
    return pltpu.CompilerParams(dimension_semantics=sem, vmem_limit_bytes=VMEM_LIMIT)


def _ffn_kernel(h_ref, gpre_ref, gpost_ref, wg_ref, wu_ref, wd_ref, o_ref, xn_ref):
    j = pl.program_id(1)

    @pl.when(j == 0)
    def _():
        xn_ref[...] = _rms(h_ref[...], gpre_ref[...]).astype(BF16)
        o_ref[...] = jnp.zeros_like(o_ref)

    xn = xn_ref[...]
    gate = _dot(xn, wg_ref[...])
    up = _dot(xn, wu_ref[...])
    act = (jax.nn.silu(gate) * up).astype(BF16)
    o_ref[...] += _dot(act, wd_ref[...])

    @pl.when(j == pl.num_programs(1) - 1)
    def _():
        o_ref[...] = h_ref[...] + 0.5 * _rms(o_ref[...], gpost_ref[...])


def _ffn(h, g_pre, g_post, wg, wu, wd):
    tm, tf = ROW_TILE, FF_TILE
    return pl.pallas_call(
        _ffn_kernel,
        grid=(SEQ // tm, D_FF // tf),
        in_specs=[
            pl.BlockSpec((tm, D_MODEL), lambda i, j: (i, 0)),
            pl.BlockSpec((1, D_MODEL), lambda i, j: (0, 0)),
            pl.BlockSpec((1, D_MODEL), lambda i, j: (0, 0)),
            pl.BlockSpec((D_MODEL, tf), lambda i, j: (0, j)),
            pl.BlockSpec((D_MODEL, tf), lambda i, j: (0, j)),
            pl.BlockSpec((tf, D_MODEL), lambda i, j: (j, 0)),
        ],
        out_specs=pl.BlockSpec((tm, D_MODEL), lambda i, j: (i, 0)),
        out_shape=jax.ShapeDtypeStruct((SEQ, D_MODEL), F32),
        scratch_shapes=[pltpu.VMEM((tm, D_MODEL), BF16)],
        compiler_params=_params("parallel", "arbitrary"),
        name="ffn",
    )(h, g_pre, g_post, wg, wu, wd)


def _ev_in_kernel(h_ref, g_ref, w_ref, wlr_ref, wg2_ref, bg_ref, p_ref, gk_ref, xn_ref):
    j = pl.program_id(1)

    @pl.when(j == 0)
    def _():
        xn = _rms(h_ref[...], g_ref[...]).astype(BF16)
        xn_ref[...] = xn
        glr = _dot(xn, wlr_ref[...]).astype(BF16)
        pre = _dot(glr, wg2_ref[...]) + bg_ref[...]
        gk_ref[...] = jax.nn.log_sigmoid(pre) * (1.0 / B_GATE_TAU)

    p_ref[...] = _dot(xn_ref[...], w_ref[...])


def _ev_in(h, g, w_main, w_lr, w_g2, b_gate):
    tm, tn = ROW_TILE, 1024
    return pl.pallas_call(
        _ev_in_kernel,
        grid=(SEQ // tm, EVEN_MAIN // tn),
        in_specs=[
            pl.BlockSpec((tm, D_MODEL), lambda i, j: (i, 0)),
            pl.BlockSpec((1, D_MODEL), lambda i, j: (0, 0)),
            pl.BlockSpec((D_MODEL, tn), lambda i, j: (0, j)),
            pl.BlockSpec((D_MODEL, LANES), lambda i, j: (0, 0)),
            pl.BlockSpec((LANES, B_KEY), lambda i, j: (0, 0)),
            pl.BlockSpec((1, B_KEY), lambda i, j: (0, 0)),
        ],
        out_specs=[
            pl.BlockSpec((tm, tn), lambda i, j: (i, j)),
            pl.BlockSpec((tm, B_KEY), lambda i, j: (i, 0)),
        ],
        out_shape=[
            jax.ShapeDtypeStruct((SEQ, EVEN_MAIN), F32),
            jax.ShapeDtypeStruct((SEQ, B_KEY), F32),
        ],
        scratch_shapes=[pltpu.VMEM((tm, D_MODEL), BF16)],
        compiler_params=_params("parallel", "arbitrary"),
        name="ev_in",
    )(h, g, w_main, w_lr, w_g2, b_gate)


def _ev_mix_kernel(za_ref, q_ref, k_ref, v_ref, r_ref, gk_ref, lng_ref, lnb_ref, ws_ref, bs_ref,
                   gng_ref, y_ref, u_ref, vln_ref, st_ref):
    tm = za_ref.shape[0]

    @pl.when(pl.program_id(0) == 0)
    def _():
        st_ref[...] = jnp.zeros_like(st_ref)

    z = jax.nn.gelu(za_ref[...])
    u_ref[...] = z[:, :A_WIDTH]
    va = z[:, A_WIDTH:]
    mu = jnp.mean(va, axis=-1, keepdims=True)
    vc = va - mu
    vln = vc * lax.rsqrt(jnp.mean(vc * vc, axis=-1, keepdims=True) + EPS)
    vln_ref[...] = (vln * lng_ref[...] + lnb_ref[...]).astype(BF16)

    n_a = tm // A_CHUNK
    row = lax.broadcasted_iota(jnp.int32, (A_CHUNK, A_CHUNK), 0)
    col = lax.broadcasted_iota(jnp.int32, (A_CHUNK, A_CHUNK), 1)
    for hd in range(A_HEADS):
        cs = slice(hd * A_HEAD_DIM, (hd + 1) * A_HEAD_DIM)
        w = jnp.where(row >= col, ws_ref[hd], 0.0).astype(BF16)
        rhs = jnp.concatenate(
            [vln_ref[c * A_CHUNK:(c + 1) * A_CHUNK, cs] for c in range(n_a)], axis=1)
        s = _dot(w, rhs)
        for c in range(n_a):
            rs = slice(c * A_CHUNK, (c + 1) * A_CHUNK)
            sc = s[:, c * A_HEAD_DIM:(c + 1) * A_HEAD_DIM] + bs_ref[:, cs]
            y_ref[rs, cs] = (u_ref[rs, cs] * sc).astype(BF16)

    n_b = tm // B_CHUNK
    r64 = lax.broadcasted_iota(jnp.int32, (B_CHUNK, B_CHUNK), 0)
    c64 = lax.broadcasted_iota(jnp.int32, (B_CHUNK, B_CHUNK), 1)
    causal = r64 >= c64
    tri = jnp.where(causal, 1.0, 0.0).astype(BF16)
    scale = B_DK ** -0.5
    for c in range(n_b):
        rs = slice(c * B_CHUNK, (c + 1) * B_CHUNK)
        g = gk_ref[rs, :]
        g_hi = g.astype(BF16)
        g_lo = (g - g_hi.astype(F32)).astype(BF16)
        bcum = _dot(tri, g_hi) + _dot(tri, g_lo)
        b_last = bcum[B_CHUNK - 1:B_CHUNK, :]
        q = q_ref[rs, :] * scale
        k = k_ref[rs, :]
        q_dec = (q * jnp.exp(bcum)).astype(BF16)
        k_inv = (k * jnp.exp(-bcum)).astype(BF16)
        k_end = (k * jnp.exp(b_last - bcum)).astype(BF16)
        decay = jnp.exp(b_last)
        for hd in range(B_HEADS):
            ks = slice(hd * B_DK, (hd + 1) * B_DK)
            vs = slice(hd * B_DV, (hd + 1) * B_DV)
            v = v_ref[rs, vs].astype(BF16)
            scores = jnp.where(causal, _dot_nt(q_dec[:, ks], k_inv[:, ks]), 0.0).astype(BF16)
            st = st_ref[hd]
            o = _dot(scores, v) + _dot_nt(q_dec[:, ks], st.astype(BF16))
            st_ref[hd] = decay[:, ks] * st + _dot_tn(v, k_end[:, ks])
            o = o * lax.rsqrt(jnp.mean(o * o, axis=-1, keepdims=True) + EPS) * gng_ref[...]
            y_ref[rs, A_WIDTH + hd * B_DV:A_WIDTH + (hd + 1) * B_DV] = (
                o * jax.nn.silu(r_ref[rs, vs])).astype(BF16)


def _ev_mix(p, gk, ln_g, ln_b, w_s, bs_full, gla_norm_g):
    tm = ROW_TILE
    return pl.pallas_call(
        _ev_mix_kernel,
        grid=(SEQ // tm,),
        in_specs=[
            pl.BlockSpec((tm, 2 * A_WIDTH), lambda i: (i, 0)),
            pl.BlockSpec((tm, B_KEY), lambda i: (i, 2 * A_WIDTH // B_KEY)),
            pl.BlockSpec((tm, B_KEY), lambda i: (i, 2 * A_WIDTH // B_KEY + 1)),
            pl.BlockSpec((tm, B_VAL), lambda i: (i, (2 * A_WIDTH + 2 * B_KEY) // B_VAL)),
            pl.BlockSpec((tm, B_VAL), lambda i: (i, (2 * A_WIDTH + 2 * B_KEY) // B_VAL + 1)),
            pl.BlockSpec((tm, B_KEY), lambda i: (i, 0)),
            pl.BlockSpec((1, A_WIDTH), lambda i: (0, 0)),
            pl.BlockSpec((1, A_WIDTH), lambda i: (0, 0)),
            pl.BlockSpec((A_HEADS, A_CHUNK, A_CHUNK), lambda i: (0, 0, 0)),
            pl.BlockSpec((A_CHUNK, A_WIDTH), lambda i: (0, 0)),
            pl.BlockSpec((1, B_DV), lambda i: (0, 0)),
        ],
        out_specs=pl.BlockSpec((tm, A_WIDTH + B_VAL), lambda i: (i, 0)),
        out_shape=jax.ShapeDtypeStruct((SEQ, A_WIDTH + B_VAL), BF16),
        scratch_shapes=[
            pltpu.VMEM((tm, A_WIDTH), F32),
            pltpu.VMEM((tm, A_WIDTH), BF16),
            pltpu.VMEM((B_HEADS, B_DV, B_DK), F32),
        ],
        compiler_params=_params("arbitrary"),
        name="ev_mix",
    )(p, p, p, p, p, gk, ln_g, ln_b, w_s, bs_full, gla_norm_g)


def _ev_out_kernel(y_ref, w_ref, g_ref, h_ref, o_ref):
    o_ref[...] = h_ref[...] + _rms(_dot(y_ref[...], w_ref[...]), g_ref[...])


def _ev_out(y, w, g, h):
    tm = ROW_TILE
    kdim = y.shape[1]
    return pl.pallas_call(
        _ev_out_kernel,
        grid=(SEQ // tm,),
        in_specs=[
            pl.BlockSpec((tm, kdim), lambda i: (i, 0)),
            pl.BlockSpec((kdim, D_MODEL), lambda i: (0, 0)),
            pl.BlockSpec((1, D_MODEL), lambda i: (0, 0)),
            pl.BlockSpec((tm, D_MODEL), lambda i: (i, 0)),
        ],
        out_specs=pl.BlockSpec((tm, D_MODEL), lambda i: (i, 0)),
        out_shape=jax.ShapeDtypeStruct((SEQ, D_MODEL), F32),
        compiler_params=_params("parallel"),
        name="ev_out",
    )(y, w, g, h)


def _od_in_kernel(h_ref, g_ref, w_ref, u_ref):
    u_ref[...] = _dot(_rms(h_ref[...], g_ref[...]).astype(BF16), w_ref[...])


def _od_in(h, g, w):
    tm = ROW_TILE
    return pl.pallas_call(
        _od_in_kernel,
        grid=(SEQ // tm,),
        in_specs=[
            pl.BlockSpec((tm, D_MODEL), lambda i: (i, 0)),
            pl.BlockSpec((1, D_MODEL), lambda i: (0, 0)),
            pl.BlockSpec((D_MODEL, C_WIDTH), lambda i: (0, 0)),
        ],
        out_specs=pl.BlockSpec((tm, C_WIDTH), lambda i: (i, 0)),
        out_shape=jax.ShapeDtypeStruct((SEQ, C_WIDTH), F32),
        compiler_params=_params("parallel"),
        name="od_in",
    )(h, g, w)


def _s5_kernel(u_ref, m_ref, p_ref, ps_ref, q_ref, a1_ref, a2_ref, a2s_ref, y_ref,
               inc_ref, incs_ref, xs_ref):
    gb = u_ref.shape[0]
    w = 2 * C_STATE
    for gi in range(gb):
        u = u_ref[gi]
        inc_ref[:, gi * w:(gi + 1) * w] = _dot(u, p_ref[gi])
        incs_ref[:, gi * w:(gi + 1) * w] = _dot(u, ps_ref[gi])

    a1 = a1_ref[...]
    a2 = a2_ref[...]
    a2s = a2s_ref[...]

    def step(n, carry):
        x, xs = carry
        xs_ref[pl.ds(n, 1), :] = x
        x_new = a1 * x + a2 * xs + inc_ref[pl.ds(n, 1), :]
        xs_new = a1 * xs + a2s * x + incs_ref[pl.ds(n, 1), :]
        return x_new, xs_new

    zero = jnp.zeros((1, gb * w), F32)
    lax.fori_loop(0, S5_NC, step, (zero, zero))

    for gi in range(gb):
        xst = xs_ref[:, gi * w:(gi + 1) * w].astype(BF16)
        y_ref[gi] = _dot(u_ref[gi], m_ref[gi]) + _dot(xst, q_ref[gi])


def _s5(u_blk, m_mat, p_mat, ps_mat, q_mat, a1, a2, a2s):
    gb = S5_GB
    w = 2 * C_STATE
    return pl.pallas_call(
        _s5_kernel,
        grid=(C_GROUPS // gb,),
        in_specs=[
            pl.BlockSpec((gb, S5_NC, S5_COLS), lambda i: (i, 0, 0)),
            pl.BlockSpec((gb, S5_COLS, S5_COLS), lambda i: (i, 0, 0)),
            pl.BlockSpec((gb, S5_COLS, w), lambda i: (i, 0, 0)),
            pl.BlockSpec((gb, S5_COLS, w), lambda i: (i, 0, 0)),
            pl.BlockSpec((gb, w, S5_COLS), lambda i: (i, 0, 0)),
            pl.BlockSpec((1, gb * w), lambda i: (0, i)),
            pl.BlockSpec((1, gb * w), lambda i: (0, i)),
            pl.BlockSpec((1, gb * w), lambda i: (0, i)),
        ],
        out_specs=pl.BlockSpec((gb, S5_NC, S5_COLS), lambda i: (i, 0, 0)),
        out_shape=jax.ShapeDtypeStruct((C_GROUPS, S5_NC, S5_COLS), F32),
        scratch_shapes=[
            pltpu.VMEM((S5_NC, gb * w), F32),
            pltpu.VMEM((S5_NC, gb * w), F32),
            pltpu.VMEM((S5_NC, gb * w), F32),
        ],
        compiler_params=_params("parallel"),
        name="s5",
    )(u_blk, m_mat, p_mat, ps_mat, q_mat, a1, a2, a2s)


def _s5_operators(lam_re, lam_im, log_dt, b_re, b_im, c_re, c_im):
    t = S5_CHUNK
    lr = jnp.minimum(lam_re, -1e-4)
    li = lam_im
    dt = jnp.exp(log_dt)[:, None]
    mag = jnp.exp(lr * dt)
    ar = mag * jnp.cos(li * dt)
    ai = mag * jnp.sin(li * dt)
    den = lr * lr + li * li
    nr = ar - 1.0
    cr = (nr * lr + ai * li) / den
    ci = (ai * lr - nr * li) / den
    bbr = cr[..., None] * b_re - ci[..., None] * b_im
    bbi = cr[..., None] * b_im + ci[..., None] * b_re
    tau = jnp.arange(t + 1, dtype=F32)[:, None, None]
    pmag = jnp.exp(tau * (lr * dt)[None])
    pr = pmag * jnp.cos(tau * (li * dt)[None])
    pi = pmag * jnp.sin(tau * (li * dt)[None])
    hp = lax.Precision.HIGHEST
    cer = c_re[None] * pr[:t, :, None, :] - c_im[None] * pi[:t, :, None, :]
    cei = c_re[None] * pi[:t, :, None, :] + c_im[None] * pr[:t, :, None, :]
    taps = (jnp.einsum('tgcp,gpd->gtcd', cer, bbr, precision=hp)
            - jnp.einsum('tgcp,gpd->gtcd', cei, bbi, precision=hp))
    s_idx = jnp.arange(t)[:, None]
    t_idx = jnp.arange(t)[None, :]
    lag = t_idx - s_idx
    toe = taps[:, jnp.clip(lag, 0, t - 1)]
    toe = jnp.where((lag >= 0)[None, :, :, None, None], toe, 0.0)
    m_mat = toe.transpose(0, 4, 1, 3, 2).reshape(C_GROUPS, S5_COLS, S5_COLS)
    er = pr[:t][::-1]
    ei = pi[:t][::-1]
    p_re = er[..., None] * bbr[None] - ei[..., None] * bbi[None]
    p_im = er[..., None] * bbi[None] + ei[..., None] * bbr[None]
    p_re = p_re.transpose(1, 3, 0, 2).reshape(C_GROUPS, S5_COLS, C_STATE)
    p_im = p_im.transpose(1, 3, 0, 2).reshape(C_GROUPS, S5_COLS, C_STATE)
    p_mat = jnp.concatenate([p_re, p_im], axis=-1)
    ps_mat = jnp.concatenate([p_im, p_re], axis=-1)
    fr = pr[1:]
    fi = pi[1:]
    q_re = c_re[None] * fr[:, :, None, :] - c_im[None] * fi[:, :, None, :]
    q_im = -(c_re[None] * fi[:, :, None, :] + c_im[None] * fr[:, :, None, :])
    q_re = q_re.transpose(1, 3, 2, 0).reshape(C_GROUPS, C_STATE, S5_COLS)
    q_im = q_im.transpose(1, 3, 2, 0).reshape(C_GROUPS, C_STATE, S5_COLS)
    q_mat = jnp.concatenate([q_re, q_im], axis=1)
    atr, ati = pr[t], pi[t]
    a1 = jnp.concatenate([atr, atr], axis=-1).reshape(1, -1)
    a2 = jnp.concatenate([-ati, ati], axis=-1).reshape(1, -1)
    a2s = jnp.concatenate([ati, -ati], axis=-1).reshape(1, -1)
    return (m_mat.astype(BF16), p_mat.astype(BF16), ps_mat.astype(BF16), q_mat.astype(BF16),
            a1, a2, a2s)


def _od_out_kernel(y_ref, u_ref, d_ref, wglu_ref, bglu_ref, wout_ref, g_ref, h_ref, o_ref):
    y = y_ref[...] + d_ref[...] * u_ref[...]
    z = jax.nn.gelu(y)
    gate = jax.nn.sigmoid(_dot(z.astype(BF16), wglu_ref[...]) + bglu_ref[...])
    m = _dot((z * gate).astype(BF16), wout_ref[...])
    o_ref[...] = h_ref[...] + _rms(m, g_ref[...])


def _od_out(y, u, d, w_glu, b_glu, w_out, g, h):
    tm = ROW_TILE
    return pl.pallas_call(
        _od_out_kernel,
        grid=(SEQ // tm,),
        in_specs=[
            pl.BlockSpec((tm, C_WIDTH), lambda i: (i, 0)),
            pl.BlockSpec((tm, C_WIDTH), lambda i: (i, 0)),
            pl.BlockSpec((1, C_WIDTH), lambda i: (0, 0)),
            pl.BlockSpec((C_WIDTH, C_WIDTH), lambda i: (0, 0)),
            pl.BlockSpec((1, C_WIDTH), lambda i: (0, 0)),
            pl.BlockSpec((C_WIDTH, D_MODEL), lambda i: (0, 0)),
            pl.BlockSpec((1, D_MODEL), lambda i: (0, 0)),
            pl.BlockSpec((tm, D_MODEL), lambda i: (i, 0)),
        ],
        out_specs=pl.BlockSpec((tm, D_MODEL), lambda i: (i, 0)),
        out_shape=jax.ShapeDtypeStruct((SEQ, D_MODEL), F32),
        compiler_params=_params("parallel"),
        name="od_out",
    )(y, u, d, w_glu, b_glu, w_out, g, h)


def _row(v):
    return v.reshape(1, -1).astype(F32)


def kernel(x, norm_g, ffn_w_gate, ffn_w_up, ffn_w_down, ev_w_in, ev_ln_g, ev_ln_b, ev_w_s, ev_b_s, ev_w_gate2, ev_b_gate, ev_gla_norm_g, ev_w_out, od_w_in, od_lam_re, od_lam_im, od_log_dt, od_b_re, od_b_im, od_c_re, od_c_im, od_d, od_w_glu, od_b_glu, od_w_out):
    depth = norm_g.shape[0]
    h = x.reshape(SEQ, D_MODEL)
    for l in range(depth):
        i = l // 2
        g = norm_g[l]
        h = _ffn(h, _row(g[0]), _row(g[1]), ffn_w_gate[l, 0].astype(BF16),
                 ffn_w_up[l, 0].astype(BF16), ffn_w_down[l, 0].astype(BF16))
        if l % 2 == 0:
            w_in = ev_w_in[i]
            w_main = w_in[:, :EVEN_MAIN].astype(BF16)
            w_lr = jnp.pad(w_in[:, EVEN_MAIN:], ((0, 0), (0, LANES - B_GATE_RANK))).astype(BF16)
            w_g2 = jnp.pad(ev_w_gate2[i], ((0, LANES - B_GATE_RANK), (0, 0))).astype(BF16)
            p, gk = _ev_in(h, _row(g[2]), w_main, w_lr, w_g2, _row(ev_b_gate[i]))
            bs_full = jnp.repeat(ev_b_s[i].T, A_HEAD_DIM, axis=1).astype(F32)
            y = _ev_mix(p, gk, _row(ev_ln_g[i]), _row(ev_ln_b[i]), ev_w_s[i], bs_full,
                        _row(ev_gla_norm_g[i]))
            h = _ev_out(y, ev_w_out[i].astype(BF16), _row(g[3]), h)
        else:
            u = _od_in(h, _row(g[2]), od_w_in[i].astype(BF16))
            ops = _s5_operators(od_lam_re[i], od_lam_im[i], od_log_dt[i], od_b_re[i], od_b_im[i],
                                od_c_re[i], od_c_im[i])
            u_blk = (u.astype(BF16).reshape(S5_NC, S5_CHUNK, C_GROUPS, C_GROUP)
                     .transpose(2, 0, 3, 1).reshape(C_GROUPS, S5_NC, S5_COLS))
            y_blk = _s5(u_blk, *ops)
            y = (y_blk.reshape(C_GROUPS, S5_NC, C_GROUP, S5_CHUNK)
                 .transpose(1, 3, 0, 2).reshape(SEQ, C_WIDTH))
            h = _od_out(y, u, _row(od_d[i]), od_w_glu[i].astype(BF16), _row(od_b_glu[i]),
                        od_w_out[i].astype(BF16), _row(g[3]), h)
        h = _ffn(h, _row(g[4]), _row(g[5]), ffn_w_gate[l, 1].astype(BF16),
                 ffn_w_up[l, 1].astype(BF16), ffn_w_down[l, 1].astype(BF16))
    return h.reshape(x.shape)
```

```python
import jax
import jax.numpy as jnp
from jax import lax
from jax.experimental import pallas as pl
from jax.experimental.pallas import tpu as pltpu

F32 = jnp.float32
BF16 = jnp.bfloat16

D_MODEL = 2048
SEQ = 8192
D_FF = 5632
EPS = 1e-6

A_HEADS = 8
A_HEAD_DIM = 128
A_WIDTH = 1024
A_CHUNK = 128
B_HEADS = 4
B_DK = 128
B_DV = 256
B_KEY = 512
B_VAL = 1024
B_GATE_RANK = 16
B_GATE_TAU = 16.0
B_CHUNK = 64
EVEN_MAIN = 2 * A_WIDTH + 2 * B_KEY + 2 * B_VAL
C_WIDTH = 1024
C_GROUP = 16
C_GROUPS = 64
C_STATE = 64

LANES = 128
VMEM_LIMIT = 56 * 1024 * 1024
FFN_VMEM_LIMIT = 60 * 1024 * 1024

ROW_TILE = 512
EV_IN_ROWS = 1024
FFN_ROWS = 1024
NORM_ROWS = 128
FF_TILE = 256
S5_CHUNK = 32
S5_NC = SEQ // S5_CHUNK
S5_COLS = C_GROUP * S5_CHUNK
S5_GB = 8


def _rms(x, g):
    return x * lax.rsqrt(jnp.mean(x * x, axis=-1, keepdims=True) + EPS) * g


def _dot(a, b):
    return jnp.dot(a, b, preferred_element_type=F32)


def _dot_nt(a, b):
    return lax.dot_general(a, b, (((1,), (1,)), ((), ())), preferred_element_type=F32)


def _dot_tn(a, b):
    return lax.dot_general(a, b, (((0,), (0,)), ((), ())), preferred_element_type=F32)


def _params(*sem, vmem=VMEM_LIMIT):
    return pltpu.CompilerParams(dimension_semantics=sem, vmem_limit_bytes=vmem)


def _ffn_kernel(h_ref, gpre_ref, gpost_ref, wg_ref, wu_ref, wd_ref, o_ref, xn_ref):
    j = pl.program_id(1)

    n_chunks = h_ref.shape[0] // NORM_ROWS

    def rows_of(r):
        return pl.ds(pl.multiple_of(r * NORM_ROWS, NORM_ROWS), NORM_ROWS)

    @pl.when(j == 0)
    def _():
        def body(r, carry):
            rows = rows_of(r)
            xn_ref[rows, :] = _rms(h_ref[rows, :], gpre_ref[...]).astype(BF16)
            o_ref[rows, :] = jnp.zeros((NORM_ROWS, o_ref.shape[1]), F32)
            return carry
        lax.fori_loop(0, n_chunks, body, 0, unroll=2)

    xn = xn_ref[...]
    gate = _dot(xn, wg_ref[...].astype(BF16))
    up = _dot(xn, wu_ref[...].astype(BF16))
    act = (jax.nn.silu(gate) * up).astype(BF16)
    o_ref[...] += _dot(act, wd_ref[...].astype(BF16))

    @pl.when(j == pl.num_programs(1) - 1)
    def _():
        def body(r, carry):
            rows = rows_of(r)
            o_ref[rows, :] = h_ref[rows, :] + 0.5 * _rms(o_ref[rows, :], gpost_ref[...])
            return carry
        lax.fori_loop(0, n_chunks, body, 0, unroll=2)


def _ffn(h, g_pre, g_post, wg, wu, wd, layer, half):
    tm, tf = FFN_ROWS, FF_TILE
    return pl.pallas_call(
        _ffn_kernel,
        grid=(SEQ // tm, D_FF // tf),
        in_specs=[
            pl.BlockSpec((tm, D_MODEL), lambda i, j: (i, 0)),
            pl.BlockSpec((1, D_MODEL), lambda i, j: (0, 0)),
            pl.BlockSpec((1, D_MODEL), lambda i, j: (0, 0)),
            pl.BlockSpec((None, None, D_MODEL, tf), lambda i, j: (layer, half, 0, j)),
            pl.BlockSpec((None, None, D_MODEL, tf), lambda i, j: (layer, half, 0, j)),
            pl.BlockSpec((None, None, tf, D_MODEL), lambda i, j: (layer, half, j, 0)),
        ],
        out_specs=pl.BlockSpec((tm, D_MODEL), lambda i, j: (i, 0)),
        out_shape=jax.ShapeDtypeStruct((SEQ, D_MODEL), F32),
        scratch_shapes=[pltpu.VMEM((tm, D_MODEL), BF16)],
        compiler_params=_params("parallel", "arbitrary", vmem=FFN_VMEM_LIMIT),
        name="ffn",
    )(h, g_pre, g_post, wg, wu, wd)


def _ev_in_kernel(h_ref, g_ref, w_ref, wlr_ref, wg2_ref, bg_ref, p_ref, gk_ref, xn_ref):
    j = pl.program_id(1)

    @pl.when(j == 0)
    def _():
        xn = _rms(h_ref[...], g_ref[...]).astype(BF16)
        xn_ref[...] = xn
        lane = lax.broadcasted_iota(jnp.int32, (xn.shape[0], LANES), 1)
        glr = jnp.where(lane < B_GATE_RANK, _dot(xn, wlr_ref[...]), 0.0).astype(BF16)
        pre = _dot(glr, wg2_ref[...]) + bg_ref[...]
        gk_ref[...] = jax.nn.log_sigmoid(pre) * (1.0 / B_GATE_TAU)

    p_ref[...] = _dot(xn_ref[...], w_ref[...])


def _ev_in(h, g, w_in, w_g2, b_gate):
    tm, tn = EV_IN_ROWS, 1024
    return pl.pallas_call(
        _ev_in_kernel,
        grid=(SEQ // tm, EVEN_MAIN // tn),
        in_specs=[
            pl.BlockSpec((tm, D_MODEL), lambda i, j: (i, 0)),
            pl.BlockSpec((1, D_MODEL), lambda i, j: (0, 0)),
            pl.BlockSpec((D_MODEL, tn), lambda i, j: (0, j)),
            pl.BlockSpec((D_MODEL, LANES), lambda i, j: (0, EVEN_MAIN // LANES)),
            pl.BlockSpec((LANES, B_KEY), lambda i, j: (0, 0)),
            pl.BlockSpec((1, B_KEY), lambda i, j: (0, 0)),
        ],
        out_specs=[
            pl.BlockSpec((tm, tn), lambda i, j: (i, j)),
            pl.BlockSpec((tm, B_KEY), lambda i, j: (i, 0)),
        ],
        out_shape=[
            jax.ShapeDtypeStruct((SEQ, EVEN_MAIN), F32),
            jax.ShapeDtypeStruct((SEQ, B_KEY), F32),
        ],
        scratch_shapes=[pltpu.VMEM((tm, D_MODEL), BF16)],
        compiler_params=_params("parallel", "arbitrary"),
        name="ev_in",
    )(h, g, w_in, w_in, w_g2, b_gate)


def _ev_mix_kernel(za_ref, q_ref, k_ref, v_ref, r_ref, gk_ref, lng_ref, lnb_ref, ws_ref, bs_ref,
                   gng_ref, y_ref, u_ref, vln_ref, st_ref):
    tm = za_ref.shape[0]

    @pl.when(pl.program_id(0) == 0)
    def _():
        st_ref[...] = jnp.zeros_like(st_ref)

    z = jax.nn.gelu(za_ref[...])
    u_ref[...] = z[:, :A_WIDTH]
    va = z[:, A_WIDTH:]
    mu = jnp.mean(va, axis=-1, keepdims=True)
    vc = va - mu
    vln = vc * lax.rsqrt(jnp.mean(vc * vc, axis=-1, keepdims=True) + EPS)
    vln_ref[...] = (vln * lng_ref[...] + lnb_ref[...]).astype(BF16)

    n_a = tm // A_CHUNK
    row = lax.broadcasted_iota(jnp.int32, (A_CHUNK, A_CHUNK), 0)
    col = lax.broadcasted_iota(jnp.int32, (A_CHUNK, A_CHUNK), 1)
    for hd in range(A_HEADS):
        cs = slice(hd * A_HEAD_DIM, (hd + 1) * A_HEAD_DIM)
        w = jnp.where(row >= col, ws_ref[hd], 0.0).astype(BF16)
        rhs = jnp.concatenate(
            [vln_ref[c * A_CHUNK:(c + 1) * A_CHUNK, cs] for c in range(n_a)], axis=1)
        s = _dot(w, rhs)
        for c in range(n_a):
            rs = slice(c * A_CHUNK, (c + 1) * A_CHUNK)
            sc = s[:, c * A_HEAD_DIM:(c + 1) * A_HEAD_DIM] + bs_ref[:, cs]
            y_ref[rs, cs] = (u_ref[rs, cs] * sc).astype(BF16)

    n_b = tm // B_CHUNK
    r64 = lax.broadcasted_iota(jnp.int32, (B_CHUNK, B_CHUNK), 0)
    c64 = lax.broadcasted_iota(jnp.int32, (B_CHUNK, B_CHUNK), 1)
    causal = r64 >= c64
    tri = jnp.where(causal, 1.0, 0.0).astype(BF16)
    scale = B_DK ** -0.5
    for c in range(n_b):
        rs = slice(c * B_CHUNK, (c + 1) * B_CHUNK)
        g = gk_ref[rs, :]
        g_hi = g.astype(BF16)
        g_lo = (g - g_hi.astype(F32)).astype(BF16)
        bcum = _dot(tri, g_hi) + _dot(tri, g_lo)
        b_last = bcum[B_CHUNK - 1:B_CHUNK, :]
        q = q_ref[rs, :] * scale
        k = k_ref[rs, :]
        q_dec = (q * jnp.exp(bcum)).astype(BF16)
        k_inv = (k * jnp.exp(-bcum)).astype(BF16)
        k_end = (k * jnp.exp(b_last - bcum)).astype(BF16)
        decay = jnp.exp(b_last)
        for hd in range(B_HEADS):
            ks = slice(hd * B_DK, (hd + 1) * B_DK)
            vs = slice(hd * B_DV, (hd + 1) * B_DV)
            v = v_ref[rs, vs].astype(BF16)
            scores = jnp.where(causal, _dot_nt(q_dec[:, ks], k_inv[:, ks]), 0.0).astype(BF16)
            st = st_ref[hd]
            o = _dot(scores, v) + _dot_nt(q_dec[:, ks], st.astype(BF16))
            st_ref[hd] = decay[:, ks] * st + _dot_tn(v, k_end[:, ks])
            o = o * lax.rsqrt(jnp.mean(o * o, axis=-1, keepdims=True) + EPS) * gng_ref[...]
            y_ref[rs, A_WIDTH + hd * B_DV:A_WIDTH + (hd + 1) * B_DV] = (
                o * jax.nn.silu(r_ref[rs, vs])).astype(BF16)


def _ev_mix(p, gk, ln_g, ln_b, w_s, bs_full, gla_norm_g):
    tm = ROW_TILE
    return pl.pallas_call(
        _ev_mix_kernel,
        grid=(SEQ // tm,),
        in_specs=[
            pl.BlockSpec((tm, 2 * A_WIDTH), lambda i: (i, 0)),
            pl.BlockSpec((tm, B_KEY), lambda i: (i, 2 * A_WIDTH // B_KEY)),
            pl.BlockSpec((tm, B_KEY), lambda i: (i, 2 * A_WIDTH // B_KEY + 1)),
            pl.BlockSpec((tm, B_VAL), lambda i: (i, (2 * A_WIDTH + 2 * B_KEY) // B_VAL)),
            pl.BlockSpec((tm, B_VAL), lambda i: (i, (2 * A_WIDTH + 2 * B_KEY) // B_VAL + 1)),
            pl.BlockSpec((tm, B_KEY), lambda i: (i, 0)),
            pl.BlockSpec((1, A_WIDTH), lambda i: (0, 0)),
            pl.BlockSpec((1, A_WIDTH), lambda i: (0, 0)),
            pl.BlockSpec((A_HEADS, A_CHUNK, A_CHUNK), lambda i: (0, 0, 0)),
            pl.BlockSpec((A_CHUNK, A_WIDTH), lambda i: (0, 0)),
            pl.BlockSpec((1, B_DV), lambda i: (0, 0)),
        ],
        out_specs=pl.BlockSpec((tm, A_WIDTH + B_VAL), lambda i: (i, 0)),
        out_shape=jax.ShapeDtypeStruct((SEQ, A_WIDTH + B_VAL), BF16),
        scratch_shapes=[
            pltpu.VMEM((tm, A_WIDTH), F32),
            pltpu.VMEM((tm, A_WIDTH), BF16),
            pltpu.VMEM((B_HEADS, B_DV, B_DK), F32),
        ],
        compiler_params=_params("arbitrary"),
        name="ev_mix",
    )(p, p, p, p, p, gk, ln_g, ln_b, w_s, bs_full, gla_norm_g)


def _ev_out_kernel(y_ref, w_ref, g_ref, h_ref, o_ref):
    o_ref[...] = h_ref[...] + _rms(_dot(y_ref[...], w_ref[...]), g_ref[...])


def _ev_out(y, w, g, h):
    tm = ROW_TILE
    kdim = y.shape[1]
    return pl.pallas_call(
        _ev_out_kernel,
        grid=(SEQ // tm,),
        in_specs=[
            pl.BlockSpec((tm, kdim), lambda i: (i, 0)),
            pl.BlockSpec((kdim, D_MODEL), lambda i: (0, 0)),
            pl.BlockSpec((1, D_MODEL), lambda i: (0, 0)),
            pl.BlockSpec((tm, D_MODEL), lambda i: (i, 0)),
        ],
        out_specs=pl.BlockSpec((tm, D_MODEL), lambda i: (i, 0)),
        out_shape=jax.ShapeDtypeStruct((SEQ, D_MODEL), F32),
        compiler_params=_params("parallel"),
        name="ev_out",
    )(y, w, g, h)


def _od_in_kernel(h_ref, g_ref, wt_ref, ut_ref):
    ut_ref[...] = _dot_nt(wt_ref[...], _rms(h_ref[...], g_ref[...]).astype(BF16))


def _od_in(h, g, w_t):
    tm = ROW_TILE
    return pl.pallas_call(
        _od_in_kernel,
        grid=(SEQ // tm,),
        in_specs=[
            pl.BlockSpec((tm, D_MODEL), lambda i: (i, 0)),
            pl.BlockSpec((1, D_MODEL), lambda i: (0, 0)),
            pl.BlockSpec((C_WIDTH, D_MODEL), lambda i: (0, 0)),
        ],
        out_specs=pl.BlockSpec((C_WIDTH, tm), lambda i: (0, i)),
        out_shape=jax.ShapeDtypeStruct((C_WIDTH, SEQ), F32),
        compiler_params=_params("parallel"),
        name="od_in",
    )(h, g, w_t)


def _split3(x):
    hi = x.astype(BF16)
    r1 = x - hi.astype(F32)
    mid = r1.astype(BF16)
    lo = (r1 - mid.astype(F32)).astype(BF16)
    return hi, mid, lo


def _dot_f32(a, b):
    a0, a1, a2 = _split3(a)
    b0, b1, b2 = _split3(b)
    return (_dot(a0, b0) + (_dot(a0, b1) + _dot(a1, b0))
            + (_dot(a0, b2) + _dot(a1, b1) + _dot(a2, b0)))


def _lane_expand(x, e):
    x0, x1, x2 = _split3(x)
    return _dot(x0, e) + _dot(x1, e) + _dot(x2, e)


def _s5_group_operators(ar, ai, bbr, bbi, bbt, crt, cit):
    lane = lax.broadcasted_iota(jnp.int32, (C_STATE, S5_COLS), 1)
    tau = lane & (S5_CHUNK - 1)
    expand = (lax.broadcasted_iota(jnp.int32, (C_GROUP, S5_COLS), 1) // S5_CHUNK
              == lax.broadcasted_iota(jnp.int32, (C_GROUP, S5_COLS), 0)).astype(BF16)
    one = jnp.ones((C_STATE, S5_COLS), F32)
    zero = jnp.zeros((C_STATE, S5_COLS), F32)
    pr, pi = one, zero
    rr, ri = one, zero
    fr, fi = ar, ai
    for b in range(S5_CHUNK.bit_length() - 1):
        bit = ((tau >> b) & 1) == 1
        pr, pi = jnp.where(bit, pr * fr - pi * fi, pr), jnp.where(bit, pr * fi + pi * fr, pi)
        rr, ri = jnp.where(bit, rr, rr * fr - ri * fi), jnp.where(bit, ri, rr * fi + ri * fr)
        fr, fi = fr * fr - fi * fi, 2.0 * (fr * fi)
    cr = _lane_expand(crt, expand)
    ci = _lane_expand(cit, expand)
    cer = cr * pr - ci * pi
    cei = cr * pi + ci * pr
    kv = _dot_f32(bbt, jnp.concatenate([cer, -cei], axis=0))
    qr = cer * ar - cei * ai
    qi = cer * ai + cei * ar
    q = jnp.concatenate([qr, -qi], axis=0).astype(BF16)
    br = _lane_expand(bbr, expand)
    bi = _lane_expand(bbi, expand)
    ptr = br * rr - bi * ri
    pti = br * ri + bi * rr
    pt = jnp.concatenate([ptr, pti], axis=0).astype(BF16)
    pts = jnp.concatenate([pti, ptr], axis=0).astype(BF16)
    return kv, pt, pts, q


def _s5_kernel(u_ref, ar_ref, ai_ref, bbr_ref, bbi_ref, bbt_ref, crt_ref, cit_ref,
               a1_ref, a2_ref, a2s_ref, y_ref, m_ref, q_ref, inc_ref, incs_ref, xs_ref):
    gb = u_ref.shape[0]
    w = 2 * C_STATE
    keep = ((lax.broadcasted_iota(jnp.int32, (S5_CHUNK, LANES), 1) & (S5_CHUNK - 1))
            >= lax.broadcasted_iota(jnp.int32, (S5_CHUNK, LANES), 0))
    for gi in range(gb):
        ar = jnp.concatenate([ar_ref[gi]] * (S5_COLS // LANES), axis=1)
        ai = jnp.concatenate([ai_ref[gi]] * (S5_COLS // LANES), axis=1)
        kv, pt, pts, q = _s5_group_operators(ar, ai, bbr_ref[gi], bbi_ref[gi], bbt_ref[gi],
                                             crt_ref[gi], cit_ref[gi])
        q_ref[gi] = q
        for cp in range(C_GROUP):
            taps = jnp.broadcast_to(kv[cp:cp + 1, :], (S5_CHUNK, S5_COLS))
            for v in range(S5_COLS // LANES):
                tile = pltpu.roll(taps[:, v * LANES:(v + 1) * LANES], 0, 1, stride=1, stride_axis=0)
                m_ref[gi, cp * S5_CHUNK:(cp + 1) * S5_CHUNK, v * LANES:(v + 1) * LANES] = (
                    jnp.where(keep, tile, 0.0).astype(BF16))
        u = u_ref[gi]
        inc_ref[:, gi * w:(gi + 1) * w] = _dot_nt(u, pt)
        incs_ref[:, gi * w:(gi + 1) * w] = _dot_nt(u, pts)

    a1 = a1_ref[...]
    a2 = a2_ref[...]
    a2s = a2s_ref[...]

    def step(n, carry):
        x, xs = carry
        xs_ref[pl.ds(n, 1), :] = x
        x_new = a1 * x + a2 * xs + inc_ref[pl.ds(n, 1), :]
        xs_new = a1 * xs + a2s * x + incs_ref[pl.ds(n, 1), :]
        return x_new, xs_new

    zero = jnp.zeros((1, gb * w), F32)
    lax.fori_loop(0, S5_NC, step, (zero, zero))

    for gi in range(gb):
        xst = xs_ref[:, gi * w:(gi + 1) * w].astype(BF16)
        y_ref[gi] = _dot(u_ref[gi], m_ref[gi]) + _dot(xst, q_ref[gi])


def _s5(u_blk, prm):
    gb = S5_GB
    w = 2 * C_STATE
    grp3 = lambda i: (i, 0, 0)
    return pl.pallas_call(
        _s5_kernel,
        grid=(C_GROUPS // gb,),
        in_specs=[
            pl.BlockSpec((gb, S5_NC, S5_COLS), grp3),
            pl.BlockSpec((gb, C_STATE, LANES), grp3),
            pl.BlockSpec((gb, C_STATE, LANES), grp3),
            pl.BlockSpec((gb, C_STATE, C_GROUP), grp3),
            pl.BlockSpec((gb, C_STATE, C_GROUP), grp3),
            pl.BlockSpec((gb, C_GROUP, w), grp3),
            pl.BlockSpec((gb, C_STATE, C_GROUP), grp3),
            pl.BlockSpec((gb, C_STATE, C_GROUP), grp3),
            pl.BlockSpec((1, gb * w), lambda i: (0, i)),
            pl.BlockSpec((1, gb * w), lambda i: (0, i)),
            pl.BlockSpec((1, gb * w), lambda i: (0, i)),
        ],
        out_specs=pl.BlockSpec((gb, S5_NC, S5_COLS), grp3),
        out_shape=jax.ShapeDtypeStruct((C_GROUPS, S5_NC, S5_COLS), F32),
        scratch_shapes=[
            pltpu.VMEM((gb, S5_COLS, S5_COLS), BF16),
            pltpu.VMEM((gb, w, S5_COLS), BF16),
            pltpu.VMEM((S5_NC, gb * w), F32),
            pltpu.VMEM((S5_NC, gb * w), F32),
            pltpu.VMEM((S5_NC, gb * w), F32),
        ],
        compiler_params=_params("parallel"),
        name="s5",
    )(u_blk, *prm)


def _s5_params(lam_re, lam_im, log_dt, b_re, b_im, c_re, c_im):
    lr = jnp.minimum(lam_re, -1e-4)
    li = lam_im
    dt = jnp.exp(log_dt)[:, None]
    mag = jnp.exp(lr * dt)
    ar = mag * jnp.cos(li * dt)
    ai = mag * jnp.sin(li * dt)
    den = lr * lr + li * li
    nr = ar - 1.0
    cr = (nr * lr + ai * li) / den
    ci = (ai * lr - nr * li) / den
    bbr = cr[..., None] * b_re - ci[..., None] * b_im
    bbi = cr[..., None] * b_im + ci[..., None] * b_re
    bbt = jnp.concatenate([bbr, bbi], axis=1).transpose(0, 2, 1)
    atr, ati = ar, ai
    for _ in range(S5_CHUNK.bit_length() - 1):
        atr, ati = atr * atr - ati * ati, 2.0 * (atr * ati)
    a1 = jnp.concatenate([atr, atr], axis=-1).reshape(1, -1)
    a2 = jnp.concatenate([-ati, ati], axis=-1).reshape(1, -1)
    a2s = jnp.concatenate([ati, -ati], axis=-1).reshape(1, -1)
    bcast = lambda a: jnp.broadcast_to(a[..., None], a.shape + (LANES,))
    return (bcast(ar), bcast(ai), bbr, bbi, bbt, c_re.transpose(0, 2, 1), c_im.transpose(0, 2, 1),
            a1, a2, a2s)


def _od_out_kernel(yt_ref, ut_ref, d_ref, wglut_ref, bglu_ref, wout_ref, g_ref, h_ref, o_ref):
    y = yt_ref[...] + d_ref[...] * ut_ref[...]
    z = jax.nn.gelu(y)
    gate = jax.nn.sigmoid(_dot(wglut_ref[...], z.astype(BF16)) + bglu_ref[...])
    m = _dot_tn((z * gate).astype(BF16), wout_ref[...])
    o_ref[...] = h_ref[...] + _rms(m, g_ref[...])


def _od_out(y_t, u_t, d, w_glu_t, b_glu, w_out, g, h):
    tm = ROW_TILE
    return pl.pallas_call(
        _od_out_kernel,
        grid=(SEQ // tm,),
        in_specs=[
            pl.BlockSpec((C_WIDTH, tm), lambda i: (0, i)),
            pl.BlockSpec((C_WIDTH, tm), lambda i: (0, i)),
            pl.BlockSpec((C_WIDTH, 1), lambda i: (0, 0)),
            pl.BlockSpec((C_WIDTH, C_WIDTH), lambda i: (0, 0)),
            pl.BlockSpec((C_WIDTH, 1), lambda i: (0, 0)),
            pl.BlockSpec((C_WIDTH, D_MODEL), lambda i: (0, 0)),
            pl.BlockSpec((1, D_MODEL), lambda i: (0, 0)),
            pl.BlockSpec((tm, D_MODEL), lambda i: (i, 0)),
        ],
        out_specs=pl.BlockSpec((tm, D_MODEL), lambda i: (i, 0)),
        out_shape=jax.ShapeDtypeStruct((SEQ, D_MODEL), F32),
        compiler_params=_params("parallel"),
        name="od_out",
    )(y_t, u_t, d, w_glu_t, b_glu, w_out, g, h)


def _row(v):
    return v.reshape(1, -1).astype(F32)


def kernel(x, norm_g, ffn_w_gate, ffn_w_up, ffn_w_down, ev_w_in, ev_ln_g, ev_ln_b, ev_w_s, ev_b_s, ev_w_gate2, ev_b_gate, ev_gla_norm_g, ev_w_out, od_w_in, od_lam_re, od_lam_im, od_log_dt, od_b_re, od_b_im, od_c_re, od_c_im, od_d, od_w_glu, od_b_glu, od_w_out):
    depth = norm_g.shape[0]
    h = x.reshape(SEQ, D_MODEL)
    for l in range(depth):
        i = l // 2
        g = norm_g[l]
        h = _ffn(h, _row(g[0]), _row(g[1]), ffn_w_gate, ffn_w_up, ffn_w_down, l, 0)
        if l % 2 == 0:
            w_g2 = jnp.pad(ev_w_gate2[i], ((0, LANES - B_GATE_RANK), (0, 0))).astype(BF16)
            p, gk = _ev_in(h, _row(g[2]), ev_w_in[i].astype(BF16), w_g2, _row(ev_b_gate[i]))
            bs_full = jnp.repeat(ev_b_s[i].T, A_HEAD_DIM, axis=1).astype(F32)
            y = _ev_mix(p, gk, _row(ev_ln_g[i]), _row(ev_ln_b[i]), ev_w_s[i], bs_full,
                        _row(ev_gla_norm_g[i]))
            h = _ev_out(y, ev_w_out[i].astype(BF16), _row(g[3]), h)
        else:
            u_t = _od_in(h, _row(g[2]), od_w_in[i].T.astype(BF16))
            prm = _s5_params(od_lam_re[i], od_lam_im[i], od_log_dt[i], od_b_re[i], od_b_im[i],
                             od_c_re[i], od_c_im[i])
            u_blk = (u_t.astype(BF16).reshape(C_GROUPS, C_GROUP, S5_NC, S5_CHUNK)
                     .transpose(0, 2, 1, 3).reshape(C_GROUPS, S5_NC, S5_COLS))
            y_blk = _s5(u_blk, prm)
            y_t = (y_blk.reshape(C_GROUPS, S5_NC, C_GROUP, S5_CHUNK)
                   .transpose(0, 2, 1, 3).reshape(C_WIDTH, SEQ))
            h = _od_out(y_t, u_t, od_d[i].reshape(C_WIDTH, 1), od_w_glu[i].T.astype(BF16),
                        od_b_glu[i].reshape(C_WIDTH, 1), od_w_out[i].astype(BF16), _row(g[3]), h)
        h = _ffn(h, _row(g[4]), _row(g[5]), ffn_w_gate, ffn_w_up, ffn_w_down, l, 1)
    return h.reshape(x.shape)
```

```python
import jax
import jax.numpy as jnp
from jax import lax
from jax.experimental import pallas as pl
from jax.experimental.pallas import tpu as pltpu

F32 = jnp.float32
BF16 = jnp.bfloat16

D_MODEL = 2048
SEQ = 8192
D_FF = 5632
EPS = 1e-6

A_HEADS = 8
A_HEAD_DIM = 128
A_WIDTH = 1024
A_CHUNK = 128
B_HEADS = 4
B_DK = 128
B_DV = 256
B_KEY = 512
B_VAL = 1024
B_GATE_RANK = 16
B_GATE_TAU = 16.0
B_CHUNK = 64
EVEN_MAIN = 2 * A_WIDTH + 2 * B_KEY + 2 * B_VAL
C_WIDTH = 1024
C_GROUP = 16
C_GROUPS = 64
C_STATE = 64

LANES = 128
VMEM_LIMIT = 56 * 1024 * 1024
FFN_VMEM_LIMIT = 60 * 1024 * 1024

ROW_TILE = 512
EV_IN_ROWS = 1024
FFN_ROWS = 1024
NORM_ROWS = 128
FF_TILE = 256
S5_CHUNK = 32
S5_NC = SEQ // S5_CHUNK
S5_COLS = C_GROUP * S5_CHUNK
S5_GB = 8


def _rms(x, g):
    return x * lax.rsqrt(jnp.mean(x * x, axis=-1, keepdims=True) + EPS) * g


def _dot(a, b):
    return jnp.dot(a, b, preferred_element_type=F32)


def _dot_nt(a, b):
    return lax.dot_general(a, b, (((1,), (1,)), ((), ())), preferred_element_type=F32)


def _dot_tn(a, b):
    return lax.dot_general(a, b, (((0,), (0,)), ((), ())), preferred_element_type=F32)


def _params(*sem, vmem=VMEM_LIMIT):
    return pltpu.CompilerParams(dimension_semantics=sem, vmem_limit_bytes=vmem)


def _ffn_kernel(h_ref, gpre_ref, gpost_ref, wg_ref, wu_ref, wd_ref, o_ref, xn_ref):
    j = pl.program_id(1)

    n_chunks = h_ref.shape[0] // NORM_ROWS

    def rows_of(r):
        return pl.ds(pl.multiple_of(r * NORM_ROWS, NORM_ROWS), NORM_ROWS)

    @pl.when(j == 0)
    def _():
        def body(r, carry):
            rows = rows_of(r)
            xn_ref[rows, :] = _rms(h_ref[rows, :], gpre_ref[...]).astype(BF16)
            o_ref[rows, :] = jnp.zeros((NORM_ROWS, o_ref.shape[1]), F32)
            return carry
        lax.fori_loop(0, n_chunks, body, 0, unroll=2)

    xn = xn_ref[...]
    gate = _dot(xn, wg_ref[...].astype(BF16))
    up = _dot(xn, wu_ref[...].astype(BF16))
    act = (jax.nn.silu(gate) * up).astype(BF16)
    o_ref[...] += _dot(act, wd_ref[...].astype(BF16))

    @pl.when(j == pl.num_programs(1) - 1)
    def _():
        def body(r, carry):
            rows = rows_of(r)
            o_ref[rows, :] = h_ref[rows, :] + 0.5 * _rms(o_ref[rows, :], gpost_ref[...])
            return carry
        lax.fori_loop(0, n_chunks, body, 0, unroll=2)


def _ffn(h, g_pre, g_post, wg, wu, wd, layer, half):
    tm, tf = FFN_ROWS, FF_TILE
    return pl.pallas_call(
        _ffn_kernel,
        grid=(SEQ // tm, D_FF // tf),
        in_specs=[
            pl.BlockSpec((tm, D_MODEL), lambda i, j: (i, 0)),
            pl.BlockSpec((1, D_MODEL), lambda i, j: (0, 0)),
            pl.BlockSpec((1, D_MODEL), lambda i, j: (0, 0)),
            pl.BlockSpec((None, None, D_MODEL, tf), lambda i, j: (layer, half, 0, j)),
            pl.BlockSpec((None, None, D_MODEL, tf), lambda i, j: (layer, half, 0, j)),
            pl.BlockSpec((None, None, tf, D_MODEL), lambda i, j: (layer, half, j, 0)),
        ],
        out_specs=pl.BlockSpec((tm, D_MODEL), lambda i, j: (i, 0)),
        out_shape=jax.ShapeDtypeStruct((SEQ, D_MODEL), F32),
        scratch_shapes=[pltpu.VMEM((tm, D_MODEL), BF16)],
        compiler_params=_params("parallel", "arbitrary", vmem=FFN_VMEM_LIMIT),
        name="ffn",
    )(h, g_pre, g_post, wg, wu, wd)


def _ev_in_kernel(h_ref, g_ref, w_ref, wlr_ref, wg2_ref, bg_ref, p_ref, gk_ref, xn_ref):
    j = pl.program_id(1)

    @pl.when(j == 0)
    def _():
        xn = _rms(h_ref[...], g_ref[...]).astype(BF16)
        xn_ref[...] = xn
        lane = lax.broadcasted_iota(jnp.int32, (xn.shape[0], LANES), 1)
        glr = jnp.where(lane < B_GATE_RANK, _dot(xn, wlr_ref[...]), 0.0).astype(BF16)
        pre = _dot(glr, wg2_ref[...]) + bg_ref[...]
        gk_ref[...] = jax.nn.log_sigmoid(pre) * (1.0 / B_GATE_TAU)

    p_ref[...] = _dot(xn_ref[...], w_ref[...])


def _ev_in(h, g, w_in, w_g2, b_gate):
    tm, tn = EV_IN_ROWS, 1024
    return pl.pallas_call(
        _ev_in_kernel,
        grid=(SEQ // tm, EVEN_MAIN // tn),
        in_specs=[
            pl.BlockSpec((tm, D_MODEL), lambda i, j: (i, 0)),
            pl.BlockSpec((1, D_MODEL), lambda i, j: (0, 0)),
            pl.BlockSpec((D_MODEL, tn), lambda i, j: (0, j)),
            pl.BlockSpec((D_MODEL, LANES), lambda i, j: (0, EVEN_MAIN // LANES)),
            pl.BlockSpec((LANES, B_KEY), lambda i, j: (0, 0)),
            pl.BlockSpec((1, B_KEY), lambda i, j: (0, 0)),
        ],
        out_specs=[
            pl.BlockSpec((tm, tn), lambda i, j: (i, j)),
            pl.BlockSpec((tm, B_KEY), lambda i, j: (i, 0)),
        ],
        out_shape=[
            jax.ShapeDtypeStruct((SEQ, EVEN_MAIN), F32),
            jax.ShapeDtypeStruct((SEQ, B_KEY), F32),
        ],
        scratch_shapes=[pltpu.VMEM((tm, D_MODEL), BF16)],
        compiler_params=_params("parallel", "arbitrary"),
        name="ev_in",
    )(h, g, w_in, w_in, w_g2, b_gate)


def _ev_mix_kernel(za_ref, q_ref, k_ref, v_ref, r_ref, gk_ref, lng_ref, lnb_ref, ws_ref, bs_ref,
                   gng_ref, y_ref, u_ref, vln_ref, st_ref):
    tm = za_ref.shape[0]

    @pl.when(pl.program_id(0) == 0)
    def _():
        st_ref[...] = jnp.zeros_like(st_ref)

    z = jax.nn.gelu(za_ref[...])
    u_ref[...] = z[:, :A_WIDTH]
    va = z[:, A_WIDTH:]
    mu = jnp.mean(va, axis=-1, keepdims=True)
    vc = va - mu
    vln = vc * lax.rsqrt(jnp.mean(vc * vc, axis=-1, keepdims=True) + EPS)
    vln_ref[...] = (vln * lng_ref[...] + lnb_ref[...]).astype(BF16)

    n_a = tm // A_CHUNK
    row = lax.broadcasted_iota(jnp.int32, (A_CHUNK, A_CHUNK), 0)
    col = lax.broadcasted_iota(jnp.int32, (A_CHUNK, A_CHUNK), 1)
    for hd in range(A_HEADS):
        cs = slice(hd * A_HEAD_DIM, (hd + 1) * A_HEAD_DIM)
        w = jnp.where(row >= col, ws_ref[hd], 0.0).astype(BF16)
        rhs = jnp.concatenate(
            [vln_ref[c * A_CHUNK:(c + 1) * A_CHUNK, cs] for c in range(n_a)], axis=1)
        s = _dot(w, rhs)
        for c in range(n_a):
            rs = slice(c * A_CHUNK, (c + 1) * A_CHUNK)
            sc = s[:, c * A_HEAD_DIM:(c + 1) * A_HEAD_DIM] + bs_ref[:, cs]
            y_ref[rs, cs] = (u_ref[rs, cs] * sc).astype(BF16)

    n_b = tm // B_CHUNK
    r64 = lax.broadcasted_iota(jnp.int32, (B_CHUNK, B_CHUNK), 0)
    c64 = lax.broadcasted_iota(jnp.int32, (B_CHUNK, B_CHUNK), 1)
    causal = r64 >= c64
    tri = jnp.where(causal, 1.0, 0.0).astype(BF16)
    scale = B_DK ** -0.5
    for c in range(n_b):
        rs = slice(c * B_CHUNK, (c + 1) * B_CHUNK)
        g = gk_ref[rs, :]
        g_hi = g.astype(BF16)
        g_lo = (g - g_hi.astype(F32)).astype(BF16)
        bcum = _dot(tri, g_hi) + _dot(tri, g_lo)
        b_last = bcum[B_CHUNK - 1:B_CHUNK, :]
        q = q_ref[rs, :] * scale
        k = k_ref[rs, :]
        q_dec = (q * jnp.exp(bcum)).astype(BF16)
        k_inv = (k * jnp.exp(-bcum)).astype(BF16)
        k_end = (k * jnp.exp(b_last - bcum)).astype(BF16)
        decay = jnp.exp(b_last)
        for hd in range(B_HEADS):
            ks = slice(hd * B_DK, (hd + 1) * B_DK)
            vs = slice(hd * B_DV, (hd + 1) * B_DV)
            v = v_ref[rs, vs].astype(BF16)
            scores = jnp.where(causal, _dot_nt(q_dec[:, ks], k_inv[:, ks]), 0.0).astype(BF16)
            st = st_ref[hd]
            o = _dot(scores, v) + _dot_nt(q_dec[:, ks], st.astype(BF16))
            st_ref[hd] = decay[:, ks] * st + _dot_tn(v, k_end[:, ks])
            o = o * lax.rsqrt(jnp.mean(o * o, axis=-1, keepdims=True) + EPS) * gng_ref[...]
            y_ref[rs, A_WIDTH + hd * B_DV:A_WIDTH + (hd + 1) * B_DV] = (
                o * jax.nn.silu(r_ref[rs, vs])).astype(BF16)


def _ev_mix(p, gk, ln_g, ln_b, w_s, bs_full, gla_norm_g):
    tm = ROW_TILE
    return pl.pallas_call(
        _ev_mix_kernel,
        grid=(SEQ // tm,),
        in_specs=[
            pl.BlockSpec((tm, 2 * A_WIDTH), lambda i: (i, 0)),
            pl.BlockSpec((tm, B_KEY), lambda i: (i, 2 * A_WIDTH // B_KEY)),
            pl.BlockSpec((tm, B_KEY), lambda i: (i, 2 * A_WIDTH // B_KEY + 1)),
            pl.BlockSpec((tm, B_VAL), lambda i: (i, (2 * A_WIDTH + 2 * B_KEY) // B_VAL)),
            pl.BlockSpec((tm, B_VAL), lambda i: (i, (2 * A_WIDTH + 2 * B_KEY) // B_VAL + 1)),
            pl.BlockSpec((tm, B_KEY), lambda i: (i, 0)),
            pl.BlockSpec((1, A_WIDTH), lambda i: (0, 0)),
            pl.BlockSpec((1, A_WIDTH), lambda i: (0, 0)),
            pl.BlockSpec((A_HEADS, A_CHUNK, A_CHUNK), lambda i: (0, 0, 0)),
            pl.BlockSpec((A_CHUNK, A_WIDTH), lambda i: (0, 0)),
            pl.BlockSpec((1, B_DV), lambda i: (0, 0)),
        ],
        out_specs=pl.BlockSpec((tm, A_WIDTH + B_VAL), lambda i: (i, 0)),
        out_shape=jax.ShapeDtypeStruct((SEQ, A_WIDTH + B_VAL), BF16),
        scratch_shapes=[
            pltpu.VMEM((tm, A_WIDTH), F32),
            pltpu.VMEM((tm, A_WIDTH), BF16),
            pltpu.VMEM((B_HEADS, B_DV, B_DK), F32),
        ],
        compiler_params=_params("arbitrary"),
        name="ev_mix",
    )(p, p, p, p, p, gk, ln_g, ln_b, w_s, bs_full, gla_norm_g)


def _ev_out_kernel(y_ref, w_ref, g_ref, h_ref, o_ref):
    o_ref[...] = h_ref[...] + _rms(_dot(y_ref[...], w_ref[...]), g_ref[...])


def _ev_out(y, w, g, h):
    tm = ROW_TILE
    kdim = y.shape[1]
    return pl.pallas_call(
        _ev_out_kernel,
        grid=(SEQ // tm,),
        in_specs=[
            pl.BlockSpec((tm, kdim), lambda i: (i, 0)),
            pl.BlockSpec((kdim, D_MODEL), lambda i: (0, 0)),
            pl.BlockSpec((1, D_MODEL), lambda i: (0, 0)),
            pl.BlockSpec((tm, D_MODEL), lambda i: (i, 0)),
        ],
        out_specs=pl.BlockSpec((tm, D_MODEL), lambda i: (i, 0)),
        out_shape=jax.ShapeDtypeStruct((SEQ, D_MODEL), F32),
        compiler_params=_params("parallel"),
        name="ev_out",
    )(y, w, g, h)


def _od_in_kernel(h_ref, g_ref, wt_ref, ut_ref, utb_ref):
    ut = _dot_nt(wt_ref[...], _rms(h_ref[...], g_ref[...]).astype(BF16))
    ut_ref[...] = ut
    utb_ref[...] = ut.astype(BF16)


def _od_in(h, g, w_t):
    tm = ROW_TILE
    return pl.pallas_call(
        _od_in_kernel,
        grid=(SEQ // tm,),
        in_specs=[
            pl.BlockSpec((tm, D_MODEL), lambda i: (i, 0)),
            pl.BlockSpec((1, D_MODEL), lambda i: (0, 0)),
            pl.BlockSpec((C_WIDTH, D_MODEL), lambda i: (0, 0)),
        ],
        out_specs=[
            pl.BlockSpec((C_WIDTH, tm), lambda i: (0, i)),
            pl.BlockSpec((C_WIDTH, tm), lambda i: (0, i)),
        ],
        out_shape=[
            jax.ShapeDtypeStruct((C_WIDTH, SEQ), F32),
            jax.ShapeDtypeStruct((C_WIDTH, SEQ), BF16),
        ],
        compiler_params=_params("parallel"),
        name="od_in",
    )(h, g, w_t)


def _split3(x):
    hi = x.astype(BF16)
    r1 = x - hi.astype(F32)
    mid = r1.astype(BF16)
    lo = (r1 - mid.astype(F32)).astype(BF16)
    return hi, mid, lo


def _dot_f32(a, b):
    a0, a1, a2 = _split3(a)
    b0, b1, b2 = _split3(b)
    return (_dot(a0, b0) + (_dot(a0, b1) + _dot(a1, b0))
            + (_dot(a0, b2) + _dot(a1, b1) + _dot(a2, b0)))


def _lane_expand(x, e):
    x0, x1, x2 = _split3(x)
    return _dot(x0, e) + _dot(x1, e) + _dot(x2, e)


def _s5_group_operators(ar, ai, bbr, bbi, bbt, crt, cit):
    tau = lax.broadcasted_iota(jnp.int32, (C_STATE, LANES), 1) & (S5_CHUNK - 1)
    expand = (lax.broadcasted_iota(jnp.int32, (C_GROUP, S5_COLS), 1) // S5_CHUNK
              == lax.broadcasted_iota(jnp.int32, (C_GROUP, S5_COLS), 0)).astype(BF16)
    one = jnp.ones((C_STATE, LANES), F32)
    zero = jnp.zeros((C_STATE, LANES), F32)
    pr, pi = one, zero
    rr, ri = one, zero
    fr, fi = ar, ai
    for b in range(S5_CHUNK.bit_length() - 1):
        bit = ((tau >> b) & 1) == 1
        pr, pi = jnp.where(bit, pr * fr - pi * fi, pr), jnp.where(bit, pr * fi + pi * fr, pi)
        rr, ri = jnp.where(bit, rr, rr * fr - ri * fi), jnp.where(bit, ri, rr * fi + ri * fr)
        fr, fi = fr * fr - fi * fi, 2.0 * (fr * fi)
    wide = lambda a: jnp.concatenate([a] * (S5_COLS // LANES), axis=1)
    pr, pi, rr, ri, ar, ai = wide(pr), wide(pi), wide(rr), wide(ri), wide(ar), wide(ai)
    cr = _lane_expand(crt, expand)
    ci = _lane_expand(cit, expand)
    cer = cr * pr - ci * pi
    cei = cr * pi + ci * pr
    kv = _dot_f32(bbt, jnp.concatenate([cer, -cei], axis=0))
    qr = cer * ar - cei * ai
    qi = cer * ai + cei * ar
    q = jnp.concatenate([qr, -qi], axis=0).astype(BF16)
    br = _lane_expand(bbr, expand)
    bi = _lane_expand(bbi, expand)
    ptr = br * rr - bi * ri
    pti = br * ri + bi * rr
    pt = jnp.concatenate([ptr, pti], axis=0).astype(BF16)
    pts = jnp.concatenate([pti, ptr], axis=0).astype(BF16)
    return kv, pt, pts, q


def _s5_kernel(ut_ref, ar_ref, ai_ref, bbr_ref, bbi_ref, bbt_ref, crt_ref, cit_ref,
               a1_ref, a2_ref, a2s_ref, yt_ref, u_ref, m_ref, q_ref, inc_ref, incs_ref, xs_ref, y_ref):
    gb = u_ref.shape[0]
    w = 2 * C_STATE
    per = LANES // S5_CHUNK
    nsup = S5_NC // per
    keep = ((lax.broadcasted_iota(jnp.int32, (S5_CHUNK, LANES), 1) & (S5_CHUNK - 1))
            >= lax.broadcasted_iota(jnp.int32, (S5_CHUNK, LANES), 0))
    for gi in range(gb):
        for cp in range(C_GROUP):
            for k in range(per):
                u_ref[gi, k * nsup:(k + 1) * nsup, cp * S5_CHUNK:(cp + 1) * S5_CHUNK] = (
                    ut_ref[gi * C_GROUP + cp, :, k * S5_CHUNK:(k + 1) * S5_CHUNK])
        kv, pt, pts, q = _s5_group_operators(ar_ref[gi], ai_ref[gi], bbr_ref[gi], bbi_ref[gi], bbt_ref[gi],
                                             crt_ref[gi], cit_ref[gi])
        q_ref[gi] = q
        for cp in range(C_GROUP):
            taps = jnp.broadcast_to(kv[cp:cp + 1, :], (S5_CHUNK, S5_COLS))
            for v in range(S5_COLS // LANES):
                tile = pltpu.roll(taps[:, v * LANES:(v + 1) * LANES], 0, 1, stride=1, stride_axis=0)
                m_ref[gi, cp * S5_CHUNK:(cp + 1) * S5_CHUNK, v * LANES:(v + 1) * LANES] = (
                    jnp.where(keep, tile, 0.0).astype(BF16))
        u = u_ref[gi]
        inc_ref[:, gi * w:(gi + 1) * w] = _dot_nt(u, pt)
        incs_ref[:, gi * w:(gi + 1) * w] = _dot_nt(u, pts)

    a1 = a1_ref[...]
    a2 = a2_ref[...]
    a2s = a2s_ref[...]

    def step(n, carry):
        x, xs = carry
        row = pl.ds((n & (per - 1)) * nsup + (n >> (per.bit_length() - 1)), 1)
        xs_ref[row, :] = x
        x_new = a1 * x + a2 * xs + inc_ref[row, :]
        xs_new = a1 * xs + a2s * x + incs_ref[row, :]
        return x_new, xs_new

    zero = jnp.zeros((1, gb * w), F32)
    lax.fori_loop(0, S5_NC, step, (zero, zero))

    for gi in range(gb):
        xst = xs_ref[:, gi * w:(gi + 1) * w].astype(BF16)
        y_ref[...] = _dot(u_ref[gi], m_ref[gi]) + _dot(xst, q_ref[gi])
        for c in range(C_GROUP):
            for k in range(per):
                yt_ref[gi * C_GROUP + c, :, k * S5_CHUNK:(k + 1) * S5_CHUNK] = (
                    y_ref[k * nsup:(k + 1) * nsup, c * S5_CHUNK:(c + 1) * S5_CHUNK])


def _s5(u_t3, prm):
    gb = S5_GB
    w = 2 * C_STATE
    grp3 = lambda i: (i, 0, 0)
    return pl.pallas_call(
        _s5_kernel,
        grid=(C_GROUPS // gb,),
        in_specs=[
            pl.BlockSpec((gb * C_GROUP, SEQ // LANES, LANES), grp3),
            pl.BlockSpec((gb, C_STATE, LANES), grp3),
            pl.BlockSpec((gb, C_STATE, LANES), grp3),
            pl.BlockSpec((gb, C_STATE, C_GROUP), grp3),
            pl.BlockSpec((gb, C_STATE, C_GROUP), grp3),
            pl.BlockSpec((gb, C_GROUP, w), grp3),
            pl.BlockSpec((gb, C_STATE, C_GROUP), grp3),
            pl.BlockSpec((gb, C_STATE, C_GROUP), grp3),
            pl.BlockSpec((1, gb * w), lambda i: (0, i)),
            pl.BlockSpec((1, gb * w), lambda i: (0, i)),
            pl.BlockSpec((1, gb * w), lambda i: (0, i)),
        ],
        out_specs=pl.BlockSpec((gb * C_GROUP, SEQ // LANES, LANES), grp3),
        out_shape=jax.ShapeDtypeStruct((C_WIDTH, SEQ // LANES, LANES), F32),
        scratch_shapes=[
            pltpu.VMEM((gb, S5_NC, S5_COLS), BF16),
            pltpu.VMEM((gb, S5_COLS, S5_COLS), BF16),
            pltpu.VMEM((gb, w, S5_COLS), BF16),
            pltpu.VMEM((S5_NC, gb * w), F32),
            pltpu.VMEM((S5_NC, gb * w), F32),
            pltpu.VMEM((S5_NC, gb * w), F32),
            pltpu.VMEM((S5_NC, S5_COLS), F32),
        ],
        compiler_params=_params("parallel"),
        name="s5",
    )(u_t3, *prm)


def _s5_params(lam_re, lam_im, log_dt, b_re, b_im, c_re, c_im):
    lr = jnp.minimum(lam_re, -1e-4)
    li = lam_im
    dt = jnp.exp(log_dt)[:, None]
    mag = jnp.exp(lr * dt)
    ar = mag * jnp.cos(li * dt)
    ai = mag * jnp.sin(li * dt)
    den = lr * lr + li * li
    nr = ar - 1.0
    cr = (nr * lr + ai * li) / den
    ci = (ai * lr - nr * li) / den
    bbr = cr[..., None] * b_re - ci[..., None] * b_im
    bbi = cr[..., None] * b_im + ci[..., None] * b_re
    bbt = jnp.concatenate([bbr, bbi], axis=1).transpose(0, 2, 1)
    atr, ati = ar, ai
    for _ in range(S5_CHUNK.bit_length() - 1):
        atr, ati = atr * atr - ati * ati, 2.0 * (atr * ati)
    a1 = jnp.concatenate([atr, atr], axis=-1).reshape(1, -1)
    a2 = jnp.concatenate([-ati, ati], axis=-1).reshape(1, -1)
    a2s = jnp.concatenate([ati, -ati], axis=-1).reshape(1, -1)
    bcast = lambda a: jnp.broadcast_to(a[..., None], a.shape + (LANES,))
    return (bcast(ar), bcast(ai), bbr, bbi, bbt, c_re.transpose(0, 2, 1), c_im.transpose(0, 2, 1),
            a1, a2, a2s)


def _od_out_kernel(yt_ref, ut_ref, d_ref, wglut_ref, bglu_ref, wout_ref, g_ref, h_ref, o_ref):
    y = yt_ref[...] + d_ref[...] * ut_ref[...]
    z = jax.nn.gelu(y)
    gate = jax.nn.sigmoid(_dot(wglut_ref[...], z.astype(BF16)) + bglu_ref[...])
    m = _dot_tn((z * gate).astype(BF16), wout_ref[...])
    o_ref[...] = h_ref[...] + _rms(m, g_ref[...])


def _od_out(y_t, u_t, d, w_glu_t, b_glu, w_out, g, h):
    tm = ROW_TILE
    return pl.pallas_call(
        _od_out_kernel,
        grid=(SEQ // tm,),
        in_specs=[
            pl.BlockSpec((C_WIDTH, tm), lambda i: (0, i)),
            pl.BlockSpec((C_WIDTH, tm), lambda i: (0, i)),
            pl.BlockSpec((C_WIDTH, 1), lambda i: (0, 0)),
            pl.BlockSpec((C_WIDTH, C_WIDTH), lambda i: (0, 0)),
            pl.BlockSpec((C_WIDTH, 1), lambda i: (0, 0)),
            pl.BlockSpec((C_WIDTH, D_MODEL), lambda i: (0, 0)),
            pl.BlockSpec((1, D_MODEL), lambda i: (0, 0)),
            pl.BlockSpec((tm, D_MODEL), lambda i: (i, 0)),
        ],
        out_specs=pl.BlockSpec((tm, D_MODEL), lambda i: (i, 0)),
        out_shape=jax.ShapeDtypeStruct((SEQ, D_MODEL), F32),
        compiler_params=_params("parallel"),
        name="od_out",
    )(y_t, u_t, d, w_glu_t, b_glu, w_out, g, h)


def _row(v):
    return v.reshape(1, -1).astype(F32)


def kernel(x, norm_g, ffn_w_gate, ffn_w_up, ffn_w_down, ev_w_in, ev_ln_g, ev_ln_b, ev_w_s, ev_b_s, ev_w_gate2, ev_b_gate, ev_gla_norm_g, ev_w_out, od_w_in, od_lam_re, od_lam_im, od_log_dt, od_b_re, od_b_im, od_c_re, od_c_im, od_d, od_w_glu, od_b_glu, od_w_out):
    depth = norm_g.shape[0]
    h = x.reshape(SEQ, D_MODEL)
    for l in range(depth):
        i = l // 2
        g = norm_g[l]
        h = _ffn(h, _row(g[0]), _row(g[1]), ffn_w_gate, ffn_w_up, ffn_w_down, l, 0)
        if l % 2 == 0:
            w_g2 = jnp.pad(ev_w_gate2[i], ((0, LANES - B_GATE_RANK), (0, 0))).astype(BF16)
            p, gk = _ev_in(h, _row(g[2]), ev_w_in[i].astype(BF16), w_g2, _row(ev_b_gate[i]))
            bs_full = jnp.repeat(ev_b_s[i].T, A_HEAD_DIM, axis=1).astype(F32)
            y = _ev_mix(p, gk, _row(ev_ln_g[i]), _row(ev_ln_b[i]), ev_w_s[i], bs_full,
                        _row(ev_gla_norm_g[i]))
            h = _ev_out(y, ev_w_out[i].astype(BF16), _row(g[3]), h)
        else:
            u_t, u_tb = _od_in(h, _row(g[2]), od_w_in[i].T.astype(BF16))
            prm = _s5_params(od_lam_re[i], od_lam_im[i], od_log_dt[i], od_b_re[i], od_b_im[i],
                             od_c_re[i], od_c_im[i])
            y_t3 = _s5(u_tb.reshape(C_WIDTH, SEQ // LANES, LANES), prm)
            y_t = y_t3.reshape(C_WIDTH, SEQ)
            h = _od_out(y_t, u_t, od_d[i].reshape(C_WIDTH, 1), od_w_glu[i].T.astype(BF16),
                        od_b_glu[i].reshape(C_WIDTH, 1), od_w_out[i].astype(BF16), _row(g[3]), h)
        h = _ffn(h, _row(g[4]), _row(g[5]), ffn_w_gate, ffn_w_up, ffn_w_down, l, 1)
    return h.reshape(x.shape)
```

```python
import jax
import jax.numpy as jnp
from jax import lax
from jax.experimental import pallas as pl
from jax.experimental.pallas import tpu as pltpu

F32 = jnp.float32
BF16 = jnp.bfloat16

D_MODEL = 2048
SEQ = 8192
D_FF = 5632
EPS = 1e-6

A_HEADS = 8
A_HEAD_DIM = 128
A_WIDTH = 1024
A_CHUNK = 128
B_HEADS = 4
B_DK = 128
B_DV = 256
B_KEY = 512
B_VAL = 1024
B_GATE_RANK = 16
B_GATE_TAU = 16.0
B_CHUNK = 64
EVEN_MAIN = 2 * A_WIDTH + 2 * B_KEY + 2 * B_VAL
C_WIDTH = 1024
C_GROUP = 16
C_GROUPS = 64
C_STATE = 64

LANES = 128
VMEM_LIMIT = 56 * 1024 * 1024
FFN_VMEM_LIMIT = 60 * 1024 * 1024

ROW_TILE = 512
EV_IN_ROWS = 1024
FFN_ROWS = 1024
FFN_SLABS = 8
FF_TILE = 512
S5_CHUNK = 32
S5_NC = SEQ // S5_CHUNK
S5_COLS = C_GROUP * S5_CHUNK
S5_GB = 8


def _rms(x, g):
    return x * lax.rsqrt(jnp.mean(x * x, axis=-1, keepdims=True) + EPS) * g


def _dot(a, b):
    return jnp.dot(a, b, preferred_element_type=F32)


def _dot_nt(a, b):
    return lax.dot_general(a, b, (((1,), (1,)), ((), ())), preferred_element_type=F32)


def _dot_tn(a, b):
    return lax.dot_general(a, b, (((0,), (0,)), ((), ())), preferred_element_type=F32)


def _params(*sem, vmem=VMEM_LIMIT):
    return pltpu.CompilerParams(dimension_semantics=sem, vmem_limit_bytes=vmem)


def _ffn_kernel(h_hbm, gpre_ref, gpost_ref, wg_ref, wu_ref, wd_ref, o_hbm,
                h_ref, acc_ref, xn_ref, h_sem, o_sem):
    i, j = pl.program_id(0), pl.program_id(1)
    n_i, n_j = pl.num_programs(0), pl.num_programs(1)
    tm = h_ref.shape[0]
    slab = tm // FFN_SLABS

    def h_copy(tile, r):
        return pltpu.make_async_copy(h_hbm.at[pl.ds(tile * tm + r * slab, slab), :],
                                     h_ref.at[pl.ds(r * slab, slab), :], h_sem.at[r])

    def o_copy(tile, r):
        return pltpu.make_async_copy(acc_ref.at[pl.ds(r * slab, slab), :],
                                     o_hbm.at[pl.ds(tile * tm + r * slab, slab), :], o_sem.at[r])

    @pl.when(j == 0)
    def _():
        @pl.when(i == 0)
        def _():
            for r in range(FFN_SLABS):
                h_copy(0, r).start()

        for r in range(FFN_SLABS):
            rows = pl.ds(r * slab, slab)
            h_copy(i, r).wait()
            xn_ref[rows, :] = _rms(h_ref[rows, :], gpre_ref[...]).astype(BF16)

            @pl.when(i > 0)
            def _():
                o_copy(i - 1, r).wait()
            acc_ref[rows, :] = jnp.zeros((slab, acc_ref.shape[1]), F32)

    xn = xn_ref[...]
    gate = _dot(xn, wg_ref[...].astype(BF16))
    up = _dot(xn, wu_ref[...].astype(BF16))
    act = (jax.nn.silu(gate) * up).astype(BF16)
    acc_ref[...] += _dot(act, wd_ref[...].astype(BF16))

    @pl.when(j == n_j - 1)
    def _():
        for r in range(FFN_SLABS):
            rows = pl.ds(r * slab, slab)
            acc_ref[rows, :] = h_ref[rows, :] + 0.5 * _rms(acc_ref[rows, :], gpost_ref[...])
            o_copy(i, r).start()

            @pl.when(i + 1 < n_i)
            def _():
                h_copy(i + 1, r).start()

        @pl.when(i == n_i - 1)
        def _():
            for r in range(FFN_SLABS):
                o_copy(i, r).wait()


def _ffn(h, g_pre, g_post, wg, wu, wd, layer, half):
    tm, tf = FFN_ROWS, FF_TILE
    return pl.pallas_call(
        _ffn_kernel,
        grid=(SEQ // tm, D_FF // tf),
        in_specs=[
            pl.BlockSpec(memory_space=pl.ANY),
            pl.BlockSpec((1, D_MODEL), lambda i, j: (0, 0)),
            pl.BlockSpec((1, D_MODEL), lambda i, j: (0, 0)),
            pl.BlockSpec((None, None, D_MODEL, tf), lambda i, j: (layer, half, 0, j)),
            pl.BlockSpec((None, None, D_MODEL, tf), lambda i, j: (layer, half, 0, j)),
            pl.BlockSpec((None, None, tf, D_MODEL), lambda i, j: (layer, half, j, 0)),
        ],
        out_specs=pl.BlockSpec(memory_space=pl.ANY),
        out_shape=jax.ShapeDtypeStruct((SEQ, D_MODEL), F32),
        scratch_shapes=[
            pltpu.VMEM((tm, D_MODEL), F32),
            pltpu.VMEM((tm, D_MODEL), F32),
            pltpu.VMEM((tm, D_MODEL), BF16),
            pltpu.SemaphoreType.DMA((FFN_SLABS,)),
            pltpu.SemaphoreType.DMA((FFN_SLABS,)),
        ],
        compiler_params=_params("arbitrary", "arbitrary", vmem=FFN_VMEM_LIMIT),
        name="ffn",
    )(h, g_pre, g_post, wg, wu, wd)


def _ev_in_kernel(h_ref, g_ref, w_ref, wlr_ref, wg2_ref, bg_ref, p_ref, gk_ref, xn_ref):
    j = pl.program_id(1)

    @pl.when(j == 0)
    def _():
        xn = _rms(h_ref[...], g_ref[...]).astype(BF16)
        xn_ref[...] = xn
        lane = lax.broadcasted_iota(jnp.int32, (xn.shape[0], LANES), 1)
        glr = jnp.where(lane < B_GATE_RANK, _dot(xn, wlr_ref[...]), 0.0).astype(BF16)
        pre = _dot(glr, wg2_ref[...]) + bg_ref[...]
        gk_ref[...] = jax.nn.log_sigmoid(pre) * (1.0 / B_GATE_TAU)

    p_ref[...] = _dot(xn_ref[...], w_ref[...])


def _ev_in(h, g, w_in, w_g2, b_gate):
    tm, tn = EV_IN_ROWS, 1024
    return pl.pallas_call(
        _ev_in_kernel,
        grid=(SEQ // tm, EVEN_MAIN // tn),
        in_specs=[
            pl.BlockSpec((tm, D_MODEL), lambda i, j: (i, 0)),
            pl.BlockSpec((1, D_MODEL), lambda i, j: (0, 0)),
            pl.BlockSpec((D_MODEL, tn), lambda i, j: (0, j)),
            pl.BlockSpec((D_MODEL, LANES), lambda i, j: (0, EVEN_MAIN // LANES)),
            pl.BlockSpec((LANES, B_KEY), lambda i, j: (0, 0)),
            pl.BlockSpec((1, B_KEY), lambda i, j: (0, 0)),
        ],
        out_specs=[
            pl.BlockSpec((tm, tn), lambda i, j: (i, j)),
            pl.BlockSpec((tm, B_KEY), lambda i, j: (i, 0)),
        ],
        out_shape=[
            jax.ShapeDtypeStruct((SEQ, EVEN_MAIN), F32),
            jax.ShapeDtypeStruct((SEQ, B_KEY), F32),
        ],
        scratch_shapes=[pltpu.VMEM((tm, D_MODEL), BF16)],
        compiler_params=_params("parallel", "arbitrary"),
        name="ev_in",
    )(h, g, w_in, w_in, w_g2, b_gate)


def _ev_mix_kernel(za_ref, q_ref, k_ref, v_ref, r_ref, gk_ref, lng_ref, lnb_ref, ws_ref, bs_ref,
                   gng_ref, y_ref, u_ref, vln_ref, st_ref):
    tm = za_ref.shape[0]

    @pl.when(pl.program_id(0) == 0)
    def _():
        st_ref[...] = jnp.zeros_like(st_ref)

    z = jax.nn.gelu(za_ref[...])
    u_ref[...] = z[:, :A_WIDTH]
    va = z[:, A_WIDTH:]
    mu = jnp.mean(va, axis=-1, keepdims=True)
    vc = va - mu
    vln = vc * lax.rsqrt(jnp.mean(vc * vc, axis=-1, keepdims=True) + EPS)
    vln_ref[...] = (vln * lng_ref[...] + lnb_ref[...]).astype(BF16)

    n_a = tm // A_CHUNK
    row = lax.broadcasted_iota(jnp.int32, (A_CHUNK, A_CHUNK), 0)
    col = lax.broadcasted_iota(jnp.int32, (A_CHUNK, A_CHUNK), 1)
    for hd in range(A_HEADS):
        cs = slice(hd * A_HEAD_DIM, (hd + 1) * A_HEAD_DIM)
        w = jnp.where(row >= col, ws_ref[hd], 0.0).astype(BF16)
        rhs = jnp.concatenate(
            [vln_ref[c * A_CHUNK:(c + 1) * A_CHUNK, cs] for c in range(n_a)], axis=1)
        s = _dot(w, rhs)
        for c in range(n_a):
            rs = slice(c * A_CHUNK, (c + 1) * A_CHUNK)
            sc = s[:, c * A_HEAD_DIM:(c + 1) * A_HEAD_DIM] + bs_ref[:, cs]
            y_ref[rs, cs] = (u_ref[rs, cs] * sc).astype(BF16)

    n_b = tm // B_CHUNK
    r64 = lax.broadcasted_iota(jnp.int32, (B_CHUNK, B_CHUNK), 0)
    c64 = lax.broadcasted_iota(jnp.int32, (B_CHUNK, B_CHUNK), 1)
    causal = r64 >= c64
    tri = jnp.where(causal, 1.0, 0.0).astype(BF16)
    scale = B_DK ** -0.5
    for c in range(n_b):
        rs = slice(c * B_CHUNK, (c + 1) * B_CHUNK)
        g = gk_ref[rs, :]
        g_hi = g.astype(BF16)
        g_lo = (g - g_hi.astype(F32)).astype(BF16)
        bcum = _dot(tri, g_hi) + _dot(tri, g_lo)
        b_last = bcum[B_CHUNK - 1:B_CHUNK, :]
        q = q_ref[rs, :] * scale
        k = k_ref[rs, :]
        q_dec = (q * jnp.exp(bcum)).astype(BF16)
        k_inv = (k * jnp.exp(-bcum)).astype(BF16)
        k_end = (k * jnp.exp(b_last - bcum)).astype(BF16)
        decay = jnp.exp(b_last)
        for hd in range(B_HEADS):
            ks = slice(hd * B_DK, (hd + 1) * B_DK)
            vs = slice(hd * B_DV, (hd + 1) * B_DV)
            v = v_ref[rs, vs].astype(BF16)
            scores = jnp.where(causal, _dot_nt(q_dec[:, ks], k_inv[:, ks]), 0.0).astype(BF16)
            st = st_ref[hd]
            o = _dot(scores, v) + _dot_nt(q_dec[:, ks], st.astype(BF16))
            st_ref[hd] = decay[:, ks] * st + _dot_tn(v, k_end[:, ks])
            o = o * lax.rsqrt(jnp.mean(o * o, axis=-1, keepdims=True) + EPS) * gng_ref[...]
            y_ref[rs, A_WIDTH + hd * B_DV:A_WIDTH + (hd + 1) * B_DV] = (
                o * jax.nn.silu(r_ref[rs, vs])).astype(BF16)


def _ev_mix(p, gk, ln_g, ln_b, w_s, bs_full, gla_norm_g):
    tm = ROW_TILE
    return pl.pallas_call(
        _ev_mix_kernel,
        grid=(SEQ // tm,),
        in_specs=[
            pl.BlockSpec((tm, 2 * A_WIDTH), lambda i: (i, 0)),
            pl.BlockSpec((tm, B_KEY), lambda i: (i, 2 * A_WIDTH // B_KEY)),
            pl.BlockSpec((tm, B_KEY), lambda i: (i, 2 * A_WIDTH // B_KEY + 1)),
            pl.BlockSpec((tm, B_VAL), lambda i: (i, (2 * A_WIDTH + 2 * B_KEY) // B_VAL)),
            pl.BlockSpec((tm, B_VAL), lambda i: (i, (2 * A_WIDTH + 2 * B_KEY) // B_VAL + 1)),
            pl.BlockSpec((tm, B_KEY), lambda i: (i, 0)),
            pl.BlockSpec((1, A_WIDTH), lambda i: (0, 0)),
            pl.BlockSpec((1, A_WIDTH), lambda i: (0, 0)),
            pl.BlockSpec((A_HEADS, A_CHUNK, A_CHUNK), lambda i: (0, 0, 0)),
            pl.BlockSpec((A_CHUNK, A_WIDTH), lambda i: (0, 0)),
            pl.BlockSpec((1, B_DV), lambda i: (0, 0)),
        ],
        out_specs=pl.BlockSpec((tm, A_WIDTH + B_VAL), lambda i: (i, 0)),
        out_shape=jax.ShapeDtypeStruct((SEQ, A_WIDTH + B_VAL), BF16),
        scratch_shapes=[
            pltpu.VMEM((tm, A_WIDTH), F32),
            pltpu.VMEM((tm, A_WIDTH), BF16),
            pltpu.VMEM((B_HEADS, B_DV, B_DK), F32),
        ],
        compiler_params=_params("arbitrary"),
        name="ev_mix",
    )(p, p, p, p, p, gk, ln_g, ln_b, w_s, bs_full, gla_norm_g)


def _ev_out_kernel(y_ref, w_ref, g_ref, h_ref, o_ref):
    o_ref[...] = h_ref[...] + _rms(_dot(y_ref[...], w_ref[...]), g_ref[...])


def _ev_out(y, w, g, h):
    tm = ROW_TILE
    kdim = y.shape[1]
    return pl.pallas_call(
        _ev_out_kernel,
        grid=(SEQ // tm,),
        in_specs=[
            pl.BlockSpec((tm, kdim), lambda i: (i, 0)),
            pl.BlockSpec((kdim, D_MODEL), lambda i: (0, 0)),
            pl.BlockSpec((1, D_MODEL), lambda i: (0, 0)),
            pl.BlockSpec((tm, D_MODEL), lambda i: (i, 0)),
        ],
        out_specs=pl.BlockSpec((tm, D_MODEL), lambda i: (i, 0)),
        out_shape=jax.ShapeDtypeStruct((SEQ, D_MODEL), F32),
        compiler_params=_params("parallel"),
        name="ev_out",
    )(y, w, g, h)


def _od_in_kernel(h_ref, g_ref, wt_ref, ut_ref, utb_ref):
    ut = _dot_nt(wt_ref[...], _rms(h_ref[...], g_ref[...]).astype(BF16))
    ut_ref[...] = ut
    utb_ref[...] = ut.astype(BF16)


def _od_in(h, g, w_t):
    tm = ROW_TILE
    return pl.pallas_call(
        _od_in_kernel,
        grid=(SEQ // tm,),
        in_specs=[
            pl.BlockSpec((tm, D_MODEL), lambda i: (i, 0)),
            pl.BlockSpec((1, D_MODEL), lambda i: (0, 0)),
            pl.BlockSpec((C_WIDTH, D_MODEL), lambda i: (0, 0)),
        ],
        out_specs=[
            pl.BlockSpec((C_WIDTH, tm), lambda i: (0, i)),
            pl.BlockSpec((C_WIDTH, tm), lambda i: (0, i)),
        ],
        out_shape=[
            jax.ShapeDtypeStruct((C_WIDTH, SEQ), F32),
            jax.ShapeDtypeStruct((C_WIDTH, SEQ), BF16),
        ],
        compiler_params=_params("parallel"),
        name="od_in",
    )(h, g, w_t)


def _split3(x):
    hi = x.astype(BF16)
    r1 = x - hi.astype(F32)
    mid = r1.astype(BF16)
    lo = (r1 - mid.astype(F32)).astype(BF16)
    return hi, mid, lo


def _dot_f32(a, b):
    a0, a1, a2 = _split3(a)
    b0, b1, b2 = _split3(b)
    return (_dot(a0, b0) + (_dot(a0, b1) + _dot(a1, b0))
            + (_dot(a0, b2) + _dot(a1, b1) + _dot(a2, b0)))


def _lane_expand(x, e):
    x0, x1, x2 = _split3(x)
    return _dot(x0, e) + _dot(x1, e) + _dot(x2, e)


def _s5_group_operators(ar, ai, bbr, bbi, bbt, crt, cit):
    tau = lax.broadcasted_iota(jnp.int32, (C_STATE, LANES), 1) & (S5_CHUNK - 1)
    expand = (lax.broadcasted_iota(jnp.int32, (C_GROUP, S5_COLS), 1) // S5_CHUNK
              == lax.broadcasted_iota(jnp.int32, (C_GROUP, S5_COLS), 0)).astype(BF16)
    one = jnp.ones((C_STATE, LANES), F32)
    zero = jnp.zeros((C_STATE, LANES), F32)
    pr, pi = one, zero
    rr, ri = one, zero
    fr, fi = ar, ai
    for b in range(S5_CHUNK.bit_length() - 1):
        bit = ((tau >> b) & 1) == 1
        pr, pi = jnp.where(bit, pr * fr - pi * fi, pr), jnp.where(bit, pr * fi + pi * fr, pi)
        rr, ri = jnp.where(bit, rr, rr * fr - ri * fi), jnp.where(bit, ri, rr * fi + ri * fr)
        fr, fi = fr * fr - fi * fi, 2.0 * (fr * fi)
    wide = lambda a: jnp.concatenate([a] * (S5_COLS // LANES), axis=1)
    pr, pi, rr, ri, ar, ai = wide(pr), wide(pi), wide(rr), wide(ri), wide(ar), wide(ai)
    cr = _lane_expand(crt, expand)
    ci = _lane_expand(cit, expand)
    cer = cr * pr - ci * pi
    cei = cr * pi + ci * pr
    kv = _dot_f32(bbt, jnp.concatenate([cer, -cei], axis=0))
    qr = cer * ar - cei * ai
    qi = cer * ai + cei * ar
    q = jnp.concatenate([qr, -qi], axis=0).astype(BF16)
    br = _lane_expand(bbr, expand)
    bi = _lane_expand(bbi, expand)
    ptr = br * rr - bi * ri
    pti = br * ri + bi * rr
    pt = jnp.concatenate([ptr, pti], axis=0).astype(BF16)
    pts = jnp.concatenate([pti, ptr], axis=0).astype(BF16)
    return kv, pt, pts, q


def _s5_kernel(ut_ref, ar_ref, ai_ref, bbr_ref, bbi_ref, bbt_ref, crt_ref, cit_ref,
               a1_ref, a2_ref, a2s_ref, yt_ref, u_ref, m_ref, q_ref, inc_ref, incs_ref, xs_ref, y_ref):
    gb = u_ref.shape[0]
    w = 2 * C_STATE
    per = LANES // S5_CHUNK
    nsup = S5_NC // per
    keep = ((lax.broadcasted_iota(jnp.int32, (S5_CHUNK, LANES), 1) & (S5_CHUNK - 1))
            >= lax.broadcasted_iota(jnp.int32, (S5_CHUNK, LANES), 0))
    for gi in range(gb):
        for cp in range(C_GROUP):
            for k in range(per):
                u_ref[gi, k * nsup:(k + 1) * nsup, cp * S5_CHUNK:(cp + 1) * S5_CHUNK] = (
                    ut_ref[gi * C_GROUP + cp, :, k * S5_CHUNK:(k + 1) * S5_CHUNK])
        kv, pt, pts, q = _s5_group_operators(ar_ref[gi], ai_ref[gi], bbr_ref[gi], bbi_ref[gi], bbt_ref[gi],
                                             crt_ref[gi], cit_ref[gi])
        q_ref[gi] = q
        for cp in range(C_GROUP):
            taps = jnp.broadcast_to(kv[cp:cp + 1, :], (S5_CHUNK, S5_COLS))
            for v in range(S5_COLS // LANES):
                tile = pltpu.roll(taps[:, v * LANES:(v + 1) * LANES], 0, 1, stride=1, stride_axis=0)
                m_ref[gi, cp * S5_CHUNK:(cp + 1) * S5_CHUNK, v * LANES:(v + 1) * LANES] = (
                    jnp.where(keep, tile, 0.0).astype(BF16))
        u = u_ref[gi]
        inc_ref[:, gi * w:(gi + 1) * w] = _dot_nt(u, pt)
        incs_ref[:, gi * w:(gi + 1) * w] = _dot_nt(u, pts)

    a1 = a1_ref[...]
    a2 = a2_ref[...]
    a2s = a2s_ref[...]

    def step(n, carry):
        x, xs = carry
        row = pl.ds((n & (per - 1)) * nsup + (n >> (per.bit_length() - 1)), 1)
        xs_ref[row, :] = x
        x_new = a1 * x + a2 * xs + inc_ref[row, :]
        xs_new = a1 * xs + a2s * x + incs_ref[row, :]
        return x_new, xs_new

    zero = jnp.zeros((1, gb * w), F32)
    lax.fori_loop(0, S5_NC, step, (zero, zero))

    for gi in range(gb):
        xst = xs_ref[:, gi * w:(gi + 1) * w].astype(BF16)
        y_ref[...] = _dot(u_ref[gi], m_ref[gi]) + _dot(xst, q_ref[gi])
        for c in range(C_GROUP):
            for k in range(per):
                yt_ref[gi * C_GROUP + c, :, k * S5_CHUNK:(k + 1) * S5_CHUNK] = (
                    y_ref[k * nsup:(k + 1) * nsup, c * S5_CHUNK:(c + 1) * S5_CHUNK])


def _s5(u_t3, prm):
    gb = S5_GB
    w = 2 * C_STATE
    grp3 = lambda i: (i, 0, 0)
    return pl.pallas_call(
        _s5_kernel,
        grid=(C_GROUPS // gb,),
        in_specs=[
            pl.BlockSpec((gb * C_GROUP, SEQ // LANES, LANES), grp3),
            pl.BlockSpec((gb, C_STATE, LANES), grp3),
            pl.BlockSpec((gb, C_STATE, LANES), grp3),
            pl.BlockSpec((gb, C_STATE, C_GROUP), grp3),
            pl.BlockSpec((gb, C_STATE, C_GROUP), grp3),
            pl.BlockSpec((gb, C_GROUP, w), grp3),
            pl.BlockSpec((gb, C_STATE, C_GROUP), grp3),
            pl.BlockSpec((gb, C_STATE, C_GROUP), grp3),
            pl.BlockSpec((1, gb * w), lambda i: (0, i)),
            pl.BlockSpec((1, gb * w), lambda i: (0, i)),
            pl.BlockSpec((1, gb * w), lambda i: (0, i)),
        ],
        out_specs=pl.BlockSpec((gb * C_GROUP, SEQ // LANES, LANES), grp3),
        out_shape=jax.ShapeDtypeStruct((C_WIDTH, SEQ // LANES, LANES), F32),
        scratch_shapes=[
            pltpu.VMEM((gb, S5_NC, S5_COLS), BF16),
            pltpu.VMEM((gb, S5_COLS, S5_COLS), BF16),
            pltpu.VMEM((gb, w, S5_COLS), BF16),
            pltpu.VMEM((S5_NC, gb * w), F32),
            pltpu.VMEM((S5_NC, gb * w), F32),
            pltpu.VMEM((S5_NC, gb * w), F32),
            pltpu.VMEM((S5_NC, S5_COLS), F32),
        ],
        compiler_params=_params("parallel"),
        name="s5",
    )(u_t3, *prm)


def _s5_params(lam_re, lam_im, log_dt, b_re, b_im, c_re, c_im):
    lr = jnp.minimum(lam_re, -1e-4)
    li = lam_im
    dt = jnp.exp(log_dt)[:, None]
    mag = jnp.exp(lr * dt)
    ar = mag * jnp.cos(li * dt)
    ai = mag * jnp.sin(li * dt)
    den = lr * lr + li * li
    nr = ar - 1.0
    cr = (nr * lr + ai * li) / den
    ci = (ai * lr - nr * li) / den
    bbr = cr[..., None] * b_re - ci[..., None] * b_im
    bbi = cr[..., None] * b_im + ci[..., None] * b_re
    bbt = jnp.concatenate([bbr, bbi], axis=1).transpose(0, 2, 1)
    atr, ati = ar, ai
    for _ in range(S5_CHUNK.bit_length() - 1):
        atr, ati = atr * atr - ati * ati, 2.0 * (atr * ati)
    a1 = jnp.concatenate([atr, atr], axis=-1).reshape(1, -1)
    a2 = jnp.concatenate([-ati, ati], axis=-1).reshape(1, -1)
    a2s = jnp.concatenate([ati, -ati], axis=-1).reshape(1, -1)
    bcast = lambda a: jnp.broadcast_to(a[..., None], a.shape + (LANES,))
    return (bcast(ar), bcast(ai), bbr, bbi, bbt, c_re.transpose(0, 2, 1), c_im.transpose(0, 2, 1),
            a1, a2, a2s)


def _od_out_kernel(yt_ref, ut_ref, d_ref, wglut_ref, bglu_ref, wout_ref, g_ref, h_ref, o_ref):
    y = yt_ref[...] + d_ref[...] * ut_ref[...]
    z = jax.nn.gelu(y)
    gate = jax.nn.sigmoid(_dot(wglut_ref[...], z.astype(BF16)) + bglu_ref[...])
    m = _dot_tn((z * gate).astype(BF16), wout_ref[...])
    o_ref[...] = h_ref[...] + _rms(m, g_ref[...])


def _od_out(y_t, u_t, d, w_glu_t, b_glu, w_out, g, h):
    tm = ROW_TILE
    return pl.pallas_call(
        _od_out_kernel,
        grid=(SEQ // tm,),
        in_specs=[
            pl.BlockSpec((C_WIDTH, tm), lambda i: (0, i)),
            pl.BlockSpec((C_WIDTH, tm), lambda i: (0, i)),
            pl.BlockSpec((C_WIDTH, 1), lambda i: (0, 0)),
            pl.BlockSpec((C_WIDTH, C_WIDTH), lambda i: (0, 0)),
            pl.BlockSpec((C_WIDTH, 1), lambda i: (0, 0)),
            pl.BlockSpec((C_WIDTH, D_MODEL), lambda i: (0, 0)),
            pl.BlockSpec((1, D_MODEL), lambda i: (0, 0)),
            pl.BlockSpec((tm, D_MODEL), lambda i: (i, 0)),
        ],
        out_specs=pl.BlockSpec((tm, D_MODEL), lambda i: (i, 0)),
        out_shape=jax.ShapeDtypeStruct((SEQ, D_MODEL), F32),
        compiler_params=_params("parallel"),
        name="od_out",
    )(y_t, u_t, d, w_glu_t, b_glu, w_out, g, h)


def _row(v):
    return v.reshape(1, -1).astype(F32)


def kernel(x, norm_g, ffn_w_gate, ffn_w_up, ffn_w_down, ev_w_in, ev_ln_g, ev_ln_b, ev_w_s, ev_b_s, ev_w_gate2, ev_b_gate, ev_gla_norm_g, ev_w_out, od_w_in, od_lam_re, od_lam_im, od_log_dt, od_b_re, od_b_im, od_c_re, od_c_im, od_d, od_w_glu, od_b_glu, od_w_out):
    depth = norm_g.shape[0]
    h = x.reshape(SEQ, D_MODEL)
    for l in range(depth):
        i = l // 2
        g = norm_g[l]
        h = _ffn(h, _row(g[0]), _row(g[1]), ffn_w_gate, ffn_w_up, ffn_w_down, l, 0)
        if l % 2 == 0:
            w_g2 = jnp.pad(ev_w_gate2[i], ((0, LANES - B_GATE_RANK), (0, 0))).astype(BF16)
            p, gk = _ev_in(h, _row(g[2]), ev_w_in[i].astype(BF16), w_g2, _row(ev_b_gate[i]))
            bs_full = jnp.repeat(ev_b_s[i].T, A_HEAD_DIM, axis=1).astype(F32)
            y = _ev_mix(p, gk, _row(ev_ln_g[i]), _row(ev_ln_b[i]), ev_w_s[i], bs_full,
                        _row(ev_gla_norm_g[i]))
            h = _ev_out(y, ev_w_out[i].astype(BF16), _row(g[3]), h)
        else:
            u_t, u_tb = _od_in(h, _row(g[2]), od_w_in[i].T.astype(BF16))
            prm = _s5_params(od_lam_re[i], od_lam_im[i], od_log_dt[i], od_b_re[i], od_b_im[i],
                             od_c_re[i], od_c_im[i])
            y_t3 = _s5(u_tb.reshape(C_WIDTH, SEQ // LANES, LANES), prm)
            y_t = y_t3.reshape(C_WIDTH, SEQ)
            h = _od_out(y_t, u_t, od_d[i].reshape(C_WIDTH, 1), od_w_glu[i].T.astype(BF16),
                        od_b_glu[i].reshape(C_WIDTH, 1), od_w_out[i].astype(BF16), _row(g[3]), h)
        h = _ffn(h, _row(g[4]), _row(g[5]), ffn_w_gate, ffn_w_up, ffn_w_down, l, 1)
    return h.reshape(x.shape)
```

```python
import jax
import jax.numpy as jnp
from jax import lax
from jax.experimental import pallas as pl
from jax.experimental.pallas import tpu as pltpu

F32 = jnp.float32
BF16 = jnp.bfloat16

D_MODEL = 2048
SEQ = 8192
D_FF = 5632
EPS = 1e-6

A_HEADS = 8
A_HEAD_DIM = 128
A_WIDTH = 1024
A_CHUNK = 128
B_HEADS = 4
B_DK = 128
B_DV = 256
B_KEY = 512
B_VAL = 1024
B_GATE_RANK = 16
B_GATE_TAU = 16.0
B_CHUNK = 64
EVEN_MAIN = 2 * A_WIDTH + 2 * B_KEY + 2 * B_VAL
C_WIDTH = 1024
C_GROUP = 16
C_GROUPS = 64
C_STATE = 64

LANES = 128
VMEM_LIMIT = 56 * 1024 * 1024
FFN_VMEM_LIMIT = 60 * 1024 * 1024

ROW_TILE = 512
EV_IN_ROWS = 1024
FFN_ROWS = 1024
FFN_SLABS = 8
FFN_EPI_ROWS = 16
FF_TILE = 512
S5_CHUNK = 32
S5_NC = SEQ // S5_CHUNK
S5_COLS = C_GROUP * S5_CHUNK
S5_GB = 8


def _rms(x, g):
    return x * lax.rsqrt(jnp.mean(x * x, axis=-1, keepdims=True) + EPS) * g


def _dot(a, b):
    return jnp.dot(a, b, preferred_element_type=F32)


def _dot_nt(a, b):
    return lax.dot_general(a, b, (((1,), (1,)), ((), ())), preferred_element_type=F32)


def _dot_tn(a, b):
    return lax.dot_general(a, b, (((0,), (0,)), ((), ())), preferred_element_type=F32)


def _params(*sem, vmem=VMEM_LIMIT):
    return pltpu.CompilerParams(dimension_semantics=sem, vmem_limit_bytes=vmem)


def _ffn_kernel(h_hbm, gpre_ref, gpost_ref, wg_ref, wu_ref, wd_ref, o_hbm,
                h_ref, acc_ref, xn_ref, h_sem, o_sem):
    i, j = pl.program_id(0), pl.program_id(1)
    n_i, n_j = pl.num_programs(0), pl.num_programs(1)
    tm = h_ref.shape[0]
    slab = tm // FFN_SLABS

    def h_copy(tile, r):
        return pltpu.make_async_copy(h_hbm.at[pl.ds(tile * tm + r * slab, slab), :],
                                     h_ref.at[pl.ds(r * slab, slab), :], h_sem.at[r])

    def o_copy(tile, r):
        return pltpu.make_async_copy(acc_ref.at[pl.ds(r * slab, slab), :],
                                     o_hbm.at[pl.ds(tile * tm + r * slab, slab), :], o_sem.at[r])

    @pl.when(j == 0)
    def _():
        @pl.when(i == 0)
        def _():
            for r in range(FFN_SLABS):
                h_copy(0, r).start()

        for r in range(FFN_SLABS):
            rows = pl.ds(r * slab, slab)
            h_copy(i, r).wait()
            xn_ref[rows, :] = _rms(h_ref[rows, :], gpre_ref[...]).astype(BF16)

            @pl.when(i > 0)
            def _():
                o_copy(i - 1, r).wait()
            acc_ref[rows, :] = jnp.zeros((slab, acc_ref.shape[1]), F32)

    xn = xn_ref[...]
    gate = _dot(xn, wg_ref[...].astype(BF16))
    up = _dot(xn, wu_ref[...].astype(BF16))
    act = (jax.nn.silu(gate) * up).astype(BF16)
    acc_ref[...] += _dot(act, wd_ref[...].astype(BF16))

    @pl.when(j == n_j - 1)
    def _():
        for r in range(FFN_SLABS):
            for q in range(slab // FFN_EPI_ROWS):
                rows = pl.ds(r * slab + q * FFN_EPI_ROWS, FFN_EPI_ROWS)
                acc_ref[rows, :] = h_ref[rows, :] + 0.5 * _rms(acc_ref[rows, :], gpost_ref[...])
            o_copy(i, r).start()

            @pl.when(i + 1 < n_i)
            def _():
                h_copy(i + 1, r).start()

        @pl.when(i == n_i - 1)
        def _():
            for r in range(FFN_SLABS):
                o_copy(i, r).wait()


def _ffn(h, g_pre, g_post, wg, wu, wd, layer, half):
    tm, tf = FFN_ROWS, FF_TILE
    return pl.pallas_call(
        _ffn_kernel,
        grid=(SEQ // tm, D_FF // tf),
        in_specs=[
            pl.BlockSpec(memory_space=pl.ANY),
            pl.BlockSpec((1, D_MODEL), lambda i, j: (0, 0)),
            pl.BlockSpec((1, D_MODEL), lambda i, j: (0, 0)),
            pl.BlockSpec((None, None, D_MODEL, tf), lambda i, j: (layer, half, 0, j)),
            pl.BlockSpec((None, None, D_MODEL, tf), lambda i, j: (layer, half, 0, j)),
            pl.BlockSpec((None, None, tf, D_MODEL), lambda i, j: (layer, half, j, 0)),
        ],
        out_specs=pl.BlockSpec(memory_space=pl.ANY),
        out_shape=jax.ShapeDtypeStruct((SEQ, D_MODEL), F32),
        scratch_shapes=[
            pltpu.VMEM((tm, D_MODEL), F32),
            pltpu.VMEM((tm, D_MODEL), F32),
            pltpu.VMEM((tm, D_MODEL), BF16),
            pltpu.SemaphoreType.DMA((FFN_SLABS,)),
            pltpu.SemaphoreType.DMA((FFN_SLABS,)),
        ],
        compiler_params=_params("arbitrary", "arbitrary", vmem=FFN_VMEM_LIMIT),
        name="ffn",
    )(h, g_pre, g_post, wg, wu, wd)


def _ev_in_kernel(h_ref, g_ref, w_ref, wlr_ref, wg2_ref, bg_ref, p_ref, gk_ref, xn_ref):
    j = pl.program_id(1)

    @pl.when(j == 0)
    def _():
        xn = _rms(h_ref[...], g_ref[...]).astype(BF16)
        xn_ref[...] = xn
        lane = lax.broadcasted_iota(jnp.int32, (xn.shape[0], LANES), 1)
        glr = jnp.where(lane < B_GATE_RANK, _dot(xn, wlr_ref[...]), 0.0).astype(BF16)
        pre = _dot(glr, wg2_ref[...]) + bg_ref[...]
        gk_ref[...] = jax.nn.log_sigmoid(pre) * (1.0 / B_GATE_TAU)

    p_ref[...] = _dot(xn_ref[...], w_ref[...])


def _ev_in(h, g, w_in, w_g2, b_gate):
    tm, tn = EV_IN_ROWS, 1024
    return pl.pallas_call(
        _ev_in_kernel,
        grid=(SEQ // tm, EVEN_MAIN // tn),
        in_specs=[
            pl.BlockSpec((tm, D_MODEL), lambda i, j: (i, 0)),
            pl.BlockSpec((1, D_MODEL), lambda i, j: (0, 0)),
            pl.BlockSpec((D_MODEL, tn), lambda i, j: (0, j)),
            pl.BlockSpec((D_MODEL, LANES), lambda i, j: (0, EVEN_MAIN // LANES)),
            pl.BlockSpec((LANES, B_KEY), lambda i, j: (0, 0)),
            pl.BlockSpec((1, B_KEY), lambda i, j: (0, 0)),
        ],
        out_specs=[
            pl.BlockSpec((tm, tn), lambda i, j: (i, j)),
            pl.BlockSpec((tm, B_KEY), lambda i, j: (i, 0)),
        ],
        out_shape=[
            jax.ShapeDtypeStruct((SEQ, EVEN_MAIN), F32),
            jax.ShapeDtypeStruct((SEQ, B_KEY), F32),
        ],
        scratch_shapes=[pltpu.VMEM((tm, D_MODEL), BF16)],
        compiler_params=_params("parallel", "arbitrary"),
        name="ev_in",
    )(h, g, w_in, w_in, w_g2, b_gate)


def _ev_mix_kernel(za_ref, q_ref, k_ref, v_ref, r_ref, gk_ref, lng_ref, lnb_ref, ws_ref, bs_ref,
                   gng_ref, y_ref, u_ref, vln_ref, st_ref):
    tm = za_ref.shape[0]

    @pl.when(pl.program_id(0) == 0)
    def _():
        st_ref[...] = jnp.zeros_like(st_ref)

    z = jax.nn.gelu(za_ref[...])
    u_ref[...] = z[:, :A_WIDTH]
    va = z[:, A_WIDTH:]
    mu = jnp.mean(va, axis=-1, keepdims=True)
    vc = va - mu
    vln = vc * lax.rsqrt(jnp.mean(vc * vc, axis=-1, keepdims=True) + EPS)
    vln_ref[...] = (vln * lng_ref[...] + lnb_ref[...]).astype(BF16)

    n_a = tm // A_CHUNK
    row = lax.broadcasted_iota(jnp.int32, (A_CHUNK, A_CHUNK), 0)
    col = lax.broadcasted_iota(jnp.int32, (A_CHUNK, A_CHUNK), 1)
    for hd in range(A_HEADS):
        cs = slice(hd * A_HEAD_DIM, (hd + 1) * A_HEAD_DIM)
        w = jnp.where(row >= col, ws_ref[hd], 0.0).astype(BF16)
        rhs = jnp.concatenate(
            [vln_ref[c * A_CHUNK:(c + 1) * A_CHUNK, cs] for c in range(n_a)], axis=1)
        s = _dot(w, rhs)
        for c in range(n_a):
            rs = slice(c * A_CHUNK, (c + 1) * A_CHUNK)
            sc = s[:, c * A_HEAD_DIM:(c + 1) * A_HEAD_DIM] + bs_ref[:, cs]
            y_ref[rs, cs] = (u_ref[rs, cs] * sc).astype(BF16)

    n_b = tm // B_CHUNK
    r64 = lax.broadcasted_iota(jnp.int32, (B_CHUNK, B_CHUNK), 0)
    c64 = lax.broadcasted_iota(jnp.int32, (B_CHUNK, B_CHUNK), 1)
    causal = r64 >= c64
    tri = jnp.where(causal, 1.0, 0.0).astype(BF16)
    scale = B_DK ** -0.5
    for c in range(n_b):
        rs = slice(c * B_CHUNK, (c + 1) * B_CHUNK)
        g = gk_ref[rs, :]
        g_hi = g.astype(BF16)
        g_lo = (g - g_hi.astype(F32)).astype(BF16)
        bcum = _dot(tri, g_hi) + _dot(tri, g_lo)
        b_last = bcum[B_CHUNK - 1:B_CHUNK, :]
        q = q_ref[rs, :] * scale
        k = k_ref[rs, :]
        q_dec = (q * jnp.exp(bcum)).astype(BF16)
        k_inv = (k * jnp.exp(-bcum)).astype(BF16)
        k_end = (k * jnp.exp(b_last - bcum)).astype(BF16)
        decay = jnp.exp(b_last)
        for hd in range(B_HEADS):
            ks = slice(hd * B_DK, (hd + 1) * B_DK)
            vs = slice(hd * B_DV, (hd + 1) * B_DV)
            v = v_ref[rs, vs].astype(BF16)
            scores = jnp.where(causal, _dot_nt(q_dec[:, ks], k_inv[:, ks]), 0.0).astype(BF16)
            st = st_ref[hd]
            o = _dot(scores, v) + _dot_nt(q_dec[:, ks], st.astype(BF16))
            st_ref[hd] = decay[:, ks] * st + _dot_tn(v, k_end[:, ks])
            o = o * lax.rsqrt(jnp.mean(o * o, axis=-1, keepdims=True) + EPS) * gng_ref[...]
            y_ref[rs, A_WIDTH + hd * B_DV:A_WIDTH + (hd + 1) * B_DV] = (
                o * jax.nn.silu(r_ref[rs, vs])).astype(BF16)


def _ev_mix(p, gk, ln_g, ln_b, w_s, bs_full, gla_norm_g):
    tm = ROW_TILE
    return pl.pallas_call(
        _ev_mix_kernel,
        grid=(SEQ // tm,),
        in_specs=[
            pl.BlockSpec((tm, 2 * A_WIDTH), lambda i: (i, 0)),
            pl.BlockSpec((tm, B_KEY), lambda i: (i, 2 * A_WIDTH // B_KEY)),
            pl.BlockSpec((tm, B_KEY), lambda i: (i, 2 * A_WIDTH // B_KEY + 1)),
            pl.BlockSpec((tm, B_VAL), lambda i: (i, (2 * A_WIDTH + 2 * B_KEY) // B_VAL)),
            pl.BlockSpec((tm, B_VAL), lambda i: (i, (2 * A_WIDTH + 2 * B_KEY) // B_VAL + 1)),
            pl.BlockSpec((tm, B_KEY), lambda i: (i, 0)),
            pl.BlockSpec((1, A_WIDTH), lambda i: (0, 0)),
            pl.BlockSpec((1, A_WIDTH), lambda i: (0, 0)),
            pl.BlockSpec((A_HEADS, A_CHUNK, A_CHUNK), lambda i: (0, 0, 0)),
            pl.BlockSpec((A_CHUNK, A_WIDTH), lambda i: (0, 0)),
            pl.BlockSpec((1, B_DV), lambda i: (0, 0)),
        ],
        out_specs=pl.BlockSpec((tm, A_WIDTH + B_VAL), lambda i: (i, 0)),
        out_shape=jax.ShapeDtypeStruct((SEQ, A_WIDTH + B_VAL), BF16),
        scratch_shapes=[
            pltpu.VMEM((tm, A_WIDTH), F32),
            pltpu.VMEM((tm, A_WIDTH), BF16),
            pltpu.VMEM((B_HEADS, B_DV, B_DK), F32),
        ],
        compiler_params=_params("arbitrary"),
        name="ev_mix",
    )(p, p, p, p, p, gk, ln_g, ln_b, w_s, bs_full, gla_norm_g)


def _ev_out_kernel(y_ref, w_ref, g_ref, h_ref, o_ref):
    o_ref[...] = h_ref[...] + _rms(_dot(y_ref[...], w_ref[...]), g_ref[...])


def _ev_out(y, w, g, h):
    tm = ROW_TILE
    kdim = y.shape[1]
    return pl.pallas_call(
        _ev_out_kernel,
        grid=(SEQ // tm,),
        in_specs=[
            pl.BlockSpec((tm, kdim), lambda i: (i, 0)),
            pl.BlockSpec((kdim, D_MODEL), lambda i: (0, 0)),
            pl.BlockSpec((1, D_MODEL), lambda i: (0, 0)),
            pl.BlockSpec((tm, D_MODEL), lambda i: (i, 0)),
        ],
        out_specs=pl.BlockSpec((tm, D_MODEL), lambda i: (i, 0)),
        out_shape=jax.ShapeDtypeStruct((SEQ, D_MODEL), F32),
        compiler_params=_params("parallel"),
        name="ev_out",
    )(y, w, g, h)


def _od_in_kernel(h_ref, g_ref, wt_ref, ut_ref, utb_ref):
    ut = _dot_nt(wt_ref[...], _rms(h_ref[...], g_ref[...]).astype(BF16))
    ut_ref[...] = ut
    utb_ref[...] = ut.astype(BF16)


def _od_in(h, g, w_t):
    tm = ROW_TILE
    return pl.pallas_call(
        _od_in_kernel,
        grid=(SEQ // tm,),
        in_specs=[
            pl.BlockSpec((tm, D_MODEL), lambda i: (i, 0)),
            pl.BlockSpec((1, D_MODEL), lambda i: (0, 0)),
            pl.BlockSpec((C_WIDTH, D_MODEL), lambda i: (0, 0)),
        ],
        out_specs=[
            pl.BlockSpec((C_WIDTH, tm), lambda i: (0, i)),
            pl.BlockSpec((C_WIDTH, tm), lambda i: (0, i)),
        ],
        out_shape=[
            jax.ShapeDtypeStruct((C_WIDTH, SEQ), F32),
            jax.ShapeDtypeStruct((C_WIDTH, SEQ), BF16),
        ],
        compiler_params=_params("parallel"),
        name="od_in",
    )(h, g, w_t)


def _split3(x):
    hi = x.astype(BF16)
    r1 = x - hi.astype(F32)
    mid = r1.astype(BF16)
    lo = (r1 - mid.astype(F32)).astype(BF16)
    return hi, mid, lo


def _dot_f32(a, b):
    a0, a1, a2 = _split3(a)
    b0, b1, b2 = _split3(b)
    return (_dot(a0, b0) + (_dot(a0, b1) + _dot(a1, b0))
            + (_dot(a0, b2) + _dot(a1, b1) + _dot(a2, b0)))


def _lane_expand(x, e):
    x0, x1, x2 = _split3(x)
    return _dot(x0, e) + _dot(x1, e) + _dot(x2, e)


def _s5_group_operators(ar, ai, bbr, bbi, bbt, crt, cit):
    tau = lax.broadcasted_iota(jnp.int32, (C_STATE, LANES), 1) & (S5_CHUNK - 1)
    expand = (lax.broadcasted_iota(jnp.int32, (C_GROUP, S5_COLS), 1) // S5_CHUNK
              == lax.broadcasted_iota(jnp.int32, (C_GROUP, S5_COLS), 0)).astype(BF16)
    one = jnp.ones((C_STATE, LANES), F32)
    zero = jnp.zeros((C_STATE, LANES), F32)
    pr, pi = one, zero
    rr, ri = one, zero
    fr, fi = ar, ai
    for b in range(S5_CHUNK.bit_length() - 1):
        bit = ((tau >> b) & 1) == 1
        pr, pi = jnp.where(bit, pr * fr - pi * fi, pr), jnp.where(bit, pr * fi + pi * fr, pi)
        rr, ri = jnp.where(bit, rr, rr * fr - ri * fi), jnp.where(bit, ri, rr * fi + ri * fr)
        fr, fi = fr * fr - fi * fi, 2.0 * (fr * fi)
    wide = lambda a: jnp.concatenate([a] * (S5_COLS // LANES), axis=1)
    pr, pi, rr, ri, ar, ai = wide(pr), wide(pi), wide(rr), wide(ri), wide(ar), wide(ai)
    cr = _lane_expand(crt, expand)
    ci = _lane_expand(cit, expand)
    cer = cr * pr - ci * pi
    cei = cr * pi + ci * pr
    kv = _dot_f32(bbt, jnp.concatenate([cer, -cei], axis=0))
    qr = cer * ar - cei * ai
    qi = cer * ai + cei * ar
    q = jnp.concatenate([qr, -qi], axis=0).astype(BF16)
    br = _lane_expand(bbr, expand)
    bi = _lane_expand(bbi, expand)
    ptr = br * rr - bi * ri
    pti = br * ri + bi * rr
    pt = jnp.concatenate([ptr, pti], axis=0).astype(BF16)
    pts = jnp.concatenate([pti, ptr], axis=0).astype(BF16)
    return kv, pt, pts, q


def _s5_kernel(ut2_ref, ar_ref, ai_ref, bbr_ref, bbi_ref, bbt_ref, crt_ref, cit_ref,
               a1_ref, a2_ref, a2s_ref, yt2_ref, ut_ref, yt_ref, u_ref, m_ref, q_ref, inc_ref, incs_ref,
               xs_ref, y_ref):
    gb = u_ref.shape[0]
    w = 2 * C_STATE
    ut_ref[...] = ut2_ref[...].reshape(ut_ref.shape)
    per = LANES // S5_CHUNK
    nsup = S5_NC // per
    keep = ((lax.broadcasted_iota(jnp.int32, (S5_CHUNK, LANES), 1) & (S5_CHUNK - 1))
            >= lax.broadcasted_iota(jnp.int32, (S5_CHUNK, LANES), 0))
    for gi in range(gb):
        for cp in range(C_GROUP):
            for k in range(per):
                u_ref[gi, k * nsup:(k + 1) * nsup, cp * S5_CHUNK:(cp + 1) * S5_CHUNK] = (
                    ut_ref[gi * C_GROUP + cp, :, k * S5_CHUNK:(k + 1) * S5_CHUNK])
        kv, pt, pts, q = _s5_group_operators(ar_ref[gi], ai_ref[gi], bbr_ref[gi], bbi_ref[gi], bbt_ref[gi],
                                             crt_ref[gi], cit_ref[gi])
        q_ref[gi] = q
        for cp in range(C_GROUP):
            taps = jnp.broadcast_to(kv[cp:cp + 1, :], (S5_CHUNK, S5_COLS))
            for v in range(S5_COLS // LANES):
                tile = pltpu.roll(taps[:, v * LANES:(v + 1) * LANES], 0, 1, stride=1, stride_axis=0)
                m_ref[gi, cp * S5_CHUNK:(cp + 1) * S5_CHUNK, v * LANES:(v + 1) * LANES] = (
                    jnp.where(keep, tile, 0.0).astype(BF16))
        u = u_ref[gi]
        inc_ref[:, gi * w:(gi + 1) * w] = _dot_nt(u, pt)
        incs_ref[:, gi * w:(gi + 1) * w] = _dot_nt(u, pts)

    a1 = a1_ref[...]
    a2 = a2_ref[...]
    a2s = a2s_ref[...]

    def step(n, carry):
        x, xs = carry
        row = pl.ds((n & (per - 1)) * nsup + (n >> (per.bit_length() - 1)), 1)
        xs_ref[row, :] = x
        x_new = a1 * x + a2 * xs + inc_ref[row, :]
        xs_new = a1 * xs + a2s * x + incs_ref[row, :]
        return x_new, xs_new

    zero = jnp.zeros((1, gb * w), F32)
    lax.fori_loop(0, S5_NC, step, (zero, zero))

    for gi in range(gb):
        xst = xs_ref[:, gi * w:(gi + 1) * w].astype(BF16)
        y_ref[...] = _dot(u_ref[gi], m_ref[gi]) + _dot(xst, q_ref[gi])
        for c in range(C_GROUP):
            for k in range(per):
                yt_ref[gi * C_GROUP + c, :, k * S5_CHUNK:(k + 1) * S5_CHUNK] = (
                    y_ref[k * nsup:(k + 1) * nsup, c * S5_CHUNK:(c + 1) * S5_CHUNK])
    yt2_ref[...] = yt_ref[...].reshape(yt2_ref.shape)


def _s5(u_tb, prm):
    gb = S5_GB
    w = 2 * C_STATE
    grp3 = lambda i: (i, 0, 0)
    return pl.pallas_call(
        _s5_kernel,
        grid=(C_GROUPS // gb,),
        in_specs=[
            pl.BlockSpec((gb * C_GROUP, SEQ), lambda i: (i, 0)),
            pl.BlockSpec((gb, C_STATE, LANES), grp3),
            pl.BlockSpec((gb, C_STATE, LANES), grp3),
            pl.BlockSpec((gb, C_STATE, C_GROUP), grp3),
            pl.BlockSpec((gb, C_STATE, C_GROUP), grp3),
            pl.BlockSpec((gb, C_GROUP, w), grp3),
            pl.BlockSpec((gb, C_STATE, C_GROUP), grp3),
            pl.BlockSpec((gb, C_STATE, C_GROUP), grp3),
            pl.BlockSpec((1, gb * w), lambda i: (0, i)),
            pl.BlockSpec((1, gb * w), lambda i: (0, i)),
            pl.BlockSpec((1, gb * w), lambda i: (0, i)),
        ],
        out_specs=pl.BlockSpec((gb * C_GROUP, SEQ), lambda i: (i, 0)),
        out_shape=jax.ShapeDtypeStruct((C_WIDTH, SEQ), F32),
        scratch_shapes=[
            pltpu.VMEM((gb * C_GROUP, SEQ // LANES, LANES), BF16),
            pltpu.VMEM((gb * C_GROUP, SEQ // LANES, LANES), F32),
            pltpu.VMEM((gb, S5_NC, S5_COLS), BF16),
            pltpu.VMEM((gb, S5_COLS, S5_COLS), BF16),
            pltpu.VMEM((gb, w, S5_COLS), BF16),
            pltpu.VMEM((S5_NC, gb * w), F32),
            pltpu.VMEM((S5_NC, gb * w), F32),
            pltpu.VMEM((S5_NC, gb * w), F32),
            pltpu.VMEM((S5_NC, S5_COLS), F32),
        ],
        compiler_params=_params("parallel"),
        name="s5",
    )(u_tb, *prm)


def _s5_params(lam_re, lam_im, log_dt, b_re, b_im, c_re, c_im):
    lr = jnp.minimum(lam_re, -1e-4)
    li = lam_im
    dt = jnp.exp(log_dt)[:, None]
    mag = jnp.exp(lr * dt)
    ar = mag * jnp.cos(li * dt)
    ai = mag * jnp.sin(li * dt)
    den = lr * lr + li * li
    nr = ar - 1.0
    cr = (nr * lr + ai * li) / den
    ci = (ai * lr - nr * li) / den
    bbr = cr[..., None] * b_re - ci[..., None] * b_im
    bbi = cr[..., None] * b_im + ci[..., None] * b_re
    bbt = jnp.concatenate([bbr, bbi], axis=1).transpose(0, 2, 1)
    atr, ati = ar, ai
    for _ in range(S5_CHUNK.bit_length() - 1):
        atr, ati = atr * atr - ati * ati, 2.0 * (atr * ati)
    a1 = jnp.concatenate([atr, atr], axis=-1).reshape(1, -1)
    a2 = jnp.concatenate([-ati, ati], axis=-1).reshape(1, -1)
    a2s = jnp.concatenate([ati, -ati], axis=-1).reshape(1, -1)
    bcast = lambda a: jnp.broadcast_to(a[..., None], a.shape + (LANES,))
    return (bcast(ar), bcast(ai), bbr, bbi, bbt, c_re.transpose(0, 2, 1), c_im.transpose(0, 2, 1),
            a1, a2, a2s)


def _od_out_kernel(yt_ref, ut_ref, d_ref, wglut_ref, bglu_ref, wout_ref, g_ref, h_ref, o_ref):
    y = yt_ref[...] + d_ref[...] * ut_ref[...]
    z = jax.nn.gelu(y)
    gate = jax.nn.sigmoid(_dot(wglut_ref[...], z.astype(BF16)) + bglu_ref[...])
    m = _dot_tn((z * gate).astype(BF16), wout_ref[...])
    o_ref[...] = h_ref[...] + _rms(m, g_ref[...])


def _od_out(y_t, u_t, d, w_glu_t, b_glu, w_out, g, h):
    tm = ROW_TILE
    return pl.pallas_call(
        _od_out_kernel,
        grid=(SEQ // tm,),
        in_specs=[
            pl.BlockSpec((C_WIDTH, tm), lambda i: (0, i)),
            pl.BlockSpec((C_WIDTH, tm), lambda i: (0, i)),
            pl.BlockSpec((C_WIDTH, 1), lambda i: (0, 0)),
            pl.BlockSpec((C_WIDTH, C_WIDTH), lambda i: (0, 0)),
            pl.BlockSpec((C_WIDTH, 1), lambda i: (0, 0)),
            pl.BlockSpec((C_WIDTH, D_MODEL), lambda i: (0, 0)),
            pl.BlockSpec((1, D_MODEL), lambda i: (0, 0)),
            pl.BlockSpec((tm, D_MODEL), lambda i: (i, 0)),
        ],
        out_specs=pl.BlockSpec((tm, D_MODEL), lambda i: (i, 0)),
        out_shape=jax.ShapeDtypeStruct((SEQ, D_MODEL), F32),
        compiler_params=_params("parallel"),
        name="od_out",
    )(y_t, u_t, d, w_glu_t, b_glu, w_out, g, h)


def _row(v):
    return v.reshape(1, -1).astype(F32)


def kernel(x, norm_g, ffn_w_gate, ffn_w_up, ffn_w_down, ev_w_in, ev_ln_g, ev_ln_b, ev_w_s, ev_b_s, ev_w_gate2, ev_b_gate, ev_gla_norm_g, ev_w_out, od_w_in, od_lam_re, od_lam_im, od_log_dt, od_b_re, od_b_im, od_c_re, od_c_im, od_d, od_w_glu, od_b_glu, od_w_out):
    depth = norm_g.shape[0]
    h = x.reshape(SEQ, D_MODEL)
    for l in range(depth):
        i = l // 2
        g = norm_g[l]
        h = _ffn(h, _row(g[0]), _row(g[1]), ffn_w_gate, ffn_w_up, ffn_w_down, l, 0)
        if l % 2 == 0:
            w_g2 = jnp.pad(ev_w_gate2[i], ((0, LANES - B_GATE_RANK), (0, 0))).astype(BF16)
            p, gk = _ev_in(h, _row(g[2]), ev_w_in[i].astype(BF16), w_g2, _row(ev_b_gate[i]))
            bs_full = jnp.repeat(ev_b_s[i].T, A_HEAD_DIM, axis=1).astype(F32)
            y = _ev_mix(p, gk, _row(ev_ln_g[i]), _row(ev_ln_b[i]), ev_w_s[i], bs_full,
                        _row(ev_gla_norm_g[i]))
            h = _ev_out(y, ev_w_out[i].astype(BF16), _row(g[3]), h)
        else:
            u_t, u_tb = _od_in(h, _row(g[2]), od_w_in[i].T.astype(BF16))
            prm = _s5_params(od_lam_re[i], od_lam_im[i], od_log_dt[i], od_b_re[i], od_b_im[i],
                             od_c_re[i], od_c_im[i])
            y_t = _s5(u_tb, prm)
            h = _od_out(y_t, u_t, od_d[i].reshape(C_WIDTH, 1), od_w_glu[i].T.astype(BF16),
                        od_b_glu[i].reshape(C_WIDTH, 1), od_w_out[i].astype(BF16), _row(g[3]), h)
        h = _ffn(h, _row(g[4]), _row(g[5]), ffn_w_gate, ffn_w_up, ffn_w_down, l, 1)
    return h.reshape(x.shape)
```

```python
import jax
import jax.numpy as jnp
from jax import lax
from jax.experimental import pallas as pl
from jax.experimental.pallas import tpu as pltpu

F32 = jnp.float32
BF16 = jnp.bfloat16

D_MODEL = 2048
SEQ = 8192
D_FF = 5632
EPS = 1e-6

A_HEADS = 8
A_HEAD_DIM = 128
A_WIDTH = 1024
A_CHUNK = 128
B_HEADS = 4
B_DK = 128
B_DV = 256
B_KEY = 512
B_VAL = 1024
B_GATE_RANK = 16
B_GATE_TAU = 16.0
B_CHUNK = 64
EVEN_MAIN = 2 * A_WIDTH + 2 * B_KEY + 2 * B_VAL
C_WIDTH = 1024
C_GROUP = 16
C_GROUPS = 64
C_STATE = 64

LANES = 128
VMEM_LIMIT = 56 * 1024 * 1024
FFN_VMEM_LIMIT = 60 * 1024 * 1024

ROW_TILE = 512
EV_IN_ROWS = 1024
FFN_ROWS = 1024
FFN_SLABS = 8
FFN_EPI_ROWS = 16
FF_TILE = 512
S5_CHUNK = 32
S5_NC = SEQ // S5_CHUNK
S5_COLS = C_GROUP * S5_CHUNK
S5_GB = 8


def _rms(x, g):
    return x * lax.rsqrt(jnp.mean(x * x, axis=-1, keepdims=True) + EPS) * g


def _dot(a, b):
    return jnp.dot(a, b, preferred_element_type=F32)


def _dot_nt(a, b):
    return lax.dot_general(a, b, (((1,), (1,)), ((), ())), preferred_element_type=F32)


def _dot_tn(a, b):
    return lax.dot_general(a, b, (((0,), (0,)), ((), ())), preferred_element_type=F32)


def _params(*sem, vmem=VMEM_LIMIT):
    return pltpu.CompilerParams(dimension_semantics=sem, vmem_limit_bytes=vmem)


def _ffn_kernel(h_hbm, gpre_ref, gpost_ref, wg_ref, wu_ref, wd_ref, o_hbm,
                h_ref, acc_ref, xn_ref, h_sem, o_sem):
    i, j = pl.program_id(0), pl.program_id(1)
    n_i, n_j = pl.num_programs(0), pl.num_programs(1)
    tm = h_ref.shape[0]
    slab = tm // FFN_SLABS

    def h_copy(tile, r):
        return pltpu.make_async_copy(h_hbm.at[pl.ds(tile * tm + r * slab, slab), :],
                                     h_ref.at[pl.ds(r * slab, slab), :], h_sem.at[r])

    def o_copy(tile, r):
        return pltpu.make_async_copy(acc_ref.at[pl.ds(r * slab, slab), :],
                                     o_hbm.at[pl.ds(tile * tm + r * slab, slab), :], o_sem.at[r])

    @pl.when(j == 0)
    def _():
        @pl.when(i == 0)
        def _():
            for r in range(FFN_SLABS):
                h_copy(0, r).start()

        for r in range(FFN_SLABS):
            rows = pl.ds(r * slab, slab)
            h_copy(i, r).wait()
            xn_ref[rows, :] = _rms(h_ref[rows, :], gpre_ref[...]).astype(BF16)

            @pl.when(i > 0)
            def _():
                o_copy(i - 1, r).wait()
            acc_ref[rows, :] = jnp.zeros((slab, acc_ref.shape[1]), F32)

    xn = xn_ref[...]
    gate = _dot(xn, wg_ref[...].astype(BF16))
    up = _dot(xn, wu_ref[...].astype(BF16))
    act = (jax.nn.silu(gate) * up).astype(BF16)
    acc_ref[...] += _dot(act, wd_ref[...].astype(BF16))

    @pl.when(j == n_j - 1)
    def _():
        for r in range(FFN_SLABS):
            for q in range(slab // FFN_EPI_ROWS):
                rows = pl.ds(r * slab + q * FFN_EPI_ROWS, FFN_EPI_ROWS)
                acc_ref[rows, :] = h_ref[rows, :] + 0.5 * _rms(acc_ref[rows, :], gpost_ref[...])
            o_copy(i, r).start()

            @pl.when(i + 1 < n_i)
            def _():
                h_copy(i + 1, r).start()

        @pl.when(i == n_i - 1)
        def _():
            for r in range(FFN_SLABS):
                o_copy(i, r).wait()


def _ffn(h, g_pre, g_post, wg, wu, wd, layer, half):
    tm, tf = FFN_ROWS, FF_TILE
    return pl.pallas_call(
        _ffn_kernel,
        grid=(SEQ // tm, D_FF // tf),
        in_specs=[
            pl.BlockSpec(memory_space=pl.ANY),
            pl.BlockSpec((1, D_MODEL), lambda i, j: (0, 0)),
            pl.BlockSpec((1, D_MODEL), lambda i, j: (0, 0)),
            pl.BlockSpec((None, None, D_MODEL, tf), lambda i, j: (layer, half, 0, j)),
            pl.BlockSpec((None, None, D_MODEL, tf), lambda i, j: (layer, half, 0, j)),
            pl.BlockSpec((None, None, tf, D_MODEL), lambda i, j: (layer, half, j, 0)),
        ],
        out_specs=pl.BlockSpec(memory_space=pl.ANY),
        out_shape=jax.ShapeDtypeStruct((SEQ, D_MODEL), F32),
        scratch_shapes=[
            pltpu.VMEM((tm, D_MODEL), F32),
            pltpu.VMEM((tm, D_MODEL), F32),
            pltpu.VMEM((tm, D_MODEL), BF16),
            pltpu.SemaphoreType.DMA((FFN_SLABS,)),
            pltpu.SemaphoreType.DMA((FFN_SLABS,)),
        ],
        compiler_params=_params("arbitrary", "arbitrary", vmem=FFN_VMEM_LIMIT),
        name="ffn",
    )(h, g_pre, g_post, wg, wu, wd)


def _ev_in_kernel(h_ref, g_ref, w_ref, wlr_ref, wg2_ref, bg_ref, p_ref, gk_ref, xn_ref):
    j = pl.program_id(1)

    @pl.when(j == 0)
    def _():
        xn = _rms(h_ref[...], g_ref[...]).astype(BF16)
        xn_ref[...] = xn
        lane = lax.broadcasted_iota(jnp.int32, (xn.shape[0], LANES), 1)
        glr = jnp.where(lane < B_GATE_RANK, _dot(xn, wlr_ref[...].astype(BF16)), 0.0).astype(BF16)
        pre = _dot(glr, wg2_ref[...]) + bg_ref[...]
        gk_ref[...] = jax.nn.log_sigmoid(pre) * (1.0 / B_GATE_TAU)

    p_ref[...] = _dot(xn_ref[...], w_ref[...].astype(BF16))


def _ev_in(h, g, w_in, w_g2, b_gate, layer):
    tm, tn = EV_IN_ROWS, 1024
    return pl.pallas_call(
        _ev_in_kernel,
        grid=(SEQ // tm, EVEN_MAIN // tn),
        in_specs=[
            pl.BlockSpec((tm, D_MODEL), lambda i, j: (i, 0)),
            pl.BlockSpec((1, D_MODEL), lambda i, j: (0, 0)),
            pl.BlockSpec((None, D_MODEL, tn), lambda i, j: (layer, 0, j)),
            pl.BlockSpec((None, D_MODEL, LANES), lambda i, j: (layer, 0, EVEN_MAIN // LANES)),
            pl.BlockSpec((LANES, B_KEY), lambda i, j: (0, 0)),
            pl.BlockSpec((1, B_KEY), lambda i, j: (0, 0)),
        ],
        out_specs=[
            pl.BlockSpec((tm, tn), lambda i, j: (i, j)),
            pl.BlockSpec((tm, B_KEY), lambda i, j: (i, 0)),
        ],
        out_shape=[
            jax.ShapeDtypeStruct((SEQ, EVEN_MAIN), F32),
            jax.ShapeDtypeStruct((SEQ, B_KEY), F32),
        ],
        scratch_shapes=[pltpu.VMEM((tm, D_MODEL), BF16)],
        compiler_params=_params("parallel", "arbitrary"),
        name="ev_in",
    )(h, g, w_in, w_in, w_g2, b_gate)


def _ev_mix_kernel(za_ref, q_ref, k_ref, v_ref, r_ref, gk_ref, lng_ref, lnb_ref, ws_ref, bs_ref,
                   gng_ref, y_ref, u_ref, vln_ref, st_ref):
    tm = za_ref.shape[0]

    @pl.when(pl.program_id(0) == 0)
    def _():
        st_ref[...] = jnp.zeros_like(st_ref)

    z = jax.nn.gelu(za_ref[...])
    u_ref[...] = z[:, :A_WIDTH]
    va = z[:, A_WIDTH:]
    mu = jnp.mean(va, axis=-1, keepdims=True)
    vc = va - mu
    vln = vc * lax.rsqrt(jnp.mean(vc * vc, axis=-1, keepdims=True) + EPS)
    vln_ref[...] = (vln * lng_ref[...] + lnb_ref[...]).astype(BF16)

    n_a = tm // A_CHUNK
    row = lax.broadcasted_iota(jnp.int32, (A_CHUNK, A_CHUNK), 0)
    col = lax.broadcasted_iota(jnp.int32, (A_CHUNK, A_CHUNK), 1)
    for hd in range(A_HEADS):
        cs = slice(hd * A_HEAD_DIM, (hd + 1) * A_HEAD_DIM)
        w = jnp.where(row >= col, ws_ref[hd], 0.0).astype(BF16)
        rhs = jnp.concatenate(
            [vln_ref[c * A_CHUNK:(c + 1) * A_CHUNK, cs] for c in range(n_a)], axis=1)
        s = _dot(w, rhs)
        for c in range(n_a):
            rs = slice(c * A_CHUNK, (c + 1) * A_CHUNK)
            sc = s[:, c * A_HEAD_DIM:(c + 1) * A_HEAD_DIM] + bs_ref[:, cs]
            y_ref[rs, cs] = (u_ref[rs, cs] * sc).astype(BF16)

    n_b = tm // B_CHUNK
    r64 = lax.broadcasted_iota(jnp.int32, (B_CHUNK, B_CHUNK), 0)
    c64 = lax.broadcasted_iota(jnp.int32, (B_CHUNK, B_CHUNK), 1)
    causal = r64 >= c64
    tri = jnp.where(causal, 1.0, 0.0).astype(BF16)
    scale = B_DK ** -0.5
    for c in range(n_b):
        rs = slice(c * B_CHUNK, (c + 1) * B_CHUNK)
        g = gk_ref[rs, :]
        g_hi = g.astype(BF16)
        g_lo = (g - g_hi.astype(F32)).astype(BF16)
        bcum = _dot(tri, g_hi) + _dot(tri, g_lo)
        b_last = bcum[B_CHUNK - 1:B_CHUNK, :]
        q = q_ref[rs, :] * scale
        k = k_ref[rs, :]
        q_dec = (q * jnp.exp(bcum)).astype(BF16)
        k_inv = (k * jnp.exp(-bcum)).astype(BF16)
        k_end = (k * jnp.exp(b_last - bcum)).astype(BF16)
        decay = jnp.exp(b_last)
        for hd in range(B_HEADS):
            ks = slice(hd * B_DK, (hd + 1) * B_DK)
            vs = slice(hd * B_DV, (hd + 1) * B_DV)
            v = v_ref[rs, vs].astype(BF16)
            scores = jnp.where(causal, _dot_nt(q_dec[:, ks], k_inv[:, ks]), 0.0).astype(BF16)
            st = st_ref[hd]
            o = _dot(scores, v) + _dot_nt(q_dec[:, ks], st.astype(BF16))
            st_ref[hd] = decay[:, ks] * st + _dot_tn(v, k_end[:, ks])
            o = o * lax.rsqrt(jnp.mean(o * o, axis=-1, keepdims=True) + EPS) * gng_ref[...]
            y_ref[rs, A_WIDTH + hd * B_DV:A_WIDTH + (hd + 1) * B_DV] = (
                o * jax.nn.silu(r_ref[rs, vs])).astype(BF16)


def _ev_mix(p, gk, ln_g, ln_b, w_s, bs_full, gla_norm_g):
    tm = ROW_TILE
    return pl.pallas_call(
        _ev_mix_kernel,
        grid=(SEQ // tm,),
        in_specs=[
            pl.BlockSpec((tm, 2 * A_WIDTH), lambda i: (i, 0)),
            pl.BlockSpec((tm, B_KEY), lambda i: (i, 2 * A_WIDTH // B_KEY)),
            pl.BlockSpec((tm, B_KEY), lambda i: (i, 2 * A_WIDTH // B_KEY + 1)),
            pl.BlockSpec((tm, B_VAL), lambda i: (i, (2 * A_WIDTH + 2 * B_KEY) // B_VAL)),
            pl.BlockSpec((tm, B_VAL), lambda i: (i, (2 * A_WIDTH + 2 * B_KEY) // B_VAL + 1)),
            pl.BlockSpec((tm, B_KEY), lambda i: (i, 0)),
            pl.BlockSpec((1, A_WIDTH), lambda i: (0, 0)),
            pl.BlockSpec((1, A_WIDTH), lambda i: (0, 0)),
            pl.BlockSpec((A_HEADS, A_CHUNK, A_CHUNK), lambda i: (0, 0, 0)),
            pl.BlockSpec((A_CHUNK, A_WIDTH), lambda i: (0, 0)),
            pl.BlockSpec((1, B_DV), lambda i: (0, 0)),
        ],
        out_specs=pl.BlockSpec((tm, A_WIDTH + B_VAL), lambda i: (i, 0)),
        out_shape=jax.ShapeDtypeStruct((SEQ, A_WIDTH + B_VAL), BF16),
        scratch_shapes=[
            pltpu.VMEM((tm, A_WIDTH), F32),
            pltpu.VMEM((tm, A_WIDTH), BF16),
            pltpu.VMEM((B_HEADS, B_DV, B_DK), F32),
        ],
        compiler_params=_params("arbitrary"),
        name="ev_mix",
    )(p, p, p, p, p, gk, ln_g, ln_b, w_s, bs_full, gla_norm_g)


def _ev_out_kernel(y_ref, w_ref, g_ref, h_ref, o_ref):
    o_ref[...] = h_ref[...] + _rms(_dot(y_ref[...], w_ref[...]), g_ref[...])


def _ev_out(y, w, g, h):
    tm = ROW_TILE
    kdim = y.shape[1]
    return pl.pallas_call(
        _ev_out_kernel,
        grid=(SEQ // tm,),
        in_specs=[
            pl.BlockSpec((tm, kdim), lambda i: (i, 0)),
            pl.BlockSpec((kdim, D_MODEL), lambda i: (0, 0)),
            pl.BlockSpec((1, D_MODEL), lambda i: (0, 0)),
            pl.BlockSpec((tm, D_MODEL), lambda i: (i, 0)),
        ],
        out_specs=pl.BlockSpec((tm, D_MODEL), lambda i: (i, 0)),
        out_shape=jax.ShapeDtypeStruct((SEQ, D_MODEL), F32),
        compiler_params=_params("parallel"),
        name="ev_out",
    )(y, w, g, h)


def _od_in_kernel(h_ref, g_ref, wt_ref, ut_ref, utb_ref):
    ut = _dot_nt(wt_ref[...], _rms(h_ref[...], g_ref[...]).astype(BF16))
    ut_ref[...] = ut
    utb_ref[...] = ut.astype(BF16)


def _od_in(h, g, w_t):
    tm = ROW_TILE
    return pl.pallas_call(
        _od_in_kernel,
        grid=(SEQ // tm,),
        in_specs=[
            pl.BlockSpec((tm, D_MODEL), lambda i: (i, 0)),
            pl.BlockSpec((1, D_MODEL), lambda i: (0, 0)),
            pl.BlockSpec((C_WIDTH, D_MODEL), lambda i: (0, 0)),
        ],
        out_specs=[
            pl.BlockSpec((C_WIDTH, tm), lambda i: (0, i)),
            pl.BlockSpec((C_WIDTH, tm), lambda i: (0, i)),
        ],
        out_shape=[
            jax.ShapeDtypeStruct((C_WIDTH, SEQ), F32),
            jax.ShapeDtypeStruct((C_WIDTH, SEQ), BF16),
        ],
        compiler_params=_params("parallel"),
        name="od_in",
    )(h, g, w_t)


def _split3(x):
    hi = x.astype(BF16)
    r1 = x - hi.astype(F32)
    mid = r1.astype(BF16)
    lo = (r1 - mid.astype(F32)).astype(BF16)
    return hi, mid, lo


def _dot_f32(a, b):
    a0, a1, a2 = _split3(a)
    b0, b1, b2 = _split3(b)
    return (_dot(a0, b0) + (_dot(a0, b1) + _dot(a1, b0))
            + (_dot(a0, b2) + _dot(a1, b1) + _dot(a2, b0)))


def _lane_expand(x, e):
    x0, x1, x2 = _split3(x)
    return _dot(x0, e) + _dot(x1, e) + _dot(x2, e)


def _s5_group_operators(ar, ai, bbr, bbi, bbt, crt, cit):
    tau = lax.broadcasted_iota(jnp.int32, (C_STATE, LANES), 1) & (S5_CHUNK - 1)
    expand = (lax.broadcasted_iota(jnp.int32, (C_GROUP, S5_COLS), 1) // S5_CHUNK
              == lax.broadcasted_iota(jnp.int32, (C_GROUP, S5_COLS), 0)).astype(BF16)
    one = jnp.ones((C_STATE, LANES), F32)
    zero = jnp.zeros((C_STATE, LANES), F32)
    pr, pi = one, zero
    rr, ri = one, zero
    fr, fi = ar, ai
    for b in range(S5_CHUNK.bit_length() - 1):
        bit = ((tau >> b) & 1) == 1
        pr, pi = jnp.where(bit, pr * fr - pi * fi, pr), jnp.where(bit, pr * fi + pi * fr, pi)
        rr, ri = jnp.where(bit, rr, rr * fr - ri * fi), jnp.where(bit, ri, rr * fi + ri * fr)
        fr, fi = fr * fr - fi * fi, 2.0 * (fr * fi)
    wide = lambda a: jnp.concatenate([a] * (S5_COLS // LANES), axis=1)
    pr, pi, rr, ri, ar, ai = wide(pr), wide(pi), wide(rr), wide(ri), wide(ar), wide(ai)
    cr = _lane_expand(crt, expand)
    ci = _lane_expand(cit, expand)
    cer = cr * pr - ci * pi
    cei = cr * pi + ci * pr
    kv = _dot_f32(bbt, jnp.concatenate([cer, -cei], axis=0))
    qr = cer * ar - cei * ai
    qi = cer * ai + cei * ar
    q = jnp.concatenate([qr, -qi], axis=0).astype(BF16)
    br = _lane_expand(bbr, expand)
    bi = _lane_expand(bbi, expand)
    ptr = br * rr - bi * ri
    pti = br * ri + bi * rr
    pt = jnp.concatenate([ptr, pti], axis=0).astype(BF16)
    pts = jnp.concatenate([pti, ptr], axis=0).astype(BF16)
    return kv, pt, pts, q


def _s5_kernel(ut2_ref, ar_ref, ai_ref, bbr_ref, bbi_ref, bbt_ref, crt_ref, cit_ref,
               a1_ref, a2_ref, a2s_ref, yt2_ref, ut_ref, yt_ref, u_ref, m_ref, q_ref, inc_ref, incs_ref,
               xs_ref, y_ref):
    gb = u_ref.shape[0]
    w = 2 * C_STATE
    ut_ref[...] = ut2_ref[...].reshape(ut_ref.shape)
    per = LANES // S5_CHUNK
    nsup = S5_NC // per
    keep = ((lax.broadcasted_iota(jnp.int32, (S5_CHUNK, LANES), 1) & (S5_CHUNK - 1))
            >= lax.broadcasted_iota(jnp.int32, (S5_CHUNK, LANES), 0))
    for gi in range(gb):
        for cp in range(C_GROUP):
            for k in range(per):
                u_ref[gi, k * nsup:(k + 1) * nsup, cp * S5_CHUNK:(cp + 1) * S5_CHUNK] = (
                    ut_ref[gi * C_GROUP + cp, :, k * S5_CHUNK:(k + 1) * S5_CHUNK])
        kv, pt, pts, q = _s5_group_operators(ar_ref[gi], ai_ref[gi], bbr_ref[gi], bbi_ref[gi], bbt_ref[gi],
                                             crt_ref[gi], cit_ref[gi])
        q_ref[gi] = q
        for cp in range(C_GROUP):
            taps = jnp.broadcast_to(kv[cp:cp + 1, :], (S5_CHUNK, S5_COLS))
            for v in range(S5_COLS // LANES):
                tile = pltpu.roll(taps[:, v * LANES:(v + 1) * LANES], 0, 1, stride=1, stride_axis=0)
                m_ref[gi, cp * S5_CHUNK:(cp + 1) * S5_CHUNK, v * LANES:(v + 1) * LANES] = (
                    jnp.where(keep, tile, 0.0).astype(BF16))
        u = u_ref[gi]
        inc_ref[:, gi * w:(gi + 1) * w] = _dot_nt(u, pt)
        incs_ref[:, gi * w:(gi + 1) * w] = _dot_nt(u, pts)

    a1 = a1_ref[...]
    a2 = a2_ref[...]
    a2s = a2s_ref[...]

    def step(n, carry):
        x, xs = carry
        row = pl.ds((n & (per - 1)) * nsup + (n >> (per.bit_length() - 1)), 1)
        xs_ref[row, :] = x
        x_new = a1 * x + a2 * xs + inc_ref[row, :]
        xs_new = a1 * xs + a2s * x + incs_ref[row, :]
        return x_new, xs_new

    zero = jnp.zeros((1, gb * w), F32)
    lax.fori_loop(0, S5_NC, step, (zero, zero))

    for gi in range(gb):
        xst = xs_ref[:, gi * w:(gi + 1) * w].astype(BF16)
        y_ref[...] = _dot(u_ref[gi], m_ref[gi]) + _dot(xst, q_ref[gi])
        for c in range(C_GROUP):
            for k in range(per):
                yt_ref[gi * C_GROUP + c, :, k * S5_CHUNK:(k + 1) * S5_CHUNK] = (
                    y_ref[k * nsup:(k + 1) * nsup, c * S5_CHUNK:(c + 1) * S5_CHUNK])
    yt2_ref[...] = yt_ref[...].reshape(yt2_ref.shape)


def _s5(u_tb, prm):
    gb = S5_GB
    w = 2 * C_STATE
    grp3 = lambda i: (i, 0, 0)
    return pl.pallas_call(
        _s5_kernel,
        grid=(C_GROUPS // gb,),
        in_specs=[
            pl.BlockSpec((gb * C_GROUP, SEQ), lambda i: (i, 0)),
            pl.BlockSpec((gb, C_STATE, LANES), grp3),
            pl.BlockSpec((gb, C_STATE, LANES), grp3),
            pl.BlockSpec((gb, C_STATE, C_GROUP), grp3),
            pl.BlockSpec((gb, C_STATE, C_GROUP), grp3),
            pl.BlockSpec((gb, C_GROUP, w), grp3),
            pl.BlockSpec((gb, C_STATE, C_GROUP), grp3),
            pl.BlockSpec((gb, C_STATE, C_GROUP), grp3),
            pl.BlockSpec((1, gb * w), lambda i: (0, i)),
            pl.BlockSpec((1, gb * w), lambda i: (0, i)),
            pl.BlockSpec((1, gb * w), lambda i: (0, i)),
        ],
        out_specs=pl.BlockSpec((gb * C_GROUP, SEQ), lambda i: (i, 0)),
        out_shape=jax.ShapeDtypeStruct((C_WIDTH, SEQ), F32),
        scratch_shapes=[
            pltpu.VMEM((gb * C_GROUP, SEQ // LANES, LANES), BF16),
            pltpu.VMEM((gb * C_GROUP, SEQ // LANES, LANES), F32),
            pltpu.VMEM((gb, S5_NC, S5_COLS), BF16),
            pltpu.VMEM((gb, S5_COLS, S5_COLS), BF16),
            pltpu.VMEM((gb, w, S5_COLS), BF16),
            pltpu.VMEM((S5_NC, gb * w), F32),
            pltpu.VMEM((S5_NC, gb * w), F32),
            pltpu.VMEM((S5_NC, gb * w), F32),
            pltpu.VMEM((S5_NC, S5_COLS), F32),
        ],
        compiler_params=_params("parallel"),
        name="s5",
    )(u_tb, *prm)


def _s5_params(lam_re, lam_im, log_dt, b_re, b_im, c_re, c_im):
    lr = jnp.minimum(lam_re, -1e-4)
    li = lam_im
    dt = jnp.exp(log_dt)[:, None]
    mag = jnp.exp(lr * dt)
    ar = mag * jnp.cos(li * dt)
    ai = mag * jnp.sin(li * dt)
    den = lr * lr + li * li
    nr = ar - 1.0
    cr = (nr * lr + ai * li) / den
    ci = (ai * lr - nr * li) / den
    bbr = cr[..., None] * b_re - ci[..., None] * b_im
    bbi = cr[..., None] * b_im + ci[..., None] * b_re
    bbt = jnp.concatenate([bbr, bbi], axis=1).transpose(0, 2, 1)
    atr, ati = ar, ai
    for _ in range(S5_CHUNK.bit_length() - 1):
        atr, ati = atr * atr - ati * ati, 2.0 * (atr * ati)
    a1 = jnp.concatenate([atr, atr], axis=-1).reshape(1, -1)
    a2 = jnp.concatenate([-ati, ati], axis=-1).reshape(1, -1)
    a2s = jnp.concatenate([ati, -ati], axis=-1).reshape(1, -1)
    bcast = lambda a: jnp.broadcast_to(a[..., None], a.shape + (LANES,))
    return (bcast(ar), bcast(ai), bbr, bbi, bbt, c_re.transpose(0, 2, 1), c_im.transpose(0, 2, 1),
            a1, a2, a2s)


def _od_out_kernel(yt_ref, ut_ref, d_ref, wglut_ref, bglu_ref, wout_ref, g_ref, h_ref, o_ref):
    y = yt_ref[...] + d_ref[...] * ut_ref[...]
    z = jax.nn.gelu(y)
    gate = jax.nn.sigmoid(_dot(wglut_ref[...], z.astype(BF16)) + bglu_ref[...])
    m = _dot_tn((z * gate).astype(BF16), wout_ref[...])
    o_ref[...] = h_ref[...] + _rms(m, g_ref[...])


def _od_out(y_t, u_t, d, w_glu_t, b_glu, w_out, g, h):
    tm = ROW_TILE
    return pl.pallas_call(
        _od_out_kernel,
        grid=(SEQ // tm,),
        in_specs=[
            pl.BlockSpec((C_WIDTH, tm), lambda i: (0, i)),
            pl.BlockSpec((C_WIDTH, tm), lambda i: (0, i)),
            pl.BlockSpec((C_WIDTH, 1), lambda i: (0, 0)),
            pl.BlockSpec((C_WIDTH, C_WIDTH), lambda i: (0, 0)),
            pl.BlockSpec((C_WIDTH, 1), lambda i: (0, 0)),
            pl.BlockSpec((C_WIDTH, D_MODEL), lambda i: (0, 0)),
            pl.BlockSpec((1, D_MODEL), lambda i: (0, 0)),
            pl.BlockSpec((tm, D_MODEL), lambda i: (i, 0)),
        ],
        out_specs=pl.BlockSpec((tm, D_MODEL), lambda i: (i, 0)),
        out_shape=jax.ShapeDtypeStruct((SEQ, D_MODEL), F32),
        compiler_params=_params("parallel"),
        name="od_out",
    )(y_t, u_t, d, w_glu_t, b_glu, w_out, g, h)


def _row(v):
    return v.reshape(1, -1).astype(F32)


def kernel(x, norm_g, ffn_w_gate, ffn_w_up, ffn_w_down, ev_w_in, ev_ln_g, ev_ln_b, ev_w_s, ev_b_s, ev_w_gate2, ev_b_gate, ev_gla_norm_g, ev_w_out, od_w_in, od_lam_re, od_lam_im, od_log_dt, od_b_re, od_b_im, od_c_re, od_c_im, od_d, od_w_glu, od_b_glu, od_w_out):
    depth = norm_g.shape[0]
    h = x.reshape(SEQ, D_MODEL)
    for l in range(depth):
        i = l // 2
        g = norm_g[l]
        h = _ffn(h, _row(g[0]), _row(g[1]), ffn_w_gate, ffn_w_up, ffn_w_down, l, 0)
        if l % 2 == 0:
            w_g2 = jnp.pad(ev_w_gate2[i], ((0, LANES - B_GATE_RANK), (0, 0))).astype(BF16)
            p, gk = _ev_in(h, _row(g[2]), ev_w_in, w_g2, _row(ev_b_gate[i]), i)
            bs_full = jnp.repeat(ev_b_s[i].T, A_HEAD_DIM, axis=1).astype(F32)
            y = _ev_mix(p, gk, _row(ev_ln_g[i]), _row(ev_ln_b[i]), ev_w_s[i], bs_full,
                        _row(ev_gla_norm_g[i]))
            h = _ev_out(y, ev_w_out[i].astype(BF16), _row(g[3]), h)
        else:
            u_t, u_tb = _od_in(h, _row(g[2]), od_w_in[i].T.astype(BF16))
            prm = _s5_params(od_lam_re[i], od_lam_im[i], od_log_dt[i], od_b_re[i], od_b_im[i],
                             od_c_re[i], od_c_im[i])
            y_t = _s5(u_tb, prm)
            h = _od_out(y_t, u_t, od_d[i].reshape(C_WIDTH, 1), od_w_glu[i].T.astype(BF16),
                        od_b_glu[i].reshape(C_WIDTH, 1), od_w_out[i].astype(BF16), _row(g[3]), h)
        h = _ffn(h, _row(g[4]), _row(g[5]), ffn_w_gate, ffn_w_up, ffn_w_down, l, 1)
    return h.reshape(x.shape)
```

```python
import jax
import jax.numpy as jnp
from jax import lax
from jax.experimental import pallas as pl
from jax.experimental.pallas import tpu as pltpu

F32 = jnp.float32
BF16 = jnp.bfloat16

D_MODEL = 2048
SEQ = 8192
D_FF = 5632
EPS = 1e-6

A_HEADS = 8
A_HEAD_DIM = 128
A_WIDTH = 1024
A_CHUNK = 128
B_HEADS = 4
B_DK = 128
B_DV = 256
B_KEY = 512
B_VAL = 1024
B_GATE_RANK = 16
B_GATE_TAU = 16.0
B_CHUNK = 64
EVEN_MAIN = 2 * A_WIDTH + 2 * B_KEY + 2 * B_VAL
C_WIDTH = 1024
C_GROUP = 16
C_GROUPS = 64
C_STATE = 64

LANES = 128
VMEM_LIMIT = 56 * 1024 * 1024
FFN_VMEM_LIMIT = 60 * 1024 * 1024

ROW_TILE = 512
EV_IN_ROWS = 1024
FFN_ROWS = 1024
FFN_SLABS = 8
FFN_EPI_ROWS = 16
FF_TILE = 512
S5_CHUNK = 32
S5_NC = SEQ // S5_CHUNK
S5_COLS = C_GROUP * S5_CHUNK
S5_GB = 8


def _rms(x, g):
    return x * lax.rsqrt(jnp.mean(x * x, axis=-1, keepdims=True) + EPS) * g


def _dot(a, b):
    return jnp.dot(a, b, preferred_element_type=F32)


def _dot_nt(a, b):
    return lax.dot_general(a, b, (((1,), (1,)), ((), ())), preferred_element_type=F32)


def _dot_tn(a, b):
    return lax.dot_general(a, b, (((0,), (0,)), ((), ())), preferred_element_type=F32)


def _params(*sem, vmem=VMEM_LIMIT):
    return pltpu.CompilerParams(dimension_semantics=sem, vmem_limit_bytes=vmem)


def _ffn_kernel(h_hbm, gpre_ref, gpost_ref, wg_ref, wu_ref, wd_ref, o_hbm,
                h_ref, acc_ref, xn_ref, h_sem, o_sem):
    i, j = pl.program_id(0), pl.program_id(1)
    n_i, n_j = pl.num_programs(0), pl.num_programs(1)
    tm = h_ref.shape[0]
    slab = tm // FFN_SLABS

    def h_copy(tile, r):
        return pltpu.make_async_copy(h_hbm.at[pl.ds(tile * tm + r * slab, slab), :],
                                     h_ref.at[pl.ds(r * slab, slab), :], h_sem.at[r])

    def o_copy(tile, r):
        return pltpu.make_async_copy(acc_ref.at[pl.ds(r * slab, slab), :],
                                     o_hbm.at[pl.ds(tile * tm + r * slab, slab), :], o_sem.at[r])

    @pl.when(j == 0)
    def _():
        @pl.when(i == 0)
        def _():
            for r in range(FFN_SLABS):
                h_copy(0, r).start()

        for r in range(FFN_SLABS):
            rows = pl.ds(r * slab, slab)
            h_copy(i, r).wait()
            xn_ref[rows, :] = _rms(h_ref[rows, :], gpre_ref[...]).astype(BF16)

            @pl.when(i > 0)
            def _():
                o_copy(i - 1, r).wait()
            acc_ref[rows, :] = jnp.zeros((slab, acc_ref.shape[1]), F32)

    xn = xn_ref[...]
    gate = _dot(xn, wg_ref[...].astype(BF16))
    up = _dot(xn, wu_ref[...].astype(BF16))
    act = (jax.nn.silu(gate) * up).astype(BF16)
    acc_ref[...] += _dot(act, wd_ref[...].astype(BF16))

    @pl.when(j == n_j - 1)
    def _():
        for r in range(FFN_SLABS):
            for q in range(slab // FFN_EPI_ROWS):
                rows = pl.ds(r * slab + q * FFN_EPI_ROWS, FFN_EPI_ROWS)
                acc_ref[rows, :] = h_ref[rows, :] + 0.5 * _rms(acc_ref[rows, :], gpost_ref[...])
            o_copy(i, r).start()

            @pl.when(i + 1 < n_i)
            def _():
                h_copy(i + 1, r).start()

        @pl.when(i == n_i - 1)
        def _():
            for r in range(FFN_SLABS):
                o_copy(i, r).wait()


def _ffn(h, g_pre, g_post, wg, wu, wd, layer, half):
    tm, tf = FFN_ROWS, FF_TILE
    return pl.pallas_call(
        _ffn_kernel,
        grid=(SEQ // tm, D_FF // tf),
        in_specs=[
            pl.BlockSpec(memory_space=pl.ANY),
            pl.BlockSpec((1, D_MODEL), lambda i, j: (0, 0)),
            pl.BlockSpec((1, D_MODEL), lambda i, j: (0, 0)),
            pl.BlockSpec((None, None, D_MODEL, tf), lambda i, j: (layer, half, 0, j)),
            pl.BlockSpec((None, None, D_MODEL, tf), lambda i, j: (layer, half, 0, j)),
            pl.BlockSpec((None, None, tf, D_MODEL), lambda i, j: (layer, half, j, 0)),
        ],
        out_specs=pl.BlockSpec(memory_space=pl.ANY),
        out_shape=jax.ShapeDtypeStruct((SEQ, D_MODEL), F32),
        scratch_shapes=[
            pltpu.VMEM((tm, D_MODEL), F32),
            pltpu.VMEM((tm, D_MODEL), F32),
            pltpu.VMEM((tm, D_MODEL), BF16),
            pltpu.SemaphoreType.DMA((FFN_SLABS,)),
            pltpu.SemaphoreType.DMA((FFN_SLABS,)),
        ],
        compiler_params=_params("arbitrary", "arbitrary", vmem=FFN_VMEM_LIMIT),
        name="ffn",
    )(h, g_pre, g_post, wg, wu, wd)


def _ev_in_kernel(h_ref, g_ref, w_ref, wlr_ref, wg2_ref, bg_ref, p_ref, gk_ref, xn_ref):
    j = pl.program_id(1)

    @pl.when(j == 0)
    def _():
        xn = _rms(h_ref[...], g_ref[...]).astype(BF16)
        xn_ref[...] = xn
        lane = lax.broadcasted_iota(jnp.int32, (xn.shape[0], LANES), 1)
        glr = jnp.where(lane < B_GATE_RANK, _dot_nt(xn, wlr_ref[...].astype(BF16)), 0.0).astype(BF16)
        pre = _dot(glr, wg2_ref[...]) + bg_ref[...]
        gk_ref[...] = jax.nn.log_sigmoid(pre) * (1.0 / B_GATE_TAU)

    p_ref[...] = _dot_nt(xn_ref[...], w_ref[...].astype(BF16))


def _ev_in(h, g, w_in_t, w_g2, b_gate, layer):
    tm, tn = EV_IN_ROWS, 1024
    return pl.pallas_call(
        _ev_in_kernel,
        grid=(SEQ // tm, EVEN_MAIN // tn),
        in_specs=[
            pl.BlockSpec((tm, D_MODEL), lambda i, j: (i, 0)),
            pl.BlockSpec((1, D_MODEL), lambda i, j: (0, 0)),
            pl.BlockSpec((None, tn, D_MODEL), lambda i, j: (layer, j, 0)),
            pl.BlockSpec((None, LANES, D_MODEL), lambda i, j: (layer, EVEN_MAIN // LANES, 0)),
            pl.BlockSpec((LANES, B_KEY), lambda i, j: (0, 0)),
            pl.BlockSpec((1, B_KEY), lambda i, j: (0, 0)),
        ],
        out_specs=[
            pl.BlockSpec((tm, tn), lambda i, j: (i, j)),
            pl.BlockSpec((tm, B_KEY), lambda i, j: (i, 0)),
        ],
        out_shape=[
            jax.ShapeDtypeStruct((SEQ, EVEN_MAIN), F32),
            jax.ShapeDtypeStruct((SEQ, B_KEY), F32),
        ],
        scratch_shapes=[pltpu.VMEM((tm, D_MODEL), BF16)],
        compiler_params=_params("parallel", "arbitrary"),
        name="ev_in",
    )(h, g, w_in_t, w_in_t, w_g2, b_gate)


def _ev_mix_kernel(za_ref, q_ref, k_ref, v_ref, r_ref, gk_ref, lng_ref, lnb_ref, ws_ref, bs_ref,
                   gng_ref, y_ref, u_ref, vln_ref, st_ref):
    tm = za_ref.shape[0]

    @pl.when(pl.program_id(0) == 0)
    def _():
        st_ref[...] = jnp.zeros_like(st_ref)

    z = jax.nn.gelu(za_ref[...])
    u_ref[...] = z[:, :A_WIDTH]
    va = z[:, A_WIDTH:]
    mu = jnp.mean(va, axis=-1, keepdims=True)
    vc = va - mu
    vln = vc * lax.rsqrt(jnp.mean(vc * vc, axis=-1, keepdims=True) + EPS)
    vln_ref[...] = (vln * lng_ref[...] + lnb_ref[...]).astype(BF16)

    n_a = tm // A_CHUNK
    row = lax.broadcasted_iota(jnp.int32, (A_CHUNK, A_CHUNK), 0)
    col = lax.broadcasted_iota(jnp.int32, (A_CHUNK, A_CHUNK), 1)
    for hd in range(A_HEADS):
        cs = slice(hd * A_HEAD_DIM, (hd + 1) * A_HEAD_DIM)
        w = jnp.where(row >= col, ws_ref[hd], 0.0).astype(BF16)
        rhs = jnp.concatenate(
            [vln_ref[c * A_CHUNK:(c + 1) * A_CHUNK, cs] for c in range(n_a)], axis=1)
        s = _dot(w, rhs)
        for c in range(n_a):
            rs = slice(c * A_CHUNK, (c + 1) * A_CHUNK)
            sc = s[:, c * A_HEAD_DIM:(c + 1) * A_HEAD_DIM] + bs_ref[:, cs]
            y_ref[rs, cs] = (u_ref[rs, cs] * sc).astype(BF16)

    n_b = tm // B_CHUNK
    r64 = lax.broadcasted_iota(jnp.int32, (B_CHUNK, B_CHUNK), 0)
    c64 = lax.broadcasted_iota(jnp.int32, (B_CHUNK, B_CHUNK), 1)
    causal = r64 >= c64
    tri = jnp.where(causal, 1.0, 0.0).astype(BF16)
    scale = B_DK ** -0.5
    for c in range(n_b):
        rs = slice(c * B_CHUNK, (c + 1) * B_CHUNK)
        g = gk_ref[rs, :]
        g_hi = g.astype(BF16)
        g_lo = (g - g_hi.astype(F32)).astype(BF16)
        bcum = _dot(tri, g_hi) + _dot(tri, g_lo)
        b_last = bcum[B_CHUNK - 1:B_CHUNK, :]
        q = q_ref[rs, :] * scale
        k = k_ref[rs, :]
        q_dec = (q * jnp.exp(bcum)).astype(BF16)
        k_inv = (k * jnp.exp(-bcum)).astype(BF16)
        k_end = (k * jnp.exp(b_last - bcum)).astype(BF16)
        decay = jnp.exp(b_last)
        for hd in range(B_HEADS):
            ks = slice(hd * B_DK, (hd + 1) * B_DK)
            vs = slice(hd * B_DV, (hd + 1) * B_DV)
            v = v_ref[rs, vs].astype(BF16)
            scores = jnp.where(causal, _dot_nt(q_dec[:, ks], k_inv[:, ks]), 0.0).astype(BF16)
            st = st_ref[hd]
            o = _dot(scores, v) + _dot_nt(q_dec[:, ks], st.astype(BF16))
            st_ref[hd] = decay[:, ks] * st + _dot_tn(v, k_end[:, ks])
            o = o * lax.rsqrt(jnp.mean(o * o, axis=-1, keepdims=True) + EPS) * gng_ref[...]
            y_ref[rs, A_WIDTH + hd * B_DV:A_WIDTH + (hd + 1) * B_DV] = (
                o * jax.nn.silu(r_ref[rs, vs])).astype(BF16)


def _ev_mix(p, gk, ln_g, ln_b, w_s, bs_full, gla_norm_g):
    tm = ROW_TILE
    return pl.pallas_call(
        _ev_mix_kernel,
        grid=(SEQ // tm,),
        in_specs=[
            pl.BlockSpec((tm, 2 * A_WIDTH), lambda i: (i, 0)),
            pl.BlockSpec((tm, B_KEY), lambda i: (i, 2 * A_WIDTH // B_KEY)),
            pl.BlockSpec((tm, B_KEY), lambda i: (i, 2 * A_WIDTH // B_KEY + 1)),
            pl.BlockSpec((tm, B_VAL), lambda i: (i, (2 * A_WIDTH + 2 * B_KEY) // B_VAL)),
            pl.BlockSpec((tm, B_VAL), lambda i: (i, (2 * A_WIDTH + 2 * B_KEY) // B_VAL + 1)),
            pl.BlockSpec((tm, B_KEY), lambda i: (i, 0)),
            pl.BlockSpec((1, A_WIDTH), lambda i: (0, 0)),
            pl.BlockSpec((1, A_WIDTH), lambda i: (0, 0)),
            pl.BlockSpec((A_HEADS, A_CHUNK, A_CHUNK), lambda i: (0, 0, 0)),
            pl.BlockSpec((A_CHUNK, A_WIDTH), lambda i: (0, 0)),
            pl.BlockSpec((1, B_DV), lambda i: (0, 0)),
        ],
        out_specs=pl.BlockSpec((tm, A_WIDTH + B_VAL), lambda i: (i, 0)),
        out_shape=jax.ShapeDtypeStruct((SEQ, A_WIDTH + B_VAL), BF16),
        scratch_shapes=[
            pltpu.VMEM((tm, A_WIDTH), F32),
            pltpu.VMEM((tm, A_WIDTH), BF16),
            pltpu.VMEM((B_HEADS, B_DV, B_DK), F32),
        ],
        compiler_params=_params("arbitrary"),
        name="ev_mix",
    )(p, p, p, p, p, gk, ln_g, ln_b, w_s, bs_full, gla_norm_g)


def _ev_out_kernel(y_ref, w_ref, g_ref, h_ref, o_ref):
    o_ref[...] = h_ref[...] + _rms(_dot(y_ref[...], w_ref[...].astype(BF16)), g_ref[...])


def _ev_out(y, w, g, h, layer):
    tm = ROW_TILE
    kdim = y.shape[1]
    return pl.pallas_call(
        _ev_out_kernel,
        grid=(SEQ // tm,),
        in_specs=[
            pl.BlockSpec((tm, kdim), lambda i: (i, 0)),
            pl.BlockSpec((None, kdim, D_MODEL), lambda i: (layer, 0, 0), pipeline_mode=pl.Buffered(1)),
            pl.BlockSpec((1, D_MODEL), lambda i: (0, 0)),
            pl.BlockSpec((tm, D_MODEL), lambda i: (i, 0)),
        ],
        out_specs=pl.BlockSpec((tm, D_MODEL), lambda i: (i, 0)),
        out_shape=jax.ShapeDtypeStruct((SEQ, D_MODEL), F32),
        compiler_params=_params("parallel"),
        name="ev_out",
    )(y, w, g, h)


def _od_in_kernel(h_ref, g_ref, w_ref, ut_ref, utb_ref):
    ut = lax.dot_general(w_ref[...].astype(BF16), _rms(h_ref[...], g_ref[...]).astype(BF16),
                         (((0,), (1,)), ((), ())), preferred_element_type=F32)
    ut_ref[...] = ut
    utb_ref[...] = ut.astype(BF16)


def _od_in(h, g, w, layer):
    tm = ROW_TILE
    return pl.pallas_call(
        _od_in_kernel,
        grid=(SEQ // tm,),
        in_specs=[
            pl.BlockSpec((tm, D_MODEL), lambda i: (i, 0)),
            pl.BlockSpec((1, D_MODEL), lambda i: (0, 0)),
            pl.BlockSpec((None, D_MODEL, C_WIDTH), lambda i: (layer, 0, 0), pipeline_mode=pl.Buffered(1)),
        ],
        out_specs=[
            pl.BlockSpec((C_WIDTH, tm), lambda i: (0, i)),
            pl.BlockSpec((C_WIDTH, tm), lambda i: (0, i)),
        ],
        out_shape=[
            jax.ShapeDtypeStruct((C_WIDTH, SEQ), F32),
            jax.ShapeDtypeStruct((C_WIDTH, SEQ), BF16),
        ],
        compiler_params=_params("parallel"),
        name="od_in",
    )(h, g, w)


def _split3(x):
    hi = x.astype(BF16)
    r1 = x - hi.astype(F32)
    mid = r1.astype(BF16)
    lo = (r1 - mid.astype(F32)).astype(BF16)
    return hi, mid, lo


def _dot_f32(a, b):
    a0, a1, a2 = _split3(a)
    b0, b1, b2 = _split3(b)
    return (_dot(a0, b0) + (_dot(a0, b1) + _dot(a1, b0))
            + (_dot(a0, b2) + _dot(a1, b1) + _dot(a2, b0)))


def _lane_expand(x, e):
    x0, x1, x2 = _split3(x)
    return _dot(x0, e) + _dot(x1, e) + _dot(x2, e)


def _s5_group_operators(ar, ai, bbr, bbi, bbt, crt, cit):
    tau = lax.broadcasted_iota(jnp.int32, (C_STATE, LANES), 1) & (S5_CHUNK - 1)
    expand = (lax.broadcasted_iota(jnp.int32, (C_GROUP, S5_COLS), 1) // S5_CHUNK
              == lax.broadcasted_iota(jnp.int32, (C_GROUP, S5_COLS), 0)).astype(BF16)
    one = jnp.ones((C_STATE, LANES), F32)
    zero = jnp.zeros((C_STATE, LANES), F32)
    pr, pi = one, zero
    rr, ri = one, zero
    fr, fi = ar, ai
    for b in range(S5_CHUNK.bit_length() - 1):
        bit = ((tau >> b) & 1) == 1
        pr, pi = jnp.where(bit, pr * fr - pi * fi, pr), jnp.where(bit, pr * fi + pi * fr, pi)
        rr, ri = jnp.where(bit, rr, rr * fr - ri * fi), jnp.where(bit, ri, rr * fi + ri * fr)
        fr, fi = fr * fr - fi * fi, 2.0 * (fr * fi)
    wide = lambda a: jnp.concatenate([a] * (S5_COLS // LANES), axis=1)
    pr, pi, rr, ri, ar, ai = wide(pr), wide(pi), wide(rr), wide(ri), wide(ar), wide(ai)
    cr = _lane_expand(crt, expand)
    ci = _lane_expand(cit, expand)
    cer = cr * pr - ci * pi
    cei = cr * pi + ci * pr
    kv = _dot_f32(bbt, jnp.concatenate([cer, -cei], axis=0))
    qr = cer * ar - cei * ai
    qi = cer * ai + cei * ar
    q = jnp.concatenate([qr, -qi], axis=0).astype(BF16)
    br = _lane_expand(bbr, expand)
    bi = _lane_expand(bbi, expand)
    ptr = br * rr - bi * ri
    pti = br * ri + bi * rr
    pt = jnp.concatenate([ptr, pti], axis=0).astype(BF16)
    pts = jnp.concatenate([pti, ptr], axis=0).astype(BF16)
    return kv, pt, pts, q


def _s5_kernel(ut2_ref, ar_ref, ai_ref, bbr_ref, bbi_ref, bbt_ref, crt_ref, cit_ref,
               a1_ref, a2_ref, a2s_ref, yt2_ref, ut_ref, yt_ref, u_ref, m_ref, q_ref, inc_ref, incs_ref,
               xs_ref, y_ref):
    gb = u_ref.shape[0]
    w = 2 * C_STATE
    ut_ref[...] = ut2_ref[...].reshape(ut_ref.shape)
    per = LANES // S5_CHUNK
    nsup = S5_NC // per
    keep = ((lax.broadcasted_iota(jnp.int32, (S5_CHUNK, LANES), 1) & (S5_CHUNK - 1))
            >= lax.broadcasted_iota(jnp.int32, (S5_CHUNK, LANES), 0))
    for gi in range(gb):
        for cp in range(C_GROUP):
            for k in range(per):
                u_ref[gi, k * nsup:(k + 1) * nsup, cp * S5_CHUNK:(cp + 1) * S5_CHUNK] = (
                    ut_ref[gi * C_GROUP + cp, :, k * S5_CHUNK:(k + 1) * S5_CHUNK])
        kv, pt, pts, q = _s5_group_operators(ar_ref[gi], ai_ref[gi], bbr_ref[gi], bbi_ref[gi], bbt_ref[gi],
                                             crt_ref[gi], cit_ref[gi])
        q_ref[gi] = q
        for cp in range(C_GROUP):
            taps = jnp.broadcast_to(kv[cp:cp + 1, :], (S5_CHUNK, S5_COLS))
            for v in range(S5_COLS // LANES):
                tile = pltpu.roll(taps[:, v * LANES:(v + 1) * LANES], 0, 1, stride=1, stride_axis=0)
                m_ref[gi, cp * S5_CHUNK:(cp + 1) * S5_CHUNK, v * LANES:(v + 1) * LANES] = (
                    jnp.where(keep, tile, 0.0).astype(BF16))
        u = u_ref[gi]
        inc_ref[:, gi * w:(gi + 1) * w] = _dot_nt(u, pt)
        incs_ref[:, gi * w:(gi + 1) * w] = _dot_nt(u, pts)

    a1 = a1_ref[...]
    a2 = a2_ref[...]
    a2s = a2s_ref[...]

    def step(n, carry):
        x, xs = carry
        row = pl.ds((n & (per - 1)) * nsup + (n >> (per.bit_length() - 1)), 1)
        xs_ref[row, :] = x
        x_new = a1 * x + a2 * xs + inc_ref[row, :]
        xs_new = a1 * xs + a2s * x + incs_ref[row, :]
        return x_new, xs_new

    zero = jnp.zeros((1, gb * w), F32)
    lax.fori_loop(0, S5_NC, step, (zero, zero))

    for gi in range(gb):
        xst = xs_ref[:, gi * w:(gi + 1) * w].astype(BF16)
        y_ref[...] = _dot(u_ref[gi], m_ref[gi]) + _dot(xst, q_ref[gi])
        for c in range(C_GROUP):
            for k in range(per):
                yt_ref[gi * C_GROUP + c, :, k * S5_CHUNK:(k + 1) * S5_CHUNK] = (
                    y_ref[k * nsup:(k + 1) * nsup, c * S5_CHUNK:(c + 1) * S5_CHUNK])
    yt2_ref[...] = yt_ref[...].reshape(yt2_ref.shape)


def _s5(u_tb, prm):
    gb = S5_GB
    w = 2 * C_STATE
    grp3 = lambda i: (i, 0, 0)
    return pl.pallas_call(
        _s5_kernel,
        grid=(C_GROUPS // gb,),
        in_specs=[
            pl.BlockSpec((gb * C_GROUP, SEQ), lambda i: (i, 0)),
            pl.BlockSpec((gb, C_STATE, LANES), grp3),
            pl.BlockSpec((gb, C_STATE, LANES), grp3),
            pl.BlockSpec((gb, C_STATE, C_GROUP), grp3),
            pl.BlockSpec((gb, C_STATE, C_GROUP), grp3),
            pl.BlockSpec((gb, C_GROUP, w), grp3),
            pl.BlockSpec((gb, C_STATE, C_GROUP), grp3),
            pl.BlockSpec((gb, C_STATE, C_GROUP), grp3),
            pl.BlockSpec((1, gb * w), lambda i: (0, i)),
            pl.BlockSpec((1, gb * w), lambda i: (0, i)),
            pl.BlockSpec((1, gb * w), lambda i: (0, i)),
        ],
        out_specs=pl.BlockSpec((gb * C_GROUP, SEQ), lambda i: (i, 0)),
        out_shape=jax.ShapeDtypeStruct((C_WIDTH, SEQ), F32),
        scratch_shapes=[
            pltpu.VMEM((gb * C_GROUP, SEQ // LANES, LANES), BF16),
            pltpu.VMEM((gb * C_GROUP, SEQ // LANES, LANES), F32),
            pltpu.VMEM((gb, S5_NC, S5_COLS), BF16),
            pltpu.VMEM((gb, S5_COLS, S5_COLS), BF16),
            pltpu.VMEM((gb, w, S5_COLS), BF16),
            pltpu.VMEM((S5_NC, gb * w), F32),
            pltpu.VMEM((S5_NC, gb * w), F32),
            pltpu.VMEM((S5_NC, gb * w), F32),
            pltpu.VMEM((S5_NC, S5_COLS), F32),
        ],
        compiler_params=_params("parallel"),
        name="s5",
    )(u_tb, *prm)


def _s5_params(lam_re, lam_im, log_dt, b_re, b_im, c_re, c_im):
    lr = jnp.minimum(lam_re, -1e-4)
    li = lam_im
    dt = jnp.exp(log_dt)[:, None]
    mag = jnp.exp(lr * dt)
    ar = mag * jnp.cos(li * dt)
    ai = mag * jnp.sin(li * dt)
    den = lr * lr + li * li
    nr = ar - 1.0
    cr = (nr * lr + ai * li) / den
    ci = (ai * lr - nr * li) / den
    bbr = cr[..., None] * b_re - ci[..., None] * b_im
    bbi = cr[..., None] * b_im + ci[..., None] * b_re
    bbt = jnp.concatenate([bbr, bbi], axis=1).transpose(0, 2, 1)
    atr, ati = ar, ai
    for _ in range(S5_CHUNK.bit_length() - 1):
        atr, ati = atr * atr - ati * ati, 2.0 * (atr * ati)
    a1 = jnp.concatenate([atr, atr], axis=-1).reshape(1, -1)
    a2 = jnp.concatenate([-ati, ati], axis=-1).reshape(1, -1)
    a2s = jnp.concatenate([ati, -ati], axis=-1).reshape(1, -1)
    bcast = lambda a: jnp.broadcast_to(a[..., None], a.shape + (LANES,))
    return (bcast(ar), bcast(ai), bbr, bbi, bbt, c_re.transpose(0, 2, 1), c_im.transpose(0, 2, 1),
            a1, a2, a2s)


def _od_out_kernel(yt_ref, ut_ref, d_ref, wglu_ref, bglu_ref, wout_ref, g_ref, h_ref, o_ref):
    y = yt_ref[...] + d_ref[...] * ut_ref[...]
    z = jax.nn.gelu(y)
    gate = jax.nn.sigmoid(_dot_tn(wglu_ref[...].astype(BF16), z.astype(BF16)) + bglu_ref[...])
    m = _dot_tn((z * gate).astype(BF16), wout_ref[...].astype(BF16))
    o_ref[...] = h_ref[...] + _rms(m, g_ref[...])


def _od_out(y_t, u_t, d, w_glu, b_glu, w_out, g, h, layer):
    tm = ROW_TILE
    return pl.pallas_call(
        _od_out_kernel,
        grid=(SEQ // tm,),
        in_specs=[
            pl.BlockSpec((C_WIDTH, tm), lambda i: (0, i)),
            pl.BlockSpec((C_WIDTH, tm), lambda i: (0, i)),
            pl.BlockSpec((C_WIDTH, 1), lambda i: (0, 0)),
            pl.BlockSpec((None, C_WIDTH, C_WIDTH), lambda i: (layer, 0, 0), pipeline_mode=pl.Buffered(1)),
            pl.BlockSpec((C_WIDTH, 1), lambda i: (0, 0)),
            pl.BlockSpec((None, C_WIDTH, D_MODEL), lambda i: (layer, 0, 0), pipeline_mode=pl.Buffered(1)),
            pl.BlockSpec((1, D_MODEL), lambda i: (0, 0)),
            pl.BlockSpec((tm, D_MODEL), lambda i: (i, 0)),
        ],
        out_specs=pl.BlockSpec((tm, D_MODEL), lambda i: (i, 0)),
        out_shape=jax.ShapeDtypeStruct((SEQ, D_MODEL), F32),
        compiler_params=_params("parallel"),
        name="od_out",
    )(y_t, u_t, d, w_glu, b_glu, w_out, g, h)


def _row(v):
    return v.reshape(1, -1).astype(F32)


def kernel(x, norm_g, ffn_w_gate, ffn_w_up, ffn_w_down, ev_w_in, ev_ln_g, ev_ln_b, ev_w_s, ev_b_s, ev_w_gate2, ev_b_gate, ev_gla_norm_g, ev_w_out, od_w_in, od_lam_re, od_lam_im, od_log_dt, od_b_re, od_b_im, od_c_re, od_c_im, od_d, od_w_glu, od_b_glu, od_w_out):
    depth = norm_g.shape[0]
    h = x.reshape(SEQ, D_MODEL)
    for l in range(depth):
        i = l // 2
        g = norm_g[l]
        h = _ffn(h, _row(g[0]), _row(g[1]), ffn_w_gate, ffn_w_up, ffn_w_down, l, 0)
        if l % 2 == 0:
            w_g2 = jnp.pad(ev_w_gate2[i], ((0, LANES - B_GATE_RANK), (0, 0))).astype(BF16)
            p, gk = _ev_in(h, _row(g[2]), jnp.swapaxes(ev_w_in, 1, 2), w_g2, _row(ev_b_gate[i]), i)
            bs_full = jnp.repeat(ev_b_s[i].T, A_HEAD_DIM, axis=1).astype(F32)
            y = _ev_mix(p, gk, _row(ev_ln_g[i]), _row(ev_ln_b[i]), ev_w_s[i], bs_full,
                        _row(ev_gla_norm_g[i]))
            h = _ev_out(y, ev_w_out, _row(g[3]), h, i)
        else:
            u_t, u_tb = _od_in(h, _row(g[2]), od_w_in, i)
            prm = _s5_params(od_lam_re[i], od_lam_im[i], od_log_dt[i], od_b_re[i], od_b_im[i],
                             od_c_re[i], od_c_im[i])
            y_t = _s5(u_tb, prm)
            h = _od_out(y_t, u_t, od_d[i].reshape(C_WIDTH, 1), od_w_glu, od_b_glu[i].reshape(C_WIDTH, 1),
                        od_w_out, _row(g[3]), h, i)
        h = _ffn(h, _row(g[4]), _row(g[5]), ffn_w_gate, ffn_w_up, ffn_w_down, l, 1)
    return h.reshape(x.shape)
```

```python
import jax
import jax.numpy as jnp
from jax import lax
from jax.experimental import pallas as pl
from jax.experimental.pallas import tpu as pltpu

F32 = jnp.float32
BF16 = jnp.bfloat16

D_MODEL = 2048
SEQ = 8192
D_FF = 5632
EPS = 1e-6

A_HEADS = 8
A_HEAD_DIM = 128
A_WIDTH = 1024
A_CHUNK = 128
B_HEADS = 4
B_DK = 128
B_DV = 256
B_KEY = 512
B_VAL = 1024
B_GATE_RANK = 16
B_GATE_TAU = 16.0
B_CHUNK = 64
EVEN_MAIN = 2 * A_WIDTH + 2 * B_KEY + 2 * B_VAL
C_WIDTH = 1024
C_GROUP = 16
C_GROUPS = 64
C_STATE = 64

LANES = 128
VMEM_LIMIT = 56 * 1024 * 1024
FFN_VMEM_LIMIT = 60 * 1024 * 1024

ROW_TILE = 512
EV_IN_ROWS = 1024
FFN_ROWS = 1024
FFN_SLABS = 8
FFN_EPI_ROWS = 16
FF_TILE = 512
S5_CHUNK = 32
S5_NC = SEQ // S5_CHUNK
S5_COLS = C_GROUP * S5_CHUNK
S5_GB = 8


def _rms(x, g):
    return x * lax.rsqrt(jnp.mean(x * x, axis=-1, keepdims=True) + EPS) * g


def _dot(a, b):
    return jnp.dot(a, b, preferred_element_type=F32)


def _dot_nt(a, b):
    return lax.dot_general(a, b, (((1,), (1,)), ((), ())), preferred_element_type=F32)


def _dot_tn(a, b):
    return lax.dot_general(a, b, (((0,), (0,)), ((), ())), preferred_element_type=F32)


def _params(*sem, vmem=VMEM_LIMIT):
    return pltpu.CompilerParams(dimension_semantics=sem, vmem_limit_bytes=vmem)


def _ffn_kernel(h_hbm, gpre_ref, gpost_ref, wg_ref, wu_ref, wd_ref, o_hbm,
                h_ref, acc_ref, xn_ref, h_sem, o_sem):
    i, j = pl.program_id(0), pl.program_id(1)
    n_i, n_j = pl.num_programs(0), pl.num_programs(1)
    tm = h_ref.shape[0]
    slab = tm // FFN_SLABS

    def h_copy(tile, r):
        return pltpu.make_async_copy(h_hbm.at[pl.ds(tile * tm + r * slab, slab), :],
                                     h_ref.at[pl.ds(r * slab, slab), :], h_sem.at[r])

    def o_copy(tile, r):
        return pltpu.make_async_copy(acc_ref.at[pl.ds(r * slab, slab), :],
                                     o_hbm.at[pl.ds(tile * tm + r * slab, slab), :], o_sem.at[r])

    @pl.when(j == 0)
    def _():
        @pl.when(i == 0)
        def _():
            for r in range(FFN_SLABS):
                h_copy(0, r).start()

        for r in range(FFN_SLABS):
            rows = pl.ds(r * slab, slab)
            h_copy(i, r).wait()
            xn_ref[rows, :] = _rms(h_ref[rows, :], gpre_ref[...]).astype(BF16)

            @pl.when(i > 0)
            def _():
                o_copy(i - 1, r).wait()
            acc_ref[rows, :] = jnp.zeros((slab, acc_ref.shape[1]), F32)

    xn = xn_ref[...]
    hf = wg_ref.shape[1] // 2
    gate_a = _dot(xn, wg_ref[:, :hf].astype(BF16))
    up_a = _dot(xn, wu_ref[:, :hf].astype(BF16))
    gate_b = _dot(xn, wg_ref[:, hf:].astype(BF16))
    up_b = _dot(xn, wu_ref[:, hf:].astype(BF16))
    act_a = (jax.nn.silu(gate_a) * up_a).astype(BF16)
    act_b = (jax.nn.silu(gate_b) * up_b).astype(BF16)
    acc_ref[...] += _dot(act_a, wd_ref[:hf, :].astype(BF16)) + _dot(act_b, wd_ref[hf:, :].astype(BF16))

    @pl.when(j == n_j - 1)
    def _():
        for r in range(FFN_SLABS):
            for q in range(slab // FFN_EPI_ROWS):
                rows = pl.ds(r * slab + q * FFN_EPI_ROWS, FFN_EPI_ROWS)
                acc_ref[rows, :] = h_ref[rows, :] + 0.5 * _rms(acc_ref[rows, :], gpost_ref[...])
            o_copy(i, r).start()

            @pl.when(i + 1 < n_i)
            def _():
                h_copy(i + 1, r).start()

        @pl.when(i == n_i - 1)
        def _():
            for r in range(FFN_SLABS):
                o_copy(i, r).wait()


def _ffn(h, g_pre, g_post, wg, wu, wd, layer, half):
    tm, tf = FFN_ROWS, FF_TILE
    return pl.pallas_call(
        _ffn_kernel,
        grid=(SEQ // tm, D_FF // tf),
        in_specs=[
            pl.BlockSpec(memory_space=pl.ANY),
            pl.BlockSpec((1, D_MODEL), lambda i, j: (0, 0)),
            pl.BlockSpec((1, D_MODEL), lambda i, j: (0, 0)),
            pl.BlockSpec((None, None, D_MODEL, tf), lambda i, j: (layer, half, 0, j)),
            pl.BlockSpec((None, None, D_MODEL, tf), lambda i, j: (layer, half, 0, j)),
            pl.BlockSpec((None, None, tf, D_MODEL), lambda i, j: (layer, half, j, 0)),
        ],
        out_specs=pl.BlockSpec(memory_space=pl.ANY),
        out_shape=jax.ShapeDtypeStruct((SEQ, D_MODEL), F32),
        scratch_shapes=[
            pltpu.VMEM((tm, D_MODEL), F32),
            pltpu.VMEM((tm, D_MODEL), F32),
            pltpu.VMEM((tm, D_MODEL), BF16),
            pltpu.SemaphoreType.DMA((FFN_SLABS,)),
            pltpu.SemaphoreType.DMA((FFN_SLABS,)),
        ],
        compiler_params=_params("arbitrary", "arbitrary", vmem=FFN_VMEM_LIMIT),
        name="ffn",
    )(h, g_pre, g_post, wg, wu, wd)


def _ev_in_kernel(h_ref, g_ref, w_ref, wlr_ref, wg2_ref, bg_ref, p_ref, gk_ref, xn_ref):
    j = pl.program_id(1)

    @pl.when(j == 0)
    def _():
        xn = _rms(h_ref[...], g_ref[...]).astype(BF16)
        xn_ref[...] = xn
        lane = lax.broadcasted_iota(jnp.int32, (xn.shape[0], LANES), 1)
        glr = jnp.where(lane < B_GATE_RANK, _dot_nt(xn, wlr_ref[...].astype(BF16)), 0.0).astype(BF16)
        pre = _dot(glr, wg2_ref[...]) + bg_ref[...]
        gk_ref[...] = jax.nn.log_sigmoid(pre) * (1.0 / B_GATE_TAU)

    p_ref[...] = _dot_nt(xn_ref[...], w_ref[...].astype(BF16))


def _ev_in(h, g, w_in_t, w_g2, b_gate, layer):
    tm, tn = EV_IN_ROWS, 1024
    return pl.pallas_call(
        _ev_in_kernel,
        grid=(SEQ // tm, EVEN_MAIN // tn),
        in_specs=[
            pl.BlockSpec((tm, D_MODEL), lambda i, j: (i, 0)),
            pl.BlockSpec((1, D_MODEL), lambda i, j: (0, 0)),
            pl.BlockSpec((None, tn, D_MODEL), lambda i, j: (layer, j, 0)),
            pl.BlockSpec((None, LANES, D_MODEL), lambda i, j: (layer, EVEN_MAIN // LANES, 0)),
            pl.BlockSpec((LANES, B_KEY), lambda i, j: (0, 0)),
            pl.BlockSpec((1, B_KEY), lambda i, j: (0, 0)),
        ],
        out_specs=[
            pl.BlockSpec((tm, tn), lambda i, j: (i, j)),
            pl.BlockSpec((tm, B_KEY), lambda i, j: (i, 0)),
        ],
        out_shape=[
            jax.ShapeDtypeStruct((SEQ, EVEN_MAIN), F32),
            jax.ShapeDtypeStruct((SEQ, B_KEY), F32),
        ],
        scratch_shapes=[pltpu.VMEM((tm, D_MODEL), BF16)],
        compiler_params=_params("parallel", "arbitrary"),
        name="ev_in",
    )(h, g, w_in_t, w_in_t, w_g2, b_gate)


def _ev_mix_kernel(za_ref, q_ref, k_ref, v_ref, r_ref, gk_ref, lng_ref, lnb_ref, ws_ref, bs_ref,
                   gng_ref, y_ref, u_ref, vln_ref, st_ref):
    tm = za_ref.shape[0]

    @pl.when(pl.program_id(0) == 0)
    def _():
        st_ref[...] = jnp.zeros_like(st_ref)

    z = jax.nn.gelu(za_ref[...])
    u_ref[...] = z[:, :A_WIDTH]
    va = z[:, A_WIDTH:]
    mu = jnp.mean(va, axis=-1, keepdims=True)
    vc = va - mu
    vln = vc * lax.rsqrt(jnp.mean(vc * vc, axis=-1, keepdims=True) + EPS)
    vln_ref[...] = (vln * lng_ref[...] + lnb_ref[...]).astype(BF16)

    n_a = tm // A_CHUNK
    row = lax.broadcasted_iota(jnp.int32, (A_CHUNK, A_CHUNK), 0)
    col = lax.broadcasted_iota(jnp.int32, (A_CHUNK, A_CHUNK), 1)
    for hd in range(A_HEADS):
        cs = slice(hd * A_HEAD_DIM, (hd + 1) * A_HEAD_DIM)
        w = jnp.where(row >= col, ws_ref[hd], 0.0).astype(BF16)
        rhs = jnp.concatenate(
            [vln_ref[c * A_CHUNK:(c + 1) * A_CHUNK, cs] for c in range(n_a)], axis=1)
        s = _dot(w, rhs)
        for c in range(n_a):
            rs = slice(c * A_CHUNK, (c + 1) * A_CHUNK)
            sc = s[:, c * A_HEAD_DIM:(c + 1) * A_HEAD_DIM] + bs_ref[:, cs]
            y_ref[rs, cs] = (u_ref[rs, cs] * sc).astype(BF16)

    n_b = tm // B_CHUNK
    r64 = lax.broadcasted_iota(jnp.int32, (B_CHUNK, B_CHUNK), 0)
    c64 = lax.broadcasted_iota(jnp.int32, (B_CHUNK, B_CHUNK), 1)
    causal = r64 >= c64
    tri = jnp.where(causal, 1.0, 0.0).astype(BF16)
    scale = B_DK ** -0.5
    for c in range(n_b):
        rs = slice(c * B_CHUNK, (c + 1) * B_CHUNK)
        g = gk_ref[rs, :]
        g_hi = g.astype(BF16)
        g_lo = (g - g_hi.astype(F32)).astype(BF16)
        bcum = _dot(tri, g_hi) + _dot(tri, g_lo)
        b_last = bcum[B_CHUNK - 1:B_CHUNK, :]
        q = q_ref[rs, :] * scale
        k = k_ref[rs, :]
        q_dec = (q * jnp.exp(bcum)).astype(BF16)
        k_inv = (k * jnp.exp(-bcum)).astype(BF16)
        k_end = (k * jnp.exp(b_last - bcum)).astype(BF16)
        decay = jnp.exp(b_last)
        for hd in range(B_HEADS):
            ks = slice(hd * B_DK, (hd + 1) * B_DK)
            vs = slice(hd * B_DV, (hd + 1) * B_DV)
            v = v_ref[rs, vs].astype(BF16)
            scores = jnp.where(causal, _dot_nt(q_dec[:, ks], k_inv[:, ks]), 0.0).astype(BF16)
            st = st_ref[hd]
            o = _dot(scores, v) + _dot_nt(q_dec[:, ks], st.astype(BF16))
            st_ref[hd] = decay[:, ks] * st + _dot_tn(v, k_end[:, ks])
            o = o * lax.rsqrt(jnp.mean(o * o, axis=-1, keepdims=True) + EPS) * gng_ref[...]
            y_ref[rs, A_WIDTH + hd * B_DV:A_WIDTH + (hd + 1) * B_DV] = (
                o * jax.nn.silu(r_ref[rs, vs])).astype(BF16)


def _ev_mix(p, gk, ln_g, ln_b, w_s, bs_full, gla_norm_g):
    tm = ROW_TILE
    return pl.pallas_call(
        _ev_mix_kernel,
        grid=(SEQ // tm,),
        in_specs=[
            pl.BlockSpec((tm, 2 * A_WIDTH), lambda i: (i, 0)),
            pl.BlockSpec((tm, B_KEY), lambda i: (i, 2 * A_WIDTH // B_KEY)),
            pl.BlockSpec((tm, B_KEY), lambda i: (i, 2 * A_WIDTH // B_KEY + 1)),
            pl.BlockSpec((tm, B_VAL), lambda i: (i, (2 * A_WIDTH + 2 * B_KEY) // B_VAL)),
            pl.BlockSpec((tm, B_VAL), lambda i: (i, (2 * A_WIDTH + 2 * B_KEY) // B_VAL + 1)),
            pl.BlockSpec((tm, B_KEY), lambda i: (i, 0)),
            pl.BlockSpec((1, A_WIDTH), lambda i: (0, 0)),
            pl.BlockSpec((1, A_WIDTH), lambda i: (0, 0)),
            pl.BlockSpec((A_HEADS, A_CHUNK, A_CHUNK), lambda i: (0, 0, 0)),
            pl.BlockSpec((A_CHUNK, A_WIDTH), lambda i: (0, 0)),
            pl.BlockSpec((1, B_DV), lambda i: (0, 0)),
        ],
        out_specs=pl.BlockSpec((tm, A_WIDTH + B_VAL), lambda i: (i, 0)),
        out_shape=jax.ShapeDtypeStruct((SEQ, A_WIDTH + B_VAL), BF16),
        scratch_shapes=[
            pltpu.VMEM((tm, A_WIDTH), F32),
            pltpu.VMEM((tm, A_WIDTH), BF16),
            pltpu.VMEM((B_HEADS, B_DV, B_DK), F32),
        ],
        compiler_params=_params("arbitrary"),
        name="ev_mix",
    )(p, p, p, p, p, gk, ln_g, ln_b, w_s, bs_full, gla_norm_g)


def _ev_out_kernel(y_ref, w_ref, g_ref, h_ref, o_ref):
    o_ref[...] = h_ref[...] + _rms(_dot(y_ref[...], w_ref[...].astype(BF16)), g_ref[...])


def _ev_out(y, w, g, h, layer):
    tm = ROW_TILE
    kdim = y.shape[1]
    return pl.pallas_call(
        _ev_out_kernel,
        grid=(SEQ // tm,),
        in_specs=[
            pl.BlockSpec((tm, kdim), lambda i: (i, 0)),
            pl.BlockSpec((None, kdim, D_MODEL), lambda i: (layer, 0, 0), pipeline_mode=pl.Buffered(1)),
            pl.BlockSpec((1, D_MODEL), lambda i: (0, 0)),
            pl.BlockSpec((tm, D_MODEL), lambda i: (i, 0)),
        ],
        out_specs=pl.BlockSpec((tm, D_MODEL), lambda i: (i, 0)),
        out_shape=jax.ShapeDtypeStruct((SEQ, D_MODEL), F32),
        compiler_params=_params("parallel"),
        name="ev_out",
    )(y, w, g, h)


def _od_in_kernel(h_ref, g_ref, w_ref, ut_ref, utb_ref):
    ut = lax.dot_general(w_ref[...].astype(BF16), _rms(h_ref[...], g_ref[...]).astype(BF16),
                         (((0,), (1,)), ((), ())), preferred_element_type=F32)
    ut_ref[...] = ut
    utb_ref[...] = ut.astype(BF16)


def _od_in(h, g, w, layer):
    tm = ROW_TILE
    return pl.pallas_call(
        _od_in_kernel,
        grid=(SEQ // tm,),
        in_specs=[
            pl.BlockSpec((tm, D_MODEL), lambda i: (i, 0)),
            pl.BlockSpec((1, D_MODEL), lambda i: (0, 0)),
            pl.BlockSpec((None, D_MODEL, C_WIDTH), lambda i: (layer, 0, 0), pipeline_mode=pl.Buffered(1)),
        ],
        out_specs=[
            pl.BlockSpec((C_WIDTH, tm), lambda i: (0, i)),
            pl.BlockSpec((C_WIDTH, tm), lambda i: (0, i)),
        ],
        out_shape=[
            jax.ShapeDtypeStruct((C_WIDTH, SEQ), F32),
            jax.ShapeDtypeStruct((C_WIDTH, SEQ), BF16),
        ],
        compiler_params=_params("parallel"),
        name="od_in",
    )(h, g, w)


def _split3(x):
    hi = x.astype(BF16)
    r1 = x - hi.astype(F32)
    mid = r1.astype(BF16)
    lo = (r1 - mid.astype(F32)).astype(BF16)
    return hi, mid, lo


def _dot_f32(a, b):
    a0, a1, a2 = _split3(a)
    b0, b1, b2 = _split3(b)
    return (_dot(a0, b0) + (_dot(a0, b1) + _dot(a1, b0))
            + (_dot(a0, b2) + _dot(a1, b1) + _dot(a2, b0)))


def _lane_expand(x, e):
    x0, x1, x2 = _split3(x)
    return _dot(x0, e) + _dot(x1, e) + _dot(x2, e)


def _s5_group_operators(ar, ai, bbr, bbi, bbt, crt, cit):
    tau = lax.broadcasted_iota(jnp.int32, (C_STATE, LANES), 1) & (S5_CHUNK - 1)
    expand = (lax.broadcasted_iota(jnp.int32, (C_GROUP, S5_COLS), 1) // S5_CHUNK
              == lax.broadcasted_iota(jnp.int32, (C_GROUP, S5_COLS), 0)).astype(BF16)
    one = jnp.ones((C_STATE, LANES), F32)
    zero = jnp.zeros((C_STATE, LANES), F32)
    pr, pi = one, zero
    rr, ri = one, zero
    fr, fi = ar, ai
    for b in range(S5_CHUNK.bit_length() - 1):
        bit = ((tau >> b) & 1) == 1
        pr, pi = jnp.where(bit, pr * fr - pi * fi, pr), jnp.where(bit, pr * fi + pi * fr, pi)
        rr, ri = jnp.where(bit, rr, rr * fr - ri * fi), jnp.where(bit, ri, rr * fi + ri * fr)
        fr, fi = fr * fr - fi * fi, 2.0 * (fr * fi)
    wide = lambda a: jnp.concatenate([a] * (S5_COLS // LANES), axis=1)
    pr, pi, rr, ri, ar, ai = wide(pr), wide(pi), wide(rr), wide(ri), wide(ar), wide(ai)
    cr = _lane_expand(crt, expand)
    ci = _lane_expand(cit, expand)
    cer = cr * pr - ci * pi
    cei = cr * pi + ci * pr
    kv = _dot_f32(bbt, jnp.concatenate([cer, -cei], axis=0))
    qr = cer * ar - cei * ai
    qi = cer * ai + cei * ar
    q = jnp.concatenate([qr, -qi], axis=0).astype(BF16)
    br = _lane_expand(bbr, expand)
    bi = _lane_expand(bbi, expand)
    ptr = br * rr - bi * ri
    pti = br * ri + bi * rr
    pt = jnp.concatenate([ptr, pti], axis=0).astype(BF16)
    pts = jnp.concatenate([pti, ptr], axis=0).astype(BF16)
    return kv, pt, pts, q


def _s5_kernel(ut2_ref, ar_ref, ai_ref, bbr_ref, bbi_ref, bbt_ref, crt_ref, cit_ref,
               a1_ref, a2_ref, a2s_ref, yt2_ref, ut_ref, yt_ref, u_ref, m_ref, q_ref, inc_ref, incs_ref,
               xs_ref, y_ref):
    gb = u_ref.shape[0]
    w = 2 * C_STATE
    ut_ref[...] = ut2_ref[...].reshape(ut_ref.shape)
    per = LANES // S5_CHUNK
    nsup = S5_NC // per
    keep = ((lax.broadcasted_iota(jnp.int32, (S5_CHUNK, LANES), 1) & (S5_CHUNK - 1))
            >= lax.broadcasted_iota(jnp.int32, (S5_CHUNK, LANES), 0))
    for gi in range(gb):
        for cp in range(C_GROUP):
            for k in range(per):
                u_ref[gi, k * nsup:(k + 1) * nsup, cp * S5_CHUNK:(cp + 1) * S5_CHUNK] = (
                    ut_ref[gi * C_GROUP + cp, :, k * S5_CHUNK:(k + 1) * S5_CHUNK])
        kv, pt, pts, q = _s5_group_operators(ar_ref[gi], ai_ref[gi], bbr_ref[gi], bbi_ref[gi], bbt_ref[gi],
                                             crt_ref[gi], cit_ref[gi])
        q_ref[gi] = q
        for cp in range(C_GROUP):
            taps = jnp.broadcast_to(kv[cp:cp + 1, :], (S5_CHUNK, S5_COLS))
            for v in range(S5_COLS // LANES):
                tile = pltpu.roll(taps[:, v * LANES:(v + 1) * LANES], 0, 1, stride=1, stride_axis=0)
                m_ref[gi, cp * S5_CHUNK:(cp + 1) * S5_CHUNK, v * LANES:(v + 1) * LANES] = (
                    jnp.where(keep, tile, 0.0).astype(BF16))
        u = u_ref[gi]
        inc_ref[:, gi * w:(gi + 1) * w] = _dot_nt(u, pt)
        incs_ref[:, gi * w:(gi + 1) * w] = _dot_nt(u, pts)

    a1 = a1_ref[...]
    a2 = a2_ref[...]
    a2s = a2s_ref[...]

    def step(n, carry):
        x, xs = carry
        row = pl.ds((n & (per - 1)) * nsup + (n >> (per.bit_length() - 1)), 1)
        xs_ref[row, :] = x
        x_new = a1 * x + a2 * xs + inc_ref[row, :]
        xs_new = a1 * xs + a2s * x + incs_ref[row, :]
        return x_new, xs_new

    zero = jnp.zeros((1, gb * w), F32)
    lax.fori_loop(0, S5_NC, step, (zero, zero))

    for gi in range(gb):
        xst = xs_ref[:, gi * w:(gi + 1) * w].astype(BF16)
        y_ref[...] = _dot(u_ref[gi], m_ref[gi]) + _dot(xst, q_ref[gi])
        for c in range(C_GROUP):
            for k in range(per):
                yt_ref[gi * C_GROUP + c, :, k * S5_CHUNK:(k + 1) * S5_CHUNK] = (
                    y_ref[k * nsup:(k + 1) * nsup, c * S5_CHUNK:(c + 1) * S5_CHUNK])
    yt2_ref[...] = yt_ref[...].reshape(yt2_ref.shape)


def _s5(u_tb, prm):
    gb = S5_GB
    w = 2 * C_STATE
    grp3 = lambda i: (i, 0, 0)
    return pl.pallas_call(
        _s5_kernel,
        grid=(C_GROUPS // gb,),
        in_specs=[
            pl.BlockSpec((gb * C_GROUP, SEQ), lambda i: (i, 0)),
            pl.BlockSpec((gb, C_STATE, LANES), grp3),
            pl.BlockSpec((gb, C_STATE, LANES), grp3),
            pl.BlockSpec((gb, C_STATE, C_GROUP), grp3),
            pl.BlockSpec((gb, C_STATE, C_GROUP), grp3),
            pl.BlockSpec((gb, C_GROUP, w), grp3),
            pl.BlockSpec((gb, C_STATE, C_GROUP), grp3),
            pl.BlockSpec((gb, C_STATE, C_GROUP), grp3),
            pl.BlockSpec((1, gb * w), lambda i: (0, i)),
            pl.BlockSpec((1, gb * w), lambda i: (0, i)),
            pl.BlockSpec((1, gb * w), lambda i: (0, i)),
        ],
        out_specs=pl.BlockSpec((gb * C_GROUP, SEQ), lambda i: (i, 0)),
        out_shape=jax.ShapeDtypeStruct((C_WIDTH, SEQ), F32),
        scratch_shapes=[
            pltpu.VMEM((gb * C_GROUP, SEQ // LANES, LANES), BF16),
            pltpu.VMEM((gb * C_GROUP, SEQ // LANES, LANES), F32),
            pltpu.VMEM((gb, S5_NC, S5_COLS), BF16),
            pltpu.VMEM((gb, S5_COLS, S5_COLS), BF16),
            pltpu.VMEM((gb, w, S5_COLS), BF16),
            pltpu.VMEM((S5_NC, gb * w), F32),
            pltpu.VMEM((S5_NC, gb * w), F32),
            pltpu.VMEM((S5_NC, gb * w), F32),
            pltpu.VMEM((S5_NC, S5_COLS), F32),
        ],
        compiler_params=_params("parallel"),
        name="s5",
    )(u_tb, *prm)


def _s5_params(lam_re, lam_im, log_dt, b_re, b_im, c_re, c_im):
    lr = jnp.minimum(lam_re, -1e-4)
    li = lam_im
    dt = jnp.exp(log_dt)[:, None]
    mag = jnp.exp(lr * dt)
    ar = mag * jnp.cos(li * dt)
    ai = mag * jnp.sin(li * dt)
    den = lr * lr + li * li
    nr = ar - 1.0
    cr = (nr * lr + ai * li) / den
    ci = (ai * lr - nr * li) / den
    bbr = cr[..., None] * b_re - ci[..., None] * b_im
    bbi = cr[..., None] * b_im + ci[..., None] * b_re
    bbt = jnp.concatenate([bbr, bbi], axis=1).transpose(0, 2, 1)
    atr, ati = ar, ai
    for _ in range(S5_CHUNK.bit_length() - 1):
        atr, ati = atr * atr - ati * ati, 2.0 * (atr * ati)
    a1 = jnp.concatenate([atr, atr], axis=-1).reshape(1, -1)
    a2 = jnp.concatenate([-ati, ati], axis=-1).reshape(1, -1)
    a2s = jnp.concatenate([ati, -ati], axis=-1).reshape(1, -1)
    bcast = lambda a: jnp.broadcast_to(a[..., None], a.shape + (LANES,))
    return (bcast(ar), bcast(ai), bbr, bbi, bbt, c_re.transpose(0, 2, 1), c_im.transpose(0, 2, 1),
            a1, a2, a2s)


def _od_out_kernel(yt_ref, ut_ref, d_ref, wglu_ref, bglu_ref, wout_ref, g_ref, h_ref, o_ref):
    y = yt_ref[...] + d_ref[...] * ut_ref[...]
    z = jax.nn.gelu(y)
    gate = jax.nn.sigmoid(_dot_tn(wglu_ref[...].astype(BF16), z.astype(BF16)) + bglu_ref[...])
    m = _dot_tn((z * gate).astype(BF16), wout_ref[...].astype(BF16))
    o_ref[...] = h_ref[...] + _rms(m, g_ref[...])


def _od_out(y_t, u_t, d, w_glu, b_glu, w_out, g, h, layer):
    tm = ROW_TILE
    return pl.pallas_call(
        _od_out_kernel,
        grid=(SEQ // tm,),
        in_specs=[
            pl.BlockSpec((C_WIDTH, tm), lambda i: (0, i)),
            pl.BlockSpec((C_WIDTH, tm), lambda i: (0, i)),
            pl.BlockSpec((C_WIDTH, 1), lambda i: (0, 0)),
            pl.BlockSpec((None, C_WIDTH, C_WIDTH), lambda i: (layer, 0, 0), pipeline_mode=pl.Buffered(1)),
            pl.BlockSpec((C_WIDTH, 1), lambda i: (0, 0)),
            pl.BlockSpec((None, C_WIDTH, D_MODEL), lambda i: (layer, 0, 0), pipeline_mode=pl.Buffered(1)),
            pl.BlockSpec((1, D_MODEL), lambda i: (0, 0)),
            pl.BlockSpec((tm, D_MODEL), lambda i: (i, 0)),
        ],
        out_specs=pl.BlockSpec((tm, D_MODEL), lambda i: (i, 0)),
        out_shape=jax.ShapeDtypeStruct((SEQ, D_MODEL), F32),
        compiler_params=_params("parallel"),
        name="od_out",
    )(y_t, u_t, d, w_glu, b_glu, w_out, g, h)


def _row(v):
    return v.reshape(1, -1).astype(F32)


def kernel(x, norm_g, ffn_w_gate, ffn_w_up, ffn_w_down, ev_w_in, ev_ln_g, ev_ln_b, ev_w_s, ev_b_s, ev_w_gate2, ev_b_gate, ev_gla_norm_g, ev_w_out, od_w_in, od_lam_re, od_lam_im, od_log_dt, od_b_re, od_b_im, od_c_re, od_c_im, od_d, od_w_glu, od_b_glu, od_w_out):
    depth = norm_g.shape[0]
    h = x.reshape(SEQ, D_MODEL)
    for l in range(depth):
        i = l // 2
        g = norm_g[l]
        h = _ffn(h, _row(g[0]), _row(g[1]), ffn_w_gate, ffn_w_up, ffn_w_down, l, 0)
        if l % 2 == 0:
            w_g2 = jnp.pad(ev_w_gate2[i], ((0, LANES - B_GATE_RANK), (0, 0))).astype(BF16)
            p, gk = _ev_in(h, _row(g[2]), jnp.swapaxes(ev_w_in, 1, 2), w_g2, _row(ev_b_gate[i]), i)
            bs_full = jnp.repeat(ev_b_s[i].T, A_HEAD_DIM, axis=1).astype(F32)
            y = _ev_mix(p, gk, _row(ev_ln_g[i]), _row(ev_ln_b[i]), ev_w_s[i], bs_full,
                        _row(ev_gla_norm_g[i]))
            h = _ev_out(y, ev_w_out, _row(g[3]), h, i)
        else:
            u_t, u_tb = _od_in(h, _row(g[2]), od_w_in, i)
            prm = _s5_params(od_lam_re[i], od_lam_im[i], od_log_dt[i], od_b_re[i], od_b_im[i],
                             od_c_re[i], od_c_im[i])
            y_t = _s5(u_tb, prm)
            h = _od_out(y_t, u_t, od_d[i].reshape(C_WIDTH, 1), od_w_glu, od_b_glu[i].reshape(C_WIDTH, 1),
                        od_w_out, _row(g[3]), h, i)
        h = _ffn(h, _row(g[4]), _row(g[5]), ffn_w_gate, ffn_w_up, ffn_w_down, l, 1)
    return h.reshape(x.shape)
```

```python
import jax
import jax.numpy as jnp
from jax import lax
from jax.experimental import pallas as pl
from jax.experimental.pallas import tpu as pltpu

F32 = jnp.float32
BF16 = jnp.bfloat16

D_MODEL = 2048
SEQ = 8192
D_FF = 5632
EPS = 1e-6

A_HEADS = 8
A_HEAD_DIM = 128
A_WIDTH = 1024
A_CHUNK = 128
B_HEADS = 4
B_DK = 128
B_DV = 256
B_KEY = 512
B_VAL = 1024
B_GATE_RANK = 16
B_GATE_TAU = 16.0
B_CHUNK = 64
EVEN_MAIN = 2 * A_WIDTH + 2 * B_KEY + 2 * B_VAL
C_WIDTH = 1024
C_GROUP = 16
C_GROUPS = 64
C_STATE = 64

LANES = 128
VMEM_LIMIT = 56 * 1024 * 1024
FFN_VMEM_LIMIT = 60 * 1024 * 1024

ROW_TILE = 512
EV_IN_ROWS = 1024
FFN_ROWS = 1024
FFN_SLABS = 8
FFN_EPI_ROWS = 16
FF_TILE = 512
S5_CHUNK = 32
S5_NC = SEQ // S5_CHUNK
S5_COLS = C_GROUP * S5_CHUNK
S5_GB = 8


def _rms(x, g):
    return x * lax.rsqrt(jnp.mean(x * x, axis=-1, keepdims=True) + EPS) * g


def _dot(a, b):
    return jnp.dot(a, b, preferred_element_type=F32)


def _dot_nt(a, b):
    return lax.dot_general(a, b, (((1,), (1,)), ((), ())), preferred_element_type=F32)


def _dot_tn(a, b):
    return lax.dot_general(a, b, (((0,), (0,)), ((), ())), preferred_element_type=F32)


def _params(*sem, vmem=VMEM_LIMIT):
    return pltpu.CompilerParams(dimension_semantics=sem, vmem_limit_bytes=vmem)


def _ffn_kernel(h_hbm, gpre_ref, gpost_ref, wg_ref, wu_ref, wd_ref, o_hbm,
                h_ref, acc_ref, xn_ref, h_sem, o_sem):
    i, j = pl.program_id(0), pl.program_id(1)
    n_i, n_j = pl.num_programs(0), pl.num_programs(1)
    tm = h_ref.shape[0]
    slab = tm // FFN_SLABS

    def h_copy(tile, r):
        return pltpu.make_async_copy(h_hbm.at[pl.ds(tile * tm + r * slab, slab), :],
                                     h_ref.at[pl.ds(r * slab, slab), :], h_sem.at[r])

    def o_copy(tile, r):
        return pltpu.make_async_copy(acc_ref.at[pl.ds(r * slab, slab), :],
                                     o_hbm.at[pl.ds(tile * tm + r * slab, slab), :], o_sem.at[r])

    @pl.when(j == 0)
    def _():
        @pl.when(i == 0)
        def _():
            for r in range(FFN_SLABS):
                h_copy(0, r).start()

        for r in range(FFN_SLABS):
            rows = pl.ds(r * slab, slab)
            h_copy(i, r).wait()
            xn_ref[rows, :] = _rms(h_ref[rows, :], gpre_ref[...]).astype(BF16)

            @pl.when(i > 0)
            def _():
                o_copy(i - 1, r).wait()
            acc_ref[rows, :] = jnp.zeros((slab, acc_ref.shape[1]), F32)

    xn = xn_ref[...]
    hf = wg_ref.shape[1] // 2
    gate_a = _dot(xn, wg_ref[:, :hf].astype(BF16))
    up_a = _dot(xn, wu_ref[:, :hf].astype(BF16))
    gate_b = _dot(xn, wg_ref[:, hf:].astype(BF16))
    up_b = _dot(xn, wu_ref[:, hf:].astype(BF16))
    act_a = (jax.nn.silu(gate_a) * up_a).astype(BF16)
    act_b = (jax.nn.silu(gate_b) * up_b).astype(BF16)
    acc_ref[...] += _dot(act_a, wd_ref[:hf, :].astype(BF16)) + _dot(act_b, wd_ref[hf:, :].astype(BF16))

    @pl.when(j == n_j - 1)
    def _():
        for r in range(FFN_SLABS):
            for q in range(slab // FFN_EPI_ROWS):
                rows = pl.ds(r * slab + q * FFN_EPI_ROWS, FFN_EPI_ROWS)
                acc_ref[rows, :] = h_ref[rows, :] + 0.5 * _rms(acc_ref[rows, :], gpost_ref[...])
            o_copy(i, r).start()

            @pl.when(i + 1 < n_i)
            def _():
                h_copy(i + 1, r).start()

        @pl.when(i == n_i - 1)
        def _():
            for r in range(FFN_SLABS):
                o_copy(i, r).wait()


def _ffn(h, g_pre, g_post, wg, wu, wd, layer, half):
    tm, tf = FFN_ROWS, FF_TILE
    return pl.pallas_call(
        _ffn_kernel,
        grid=(SEQ // tm, D_FF // tf),
        in_specs=[
            pl.BlockSpec(memory_space=pl.ANY),
            pl.BlockSpec((1, D_MODEL), lambda i, j: (0, 0)),
            pl.BlockSpec((1, D_MODEL), lambda i, j: (0, 0)),
            pl.BlockSpec((None, None, D_MODEL, tf), lambda i, j: (layer, half, 0, j)),
            pl.BlockSpec((None, None, D_MODEL, tf), lambda i, j: (layer, half, 0, j)),
            pl.BlockSpec((None, None, tf, D_MODEL), lambda i, j: (layer, half, j, 0)),
        ],
        out_specs=pl.BlockSpec(memory_space=pl.ANY),
        out_shape=jax.ShapeDtypeStruct((SEQ, D_MODEL), F32),
        scratch_shapes=[
            pltpu.VMEM((tm, D_MODEL), F32),
            pltpu.VMEM((tm, D_MODEL), F32),
            pltpu.VMEM((tm, D_MODEL), BF16),
            pltpu.SemaphoreType.DMA((FFN_SLABS,)),
            pltpu.SemaphoreType.DMA((FFN_SLABS,)),
        ],
        compiler_params=_params("arbitrary", "arbitrary", vmem=FFN_VMEM_LIMIT),
        name="ffn",
    )(h, g_pre, g_post, wg, wu, wd)


def _ev_in_kernel(h_ref, g_ref, w_ref, wlr_ref, wg2_ref, bg_ref, lng_ref, lnb_ref, p_ref, gk_ref, xn_ref):
    j = pl.program_id(1)

    @pl.when(j == 0)
    def _():
        xn = _rms(h_ref[...], g_ref[...]).astype(BF16)
        xn_ref[...] = xn
        lane = lax.broadcasted_iota(jnp.int32, (xn.shape[0], LANES), 1)
        glr = jnp.where(lane < B_GATE_RANK, _dot_nt(xn, wlr_ref[...].astype(BF16)), 0.0).astype(BF16)
        pre = _dot(glr, wg2_ref[...]) + bg_ref[...]
        gk_ref[...] = jax.nn.log_sigmoid(pre) * (1.0 / B_GATE_TAU)

    def project():
        return _dot_nt(xn_ref[...], w_ref[...].astype(BF16))

    @pl.when(j == 0)
    def _():
        p_ref[...] = jax.nn.gelu(project())

    @pl.when(j == 1)
    def _():
        va = jax.nn.gelu(project())
        vc = va - jnp.mean(va, axis=-1, keepdims=True)
        vln = vc * lax.rsqrt(jnp.mean(vc * vc, axis=-1, keepdims=True) + EPS)
        p_ref[...] = vln * lng_ref[...] + lnb_ref[...]

    @pl.when(jnp.logical_or(j == 2, j == 3))
    def _():
        p_ref[...] = project()

    @pl.when(j == 4)
    def _():
        p_ref[...] = jax.nn.silu(project())


def _ev_in(h, g, w_in_t, w_g2, b_gate, ln_g, ln_b, layer):
    tm, tn = EV_IN_ROWS, 1024
    return pl.pallas_call(
        _ev_in_kernel,
        grid=(SEQ // tm, EVEN_MAIN // tn),
        in_specs=[
            pl.BlockSpec((tm, D_MODEL), lambda i, j: (i, 0)),
            pl.BlockSpec((1, D_MODEL), lambda i, j: (0, 0)),
            pl.BlockSpec((None, tn, D_MODEL), lambda i, j: (layer, j, 0)),
            pl.BlockSpec((None, LANES, D_MODEL), lambda i, j: (layer, EVEN_MAIN // LANES, 0)),
            pl.BlockSpec((LANES, B_KEY), lambda i, j: (0, 0)),
            pl.BlockSpec((1, B_KEY), lambda i, j: (0, 0)),
            pl.BlockSpec((1, A_WIDTH), lambda i, j: (0, 0)),
            pl.BlockSpec((1, A_WIDTH), lambda i, j: (0, 0)),
        ],
        out_specs=[
            pl.BlockSpec((tm, tn), lambda i, j: (i, j)),
            pl.BlockSpec((tm, B_KEY), lambda i, j: (i, 0)),
        ],
        out_shape=[
            jax.ShapeDtypeStruct((SEQ, EVEN_MAIN), F32),
            jax.ShapeDtypeStruct((SEQ, B_KEY), F32),
        ],
        scratch_shapes=[pltpu.VMEM((tm, D_MODEL), BF16)],
        compiler_params=_params("parallel", "arbitrary"),
        name="ev_in",
    )(h, g, w_in_t, w_in_t, w_g2, b_gate, ln_g, ln_b)


def _ev_mix_kernel(za_ref, q_ref, k_ref, v_ref, r_ref, gk_ref, ws_ref, bs_ref,
                   gng_ref, y_ref, vln_ref, st_ref):
    tm = za_ref.shape[0]

    @pl.when(pl.program_id(0) == 0)
    def _():
        st_ref[...] = jnp.zeros_like(st_ref)

    vln_ref[...] = za_ref[:, A_WIDTH:].astype(BF16)

    n_a = tm // A_CHUNK
    row = lax.broadcasted_iota(jnp.int32, (A_CHUNK, A_CHUNK), 0)
    col = lax.broadcasted_iota(jnp.int32, (A_CHUNK, A_CHUNK), 1)
    for hd in range(A_HEADS):
        cs = slice(hd * A_HEAD_DIM, (hd + 1) * A_HEAD_DIM)
        w = jnp.where(row >= col, ws_ref[hd], 0.0).astype(BF16)
        rhs = jnp.concatenate(
            [vln_ref[c * A_CHUNK:(c + 1) * A_CHUNK, cs] for c in range(n_a)], axis=1)
        s = _dot(w, rhs)
        for c in range(n_a):
            rs = slice(c * A_CHUNK, (c + 1) * A_CHUNK)
            sc = s[:, c * A_HEAD_DIM:(c + 1) * A_HEAD_DIM] + bs_ref[:, cs]
            y_ref[rs, cs] = (za_ref[rs, cs] * sc).astype(BF16)

    n_b = tm // B_CHUNK
    r64 = lax.broadcasted_iota(jnp.int32, (B_CHUNK, B_CHUNK), 0)
    c64 = lax.broadcasted_iota(jnp.int32, (B_CHUNK, B_CHUNK), 1)
    causal = r64 >= c64
    tri = jnp.where(causal, 1.0, 0.0).astype(BF16)
    scale = B_DK ** -0.5
    for c in range(n_b):
        rs = slice(c * B_CHUNK, (c + 1) * B_CHUNK)
        g = gk_ref[rs, :]
        g_hi = g.astype(BF16)
        g_lo = (g - g_hi.astype(F32)).astype(BF16)
        bcum = _dot(tri, g_hi) + _dot(tri, g_lo)
        b_last = bcum[B_CHUNK - 1:B_CHUNK, :]
        q = q_ref[rs, :] * scale
        k = k_ref[rs, :]
        q_dec = (q * jnp.exp(bcum)).astype(BF16)
        k_inv = (k * jnp.exp(-bcum)).astype(BF16)
        k_end = (k * jnp.exp(b_last - bcum)).astype(BF16)
        decay = jnp.exp(b_last)
        for hd in range(B_HEADS):
            ks = slice(hd * B_DK, (hd + 1) * B_DK)
            vs = slice(hd * B_DV, (hd + 1) * B_DV)
            v = v_ref[rs, vs].astype(BF16)
            scores = jnp.where(causal, _dot_nt(q_dec[:, ks], k_inv[:, ks]), 0.0).astype(BF16)
            st = st_ref[hd]
            o = _dot(scores, v) + _dot_nt(q_dec[:, ks], st.astype(BF16))
            st_ref[hd] = decay[:, ks] * st + _dot_tn(v, k_end[:, ks])
            o = o * lax.rsqrt(jnp.mean(o * o, axis=-1, keepdims=True) + EPS) * gng_ref[...]
            y_ref[rs, A_WIDTH + hd * B_DV:A_WIDTH + (hd + 1) * B_DV] = (
                o * r_ref[rs, vs]).astype(BF16)


def _ev_mix(p, gk, w_s, bs_full, gla_norm_g):
    tm = ROW_TILE
    return pl.pallas_call(
        _ev_mix_kernel,
        grid=(SEQ // tm,),
        in_specs=[
            pl.BlockSpec((tm, 2 * A_WIDTH), lambda i: (i, 0)),
            pl.BlockSpec((tm, B_KEY), lambda i: (i, 2 * A_WIDTH // B_KEY)),
            pl.BlockSpec((tm, B_KEY), lambda i: (i, 2 * A_WIDTH // B_KEY + 1)),
            pl.BlockSpec((tm, B_VAL), lambda i: (i, (2 * A_WIDTH + 2 * B_KEY) // B_VAL)),
            pl.BlockSpec((tm, B_VAL), lambda i: (i, (2 * A_WIDTH + 2 * B_KEY) // B_VAL + 1)),
            pl.BlockSpec((tm, B_KEY), lambda i: (i, 0)),
            pl.BlockSpec((A_HEADS, A_CHUNK, A_CHUNK), lambda i: (0, 0, 0)),
            pl.BlockSpec((A_CHUNK, A_WIDTH), lambda i: (0, 0)),
            pl.BlockSpec((1, B_DV), lambda i: (0, 0)),
        ],
        out_specs=pl.BlockSpec((tm, A_WIDTH + B_VAL), lambda i: (i, 0)),
        out_shape=jax.ShapeDtypeStruct((SEQ, A_WIDTH + B_VAL), BF16),
        scratch_shapes=[
            pltpu.VMEM((tm, A_WIDTH), BF16),
            pltpu.VMEM((B_HEADS, B_DV, B_DK), F32),
        ],
        compiler_params=_params("arbitrary"),
        name="ev_mix",
    )(p, p, p, p, p, gk, w_s, bs_full, gla_norm_g)


def _ev_out_kernel(y_ref, w_ref, g_ref, h_ref, o_ref):
    o_ref[...] = h_ref[...] + _rms(_dot(y_ref[...], w_ref[...].astype(BF16)), g_ref[...])


def _ev_out(y, w, g, h, layer):
    tm = ROW_TILE
    kdim = y.shape[1]
    return pl.pallas_call(
        _ev_out_kernel,
        grid=(SEQ // tm,),
        in_specs=[
            pl.BlockSpec((tm, kdim), lambda i: (i, 0)),
            pl.BlockSpec((None, kdim, D_MODEL), lambda i: (layer, 0, 0), pipeline_mode=pl.Buffered(1)),
            pl.BlockSpec((1, D_MODEL), lambda i: (0, 0)),
            pl.BlockSpec((tm, D_MODEL), lambda i: (i, 0)),
        ],
        out_specs=pl.BlockSpec((tm, D_MODEL), lambda i: (i, 0)),
        out_shape=jax.ShapeDtypeStruct((SEQ, D_MODEL), F32),
        compiler_params=_params("parallel"),
        name="ev_out",
    )(y, w, g, h)


def _od_in_kernel(h_ref, g_ref, w_ref, ut_ref, utb_ref):
    ut = lax.dot_general(w_ref[...].astype(BF16), _rms(h_ref[...], g_ref[...]).astype(BF16),
                         (((0,), (1,)), ((), ())), preferred_element_type=F32)
    ut_ref[...] = ut
    utb_ref[...] = ut.astype(BF16)


def _od_in(h, g, w, layer):
    tm = ROW_TILE
    return pl.pallas_call(
        _od_in_kernel,
        grid=(SEQ // tm,),
        in_specs=[
            pl.BlockSpec((tm, D_MODEL), lambda i: (i, 0)),
            pl.BlockSpec((1, D_MODEL), lambda i: (0, 0)),
            pl.BlockSpec((None, D_MODEL, C_WIDTH), lambda i: (layer, 0, 0), pipeline_mode=pl.Buffered(1)),
        ],
        out_specs=[
            pl.BlockSpec((C_WIDTH, tm), lambda i: (0, i)),
            pl.BlockSpec((C_WIDTH, tm), lambda i: (0, i)),
        ],
        out_shape=[
            jax.ShapeDtypeStruct((C_WIDTH, SEQ), F32),
            jax.ShapeDtypeStruct((C_WIDTH, SEQ), BF16),
        ],
        compiler_params=_params("parallel"),
        name="od_in",
    )(h, g, w)


def _split3(x):
    hi = x.astype(BF16)
    r1 = x - hi.astype(F32)
    mid = r1.astype(BF16)
    lo = (r1 - mid.astype(F32)).astype(BF16)
    return hi, mid, lo


def _dot_f32(a, b):
    a0, a1, a2 = _split3(a)
    b0, b1, b2 = _split3(b)
    return (_dot(a0, b0) + (_dot(a0, b1) + _dot(a1, b0))
            + (_dot(a0, b2) + _dot(a1, b1) + _dot(a2, b0)))


def _lane_expand(x, e):
    x0, x1, x2 = _split3(x)
    return _dot(x0, e) + _dot(x1, e) + _dot(x2, e)


def _s5_group_operators(ar, ai, bbr, bbi, bbt, crt, cit):
    tau = lax.broadcasted_iota(jnp.int32, (C_STATE, LANES), 1) & (S5_CHUNK - 1)
    expand = (lax.broadcasted_iota(jnp.int32, (C_GROUP, S5_COLS), 1) // S5_CHUNK
              == lax.broadcasted_iota(jnp.int32, (C_GROUP, S5_COLS), 0)).astype(BF16)
    one = jnp.ones((C_STATE, LANES), F32)
    zero = jnp.zeros((C_STATE, LANES), F32)
    pr, pi = one, zero
    rr, ri = one, zero
    fr, fi = ar, ai
    for b in range(S5_CHUNK.bit_length() - 1):
        bit = ((tau >> b) & 1) == 1
        pr, pi = jnp.where(bit, pr * fr - pi * fi, pr), jnp.where(bit, pr * fi + pi * fr, pi)
        rr, ri = jnp.where(bit, rr, rr * fr - ri * fi), jnp.where(bit, ri, rr * fi + ri * fr)
        fr, fi = fr * fr - fi * fi, 2.0 * (fr * fi)
    wide = lambda a: jnp.concatenate([a] * (S5_COLS // LANES), axis=1)
    pr, pi, rr, ri, ar, ai = wide(pr), wide(pi), wide(rr), wide(ri), wide(ar), wide(ai)
    cr = _lane_expand(crt, expand)
    ci = _lane_expand(cit, expand)
    cer = cr * pr - ci * pi
    cei = cr * pi + ci * pr
    kv = _dot_f32(bbt, jnp.concatenate([cer, -cei], axis=0))
    qr = cer * ar - cei * ai
    qi = cer * ai + cei * ar
    q = jnp.concatenate([qr, -qi], axis=0).astype(BF16)
    br = _lane_expand(bbr, expand)
    bi = _lane_expand(bbi, expand)
    ptr = br * rr - bi * ri
    pti = br * ri + bi * rr
    pt = jnp.concatenate([ptr, pti], axis=0).astype(BF16)
    pts = jnp.concatenate([pti, ptr], axis=0).astype(BF16)
    return kv, pt, pts, q


def _s5_kernel(ut2_ref, ar_ref, ai_ref, bbr_ref, bbi_ref, bbt_ref, crt_ref, cit_ref,
               a1_ref, a2_ref, a2s_ref, yt2_ref, ut_ref, yt_ref, u_ref, m_ref, q_ref, inc_ref, incs_ref,
               xs_ref, y_ref):
    gb = u_ref.shape[0]
    w = 2 * C_STATE
    ut_ref[...] = ut2_ref[...].reshape(ut_ref.shape)
    per = LANES // S5_CHUNK
    nsup = S5_NC // per
    keep = ((lax.broadcasted_iota(jnp.int32, (S5_CHUNK, LANES), 1) & (S5_CHUNK - 1))
            >= lax.broadcasted_iota(jnp.int32, (S5_CHUNK, LANES), 0))
    for gi in range(gb):
        for cp in range(C_GROUP):
            for k in range(per):
                u_ref[gi, k * nsup:(k + 1) * nsup, cp * S5_CHUNK:(cp + 1) * S5_CHUNK] = (
                    ut_ref[gi * C_GROUP + cp, :, k * S5_CHUNK:(k + 1) * S5_CHUNK])
        kv, pt, pts, q = _s5_group_operators(ar_ref[gi], ai_ref[gi], bbr_ref[gi], bbi_ref[gi], bbt_ref[gi],
                                             crt_ref[gi], cit_ref[gi])
        q_ref[gi] = q
        for cp in range(C_GROUP):
            taps = jnp.broadcast_to(kv[cp:cp + 1, :], (S5_CHUNK, S5_COLS))
            for v in range(S5_COLS // LANES):
                tile = pltpu.roll(taps[:, v * LANES:(v + 1) * LANES], 0, 1, stride=1, stride_axis=0)
                m_ref[gi, cp * S5_CHUNK:(cp + 1) * S5_CHUNK, v * LANES:(v + 1) * LANES] = (
                    jnp.where(keep, tile, 0.0).astype(BF16))
        u = u_ref[gi]
        inc_ref[:, gi * w:(gi + 1) * w] = _dot_nt(u, pt)
        incs_ref[:, gi * w:(gi + 1) * w] = _dot_nt(u, pts)

    a1 = a1_ref[...]
    a2 = a2_ref[...]
    a2s = a2s_ref[...]

    def step(n, carry):
        x, xs = carry
        row = pl.ds((n & (per - 1)) * nsup + (n >> (per.bit_length() - 1)), 1)
        xs_ref[row, :] = x
        x_new = a1 * x + a2 * xs + inc_ref[row, :]
        xs_new = a1 * xs + a2s * x + incs_ref[row, :]
        return x_new, xs_new

    zero = jnp.zeros((1, gb * w), F32)
    lax.fori_loop(0, S5_NC, step, (zero, zero))

    for gi in range(gb):
        xst = xs_ref[:, gi * w:(gi + 1) * w].astype(BF16)
        y_ref[...] = _dot(u_ref[gi], m_ref[gi]) + _dot(xst, q_ref[gi])
        for c in range(C_GROUP):
            for k in range(per):
                yt_ref[gi * C_GROUP + c, :, k * S5_CHUNK:(k + 1) * S5_CHUNK] = (
                    y_ref[k * nsup:(k + 1) * nsup, c * S5_CHUNK:(c + 1) * S5_CHUNK])
    yt2_ref[...] = yt_ref[...].reshape(yt2_ref.shape)


def _s5(u_tb, prm):
    gb = S5_GB
    w = 2 * C_STATE
    grp3 = lambda i: (i, 0, 0)
    return pl.pallas_call(
        _s5_kernel,
        grid=(C_GROUPS // gb,),
        in_specs=[
            pl.BlockSpec((gb * C_GROUP, SEQ), lambda i: (i, 0)),
            pl.BlockSpec((gb, C_STATE, LANES), grp3),
            pl.BlockSpec((gb, C_STATE, LANES), grp3),
            pl.BlockSpec((gb, C_STATE, C_GROUP), grp3),
            pl.BlockSpec((gb, C_STATE, C_GROUP), grp3),
            pl.BlockSpec((gb, C_GROUP, w), grp3),
            pl.BlockSpec((gb, C_STATE, C_GROUP), grp3),
            pl.BlockSpec((gb, C_STATE, C_GROUP), grp3),
            pl.BlockSpec((1, gb * w), lambda i: (0, i)),
            pl.BlockSpec((1, gb * w), lambda i: (0, i)),
            pl.BlockSpec((1, gb * w), lambda i: (0, i)),
        ],
        out_specs=pl.BlockSpec((gb * C_GROUP, SEQ), lambda i: (i, 0)),
        out_shape=jax.ShapeDtypeStruct((C_WIDTH, SEQ), F32),
        scratch_shapes=[
            pltpu.VMEM((gb * C_GROUP, SEQ // LANES, LANES), BF16),
            pltpu.VMEM((gb * C_GROUP, SEQ // LANES, LANES), F32),
            pltpu.VMEM((gb, S5_NC, S5_COLS), BF16),
            pltpu.VMEM((gb, S5_COLS, S5_COLS), BF16),
            pltpu.VMEM((gb, w, S5_COLS), BF16),
            pltpu.VMEM((S5_NC, gb * w), F32),
            pltpu.VMEM((S5_NC, gb * w), F32),
            pltpu.VMEM((S5_NC, gb * w), F32),
            pltpu.VMEM((S5_NC, S5_COLS), F32),
        ],
        compiler_params=_params("parallel"),
        name="s5",
    )(u_tb, *prm)


def _s5_params(lam_re, lam_im, log_dt, b_re, b_im, c_re, c_im):
    lr = jnp.minimum(lam_re, -1e-4)
    li = lam_im
    dt = jnp.exp(log_dt)[:, None]
    mag = jnp.exp(lr * dt)
    ar = mag * jnp.cos(li * dt)
    ai = mag * jnp.sin(li * dt)
    den = lr * lr + li * li
    nr = ar - 1.0
    cr = (nr * lr + ai * li) / den
    ci = (ai * lr - nr * li) / den
    bbr = cr[..., None] * b_re - ci[..., None] * b_im
    bbi = cr[..., None] * b_im + ci[..., None] * b_re
    bbt = jnp.concatenate([bbr, bbi], axis=1).transpose(0, 2, 1)
    atr, ati = ar, ai
    for _ in range(S5_CHUNK.bit_length() - 1):
        atr, ati = atr * atr - ati * ati, 2.0 * (atr * ati)
    a1 = jnp.concatenate([atr, atr], axis=-1).reshape(1, -1)
    a2 = jnp.concatenate([-ati, ati], axis=-1).reshape(1, -1)
    a2s = jnp.concatenate([ati, -ati], axis=-1).reshape(1, -1)
    bcast = lambda a: jnp.broadcast_to(a[..., None], a.shape + (LANES,))
    return (bcast(ar), bcast(ai), bbr, bbi, bbt, c_re.transpose(0, 2, 1), c_im.transpose(0, 2, 1),
            a1, a2, a2s)


def _od_out_kernel(yt_ref, ut_ref, d_ref, wglu_ref, bglu_ref, wout_ref, g_ref, h_ref, o_ref):
    y = yt_ref[...] + d_ref[...] * ut_ref[...]
    z = jax.nn.gelu(y)
    gate = jax.nn.sigmoid(_dot_tn(wglu_ref[...].astype(BF16), z.astype(BF16)) + bglu_ref[...])
    m = _dot_tn((z * gate).astype(BF16), wout_ref[...].astype(BF16))
    o_ref[...] = h_ref[...] + _rms(m, g_ref[...])


def _od_out(y_t, u_t, d, w_glu, b_glu, w_out, g, h, layer):
    tm = ROW_TILE
    return pl.pallas_call(
        _od_out_kernel,
        grid=(SEQ // tm,),
        in_specs=[
            pl.BlockSpec((C_WIDTH, tm), lambda i: (0, i)),
            pl.BlockSpec((C_WIDTH, tm), lambda i: (0, i)),
            pl.BlockSpec((C_WIDTH, 1), lambda i: (0, 0)),
            pl.BlockSpec((None, C_WIDTH, C_WIDTH), lambda i: (layer, 0, 0), pipeline_mode=pl.Buffered(1)),
            pl.BlockSpec((C_WIDTH, 1), lambda i: (0, 0)),
            pl.BlockSpec((None, C_WIDTH, D_MODEL), lambda i: (layer, 0, 0), pipeline_mode=pl.Buffered(1)),
            pl.BlockSpec((1, D_MODEL), lambda i: (0, 0)),
            pl.BlockSpec((tm, D_MODEL), lambda i: (i, 0)),
        ],
        out_specs=pl.BlockSpec((tm, D_MODEL), lambda i: (i, 0)),
        out_shape=jax.ShapeDtypeStruct((SEQ, D_MODEL), F32),
        compiler_params=_params("parallel"),
        name="od_out",
    )(y_t, u_t, d, w_glu, b_glu, w_out, g, h)


def _row(v):
    return v.reshape(1, -1).astype(F32)


def kernel(x, norm_g, ffn_w_gate, ffn_w_up, ffn_w_down, ev_w_in, ev_ln_g, ev_ln_b, ev_w_s, ev_b_s, ev_w_gate2, ev_b_gate, ev_gla_norm_g, ev_w_out, od_w_in, od_lam_re, od_lam_im, od_log_dt, od_b_re, od_b_im, od_c_re, od_c_im, od_d, od_w_glu, od_b_glu, od_w_out):
    depth = norm_g.shape[0]
    h = x.reshape(SEQ, D_MODEL)
    for l in range(depth):
        i = l // 2
        g = norm_g[l]
        h = _ffn(h, _row(g[0]), _row(g[1]), ffn_w_gate, ffn_w_up, ffn_w_down, l, 0)
        if l % 2 == 0:
            w_g2 = jnp.pad(ev_w_gate2[i], ((0, LANES - B_GATE_RANK), (0, 0))).astype(BF16)
            p, gk = _ev_in(h, _row(g[2]), jnp.swapaxes(ev_w_in, 1, 2), w_g2, _row(ev_b_gate[i]),
                           _row(ev_ln_g[i]), _row(ev_ln_b[i]), i)
            bs_full = jnp.repeat(ev_b_s[i].T, A_HEAD_DIM, axis=1).astype(F32)
            y = _ev_mix(p, gk, ev_w_s[i], bs_full, _row(ev_gla_norm_g[i]))
            h = _ev_out(y, ev_w_out, _row(g[3]), h, i)
        else:
            u_t, u_tb = _od_in(h, _row(g[2]), od_w_in, i)
            prm = _s5_params(od_lam_re[i], od_lam_im[i], od_log_dt[i], od_b_re[i], od_b_im[i],
                             od_c_re[i], od_c_im[i])
            y_t = _s5(u_tb, prm)
            h = _od_out(y_t, u_t, od_d[i].reshape(C_WIDTH, 1), od_w_glu, od_b_glu[i].reshape(C_WIDTH, 1),
                        od_w_out, _row(g[3]), h, i)
        h = _ffn(h, _row(g[4]), _row(g[5]), ffn_w_gate, ffn_w_up, ffn_w_down, l, 1)
    return h.reshape(x.shape)
```

```python
import jax
import jax.numpy as jnp
from jax import lax
from jax.experimental import pallas as pl
from jax.experimental.pallas import tpu as pltpu

F32 = jnp.float32
BF16 = jnp.bfloat16

D_MODEL = 2048
SEQ = 8192
D_FF = 5632
EPS = 1e-6

A_HEADS = 8
A_HEAD_DIM = 128
A_WIDTH = 1024
A_CHUNK = 128
B_HEADS = 4
B_DK = 128
B_DV = 256
B_KEY = 512
B_VAL = 1024
B_GATE_RANK = 16
B_GATE_TAU = 16.0
B_CHUNK = 64
EVEN_MAIN = 2 * A_WIDTH + 2 * B_KEY + 2 * B_VAL
C_WIDTH = 1024
C_GROUP = 16
C_GROUPS = 64
C_STATE = 64

LANES = 128
VMEM_LIMIT = 56 * 1024 * 1024
FFN_VMEM_LIMIT = 60 * 1024 * 1024

ROW_TILE = 512
EV_IN_ROWS = 1024
FFN_ROWS = 1024
FFN_SLABS = 8
FFN_EPI_ROWS = 8
FF_TILE = 512
S5_CHUNK = 32
S5_NC = SEQ // S5_CHUNK
S5_COLS = C_GROUP * S5_CHUNK
S5_GB = 8


def _rms(x, g):
    return x * lax.rsqrt(jnp.mean(x * x, axis=-1, keepdims=True) + EPS) * g


def _dot(a, b):
    return jnp.dot(a, b, preferred_element_type=F32)


def _dot_nt(a, b):
    return lax.dot_general(a, b, (((1,), (1,)), ((), ())), preferred_element_type=F32)


def _dot_tn(a, b):
    return lax.dot_general(a, b, (((0,), (0,)), ((), ())), preferred_element_type=F32)


def _params(*sem, vmem=VMEM_LIMIT):
    return pltpu.CompilerParams(dimension_semantics=sem, vmem_limit_bytes=vmem)


def _ffn_kernel(h_hbm, gpre_ref, gpost_ref, wg_ref, wu_ref, wd_ref, o_hbm,
                h_ref, acc_ref, xn_ref, h_sem, o_sem):
    i, j = pl.program_id(0), pl.program_id(1)
    n_i, n_j = pl.num_programs(0), pl.num_programs(1)
    tm = h_ref.shape[0]
    slab = tm // FFN_SLABS

    def h_copy(tile, r):
        return pltpu.make_async_copy(h_hbm.at[pl.ds(tile * tm + r * slab, slab), :],
                                     h_ref.at[pl.ds(r * slab, slab), :], h_sem.at[r])

    def o_copy(tile, r):
        return pltpu.make_async_copy(acc_ref.at[pl.ds(r * slab, slab), :],
                                     o_hbm.at[pl.ds(tile * tm + r * slab, slab), :], o_sem.at[r])

    @pl.when(j == 0)
    def _():
        @pl.when(i == 0)
        def _():
            for r in range(FFN_SLABS):
                h_copy(0, r).start()

        for r in range(FFN_SLABS):
            rows = pl.ds(r * slab, slab)
            h_copy(i, r).wait()
            xn_ref[rows, :] = _rms(h_ref[rows, :], gpre_ref[...]).astype(BF16)

            @pl.when(i > 0)
            def _():
                o_copy(i - 1, r).wait()
            acc_ref[rows, :] = jnp.zeros((slab, acc_ref.shape[1]), F32)

    xn = xn_ref[...]
    hf = wg_ref.shape[1] // 2
    gate_a = _dot(xn, wg_ref[:, :hf].astype(BF16))
    up_a = _dot(xn, wu_ref[:, :hf].astype(BF16))
    gate_b = _dot(xn, wg_ref[:, hf:].astype(BF16))
    up_b = _dot(xn, wu_ref[:, hf:].astype(BF16))
    act_a = (jax.nn.silu(gate_a) * up_a).astype(BF16)
    act_b = (jax.nn.silu(gate_b) * up_b).astype(BF16)
    acc_ref[...] += _dot(act_a, wd_ref[:hf, :].astype(BF16)) + _dot(act_b, wd_ref[hf:, :].astype(BF16))

    @pl.when(j == n_j - 1)
    def _():
        for r in range(FFN_SLABS):
            for q in range(slab // FFN_EPI_ROWS):
                rows = pl.ds(r * slab + q * FFN_EPI_ROWS, FFN_EPI_ROWS)
                acc_ref[rows, :] = h_ref[rows, :] + 0.5 * _rms(acc_ref[rows, :], gpost_ref[...])
            o_copy(i, r).start()

            @pl.when(i + 1 < n_i)
            def _():
                h_copy(i + 1, r).start()

        @pl.when(i == n_i - 1)
        def _():
            for r in range(FFN_SLABS):
                o_copy(i, r).wait()


def _ffn(h, g_pre, g_post, wg, wu, wd, layer, half):
    tm, tf = FFN_ROWS, FF_TILE
    return pl.pallas_call(
        _ffn_kernel,
        grid=(SEQ // tm, D_FF // tf),
        in_specs=[
            pl.BlockSpec(memory_space=pl.ANY),
            pl.BlockSpec((1, D_MODEL), lambda i, j: (0, 0)),
            pl.BlockSpec((1, D_MODEL), lambda i, j: (0, 0)),
            pl.BlockSpec((None, None, D_MODEL, tf), lambda i, j: (layer, half, 0, j)),
            pl.BlockSpec((None, None, D_MODEL, tf), lambda i, j: (layer, half, 0, j)),
            pl.BlockSpec((None, None, tf, D_MODEL), lambda i, j: (layer, half, j, 0)),
        ],
        out_specs=pl.BlockSpec(memory_space=pl.ANY),
        out_shape=jax.ShapeDtypeStruct((SEQ, D_MODEL), F32),
        scratch_shapes=[
            pltpu.VMEM((tm, D_MODEL), F32),
            pltpu.VMEM((tm, D_MODEL), F32),
            pltpu.VMEM((tm, D_MODEL), BF16),
            pltpu.SemaphoreType.DMA((FFN_SLABS,)),
            pltpu.SemaphoreType.DMA((FFN_SLABS,)),
        ],
        compiler_params=_params("arbitrary", "arbitrary", vmem=FFN_VMEM_LIMIT),
        name="ffn",
    )(h, g_pre, g_post, wg, wu, wd)


def _ev_in_kernel(h_ref, g_ref, w_ref, wlr_ref, wg2_ref, bg_ref, lng_ref, lnb_ref, p_ref, gk_ref, xn_ref):
    j = pl.program_id(1)

    @pl.when(j == 0)
    def _():
        xn = _rms(h_ref[...], g_ref[...]).astype(BF16)
        xn_ref[...] = xn
        lane = lax.broadcasted_iota(jnp.int32, (xn.shape[0], LANES), 1)
        glr = jnp.where(lane < B_GATE_RANK, _dot_nt(xn, wlr_ref[...].astype(BF16)), 0.0).astype(BF16)
        pre = _dot(glr, wg2_ref[...]) + bg_ref[...]
        gk_ref[...] = jax.nn.log_sigmoid(pre) * (1.0 / B_GATE_TAU)

    def project():
        return _dot_nt(xn_ref[...], w_ref[...].astype(BF16))

    @pl.when(j == 0)
    def _():
        p_ref[...] = jax.nn.gelu(project())

    @pl.when(j == 1)
    def _():
        va = jax.nn.gelu(project())
        vc = va - jnp.mean(va, axis=-1, keepdims=True)
        vln = vc * lax.rsqrt(jnp.mean(vc * vc, axis=-1, keepdims=True) + EPS)
        p_ref[...] = vln * lng_ref[...] + lnb_ref[...]

    @pl.when(jnp.logical_or(j == 2, j == 3))
    def _():
        p_ref[...] = project()

    @pl.when(j == 4)
    def _():
        p_ref[...] = jax.nn.silu(project())


def _ev_in(h, g, w_in_t, w_g2, b_gate, ln_g, ln_b, layer):
    tm, tn = EV_IN_ROWS, A_WIDTH
    assert tn == 2 * B_KEY == B_VAL, "the kernel's per-block activations assume blocks u | v | q,k | v | r"
    return pl.pallas_call(
        _ev_in_kernel,
        grid=(SEQ // tm, EVEN_MAIN // tn),
        in_specs=[
            pl.BlockSpec((tm, D_MODEL), lambda i, j: (i, 0)),
            pl.BlockSpec((1, D_MODEL), lambda i, j: (0, 0)),
            pl.BlockSpec((None, tn, D_MODEL), lambda i, j: (layer, j, 0)),
            pl.BlockSpec((None, LANES, D_MODEL), lambda i, j: (layer, EVEN_MAIN // LANES, 0)),
            pl.BlockSpec((LANES, B_KEY), lambda i, j: (0, 0)),
            pl.BlockSpec((1, B_KEY), lambda i, j: (0, 0)),
            pl.BlockSpec((1, A_WIDTH), lambda i, j: (0, 0)),
            pl.BlockSpec((1, A_WIDTH), lambda i, j: (0, 0)),
        ],
        out_specs=[
            pl.BlockSpec((tm, tn), lambda i, j: (i, j)),
            pl.BlockSpec((tm, B_KEY), lambda i, j: (i, 0)),
        ],
        out_shape=[
            jax.ShapeDtypeStruct((SEQ, EVEN_MAIN), F32),
            jax.ShapeDtypeStruct((SEQ, B_KEY), F32),
        ],
        scratch_shapes=[pltpu.VMEM((tm, D_MODEL), BF16)],
        compiler_params=_params("parallel", "arbitrary"),
        name="ev_in",
    )(h, g, w_in_t, w_in_t, w_g2, b_gate, ln_g, ln_b)


def _ev_mix_kernel(za_ref, q_ref, k_ref, v_ref, r_ref, gk_ref, ws_ref, bs_ref,
                   gng_ref, y_ref, vln_ref, st_ref):
    tm = za_ref.shape[0]

    @pl.when(pl.program_id(0) == 0)
    def _():
        st_ref[...] = jnp.zeros_like(st_ref)

    vln_ref[...] = za_ref[:, A_WIDTH:].astype(BF16)

    n_a = tm // A_CHUNK
    row = lax.broadcasted_iota(jnp.int32, (A_CHUNK, A_CHUNK), 0)
    col = lax.broadcasted_iota(jnp.int32, (A_CHUNK, A_CHUNK), 1)
    for hd in range(A_HEADS):
        cs = slice(hd * A_HEAD_DIM, (hd + 1) * A_HEAD_DIM)
        w = jnp.where(row >= col, ws_ref[hd], 0.0).astype(BF16)
        rhs = jnp.concatenate(
            [vln_ref[c * A_CHUNK:(c + 1) * A_CHUNK, cs] for c in range(n_a)], axis=1)
        s = _dot(w, rhs)
        for c in range(n_a):
            rs = slice(c * A_CHUNK, (c + 1) * A_CHUNK)
            sc = s[:, c * A_HEAD_DIM:(c + 1) * A_HEAD_DIM] + bs_ref[:, cs]
            y_ref[rs, cs] = (za_ref[rs, cs] * sc).astype(BF16)

    n_b = tm // B_CHUNK
    r64 = lax.broadcasted_iota(jnp.int32, (B_CHUNK, B_CHUNK), 0)
    c64 = lax.broadcasted_iota(jnp.int32, (B_CHUNK, B_CHUNK), 1)
    causal = r64 >= c64
    tri = jnp.where(causal, 1.0, 0.0).astype(BF16)
    scale = B_DK ** -0.5
    for c in range(n_b):
        rs = slice(c * B_CHUNK, (c + 1) * B_CHUNK)
        g = gk_ref[rs, :]
        g_hi = g.astype(BF16)
        g_lo = (g - g_hi.astype(F32)).astype(BF16)
        bcum = _dot(tri, g_hi) + _dot(tri, g_lo)
        b_last = bcum[B_CHUNK - 1:B_CHUNK, :]
        q = q_ref[rs, :] * scale
        k = k_ref[rs, :]
        q_dec = (q * jnp.exp(bcum)).astype(BF16)
        k_inv = (k * jnp.exp(-bcum)).astype(BF16)
        k_end = (k * jnp.exp(b_last - bcum)).astype(BF16)
        decay = jnp.exp(b_last)
        for hd in range(B_HEADS):
            ks = slice(hd * B_DK, (hd + 1) * B_DK)
            vs = slice(hd * B_DV, (hd + 1) * B_DV)
            v = v_ref[rs, vs].astype(BF16)
            scores = jnp.where(causal, _dot_nt(q_dec[:, ks], k_inv[:, ks]), 0.0).astype(BF16)
            st = st_ref[hd]
            o = _dot(scores, v) + _dot_nt(q_dec[:, ks], st.astype(BF16))
            st_ref[hd] = decay[:, ks] * st + _dot_tn(v, k_end[:, ks])
            o = o * lax.rsqrt(jnp.mean(o * o, axis=-1, keepdims=True) + EPS) * gng_ref[...]
            y_ref[rs, A_WIDTH + hd * B_DV:A_WIDTH + (hd + 1) * B_DV] = (
                o * r_ref[rs, vs]).astype(BF16)


def _ev_mix(p, gk, w_s, bs_full, gla_norm_g):
    tm = ROW_TILE
    return pl.pallas_call(
        _ev_mix_kernel,
        grid=(SEQ // tm,),
        in_specs=[
            pl.BlockSpec((tm, 2 * A_WIDTH), lambda i: (i, 0)),
            pl.BlockSpec((tm, B_KEY), lambda i: (i, 2 * A_WIDTH // B_KEY)),
            pl.BlockSpec((tm, B_KEY), lambda i: (i, 2 * A_WIDTH // B_KEY + 1)),
            pl.BlockSpec((tm, B_VAL), lambda i: (i, (2 * A_WIDTH + 2 * B_KEY) // B_VAL)),
            pl.BlockSpec((tm, B_VAL), lambda i: (i, (2 * A_WIDTH + 2 * B_KEY) // B_VAL + 1)),
            pl.BlockSpec((tm, B_KEY), lambda i: (i, 0)),
            pl.BlockSpec((A_HEADS, A_CHUNK, A_CHUNK), lambda i: (0, 0, 0)),
            pl.BlockSpec((A_CHUNK, A_WIDTH), lambda i: (0, 0)),
            pl.BlockSpec((1, B_DV), lambda i: (0, 0)),
        ],
        out_specs=pl.BlockSpec((tm, A_WIDTH + B_VAL), lambda i: (i, 0)),
        out_shape=jax.ShapeDtypeStruct((SEQ, A_WIDTH + B_VAL), BF16),
        scratch_shapes=[
            pltpu.VMEM((tm, A_WIDTH), BF16),
            pltpu.VMEM((B_HEADS, B_DV, B_DK), F32),
        ],
        compiler_params=_params("arbitrary"),
        name="ev_mix",
    )(p, p, p, p, p, gk, w_s, bs_full, gla_norm_g)


def _ev_out_kernel(y_ref, w_ref, g_ref, h_ref, o_ref):
    o_ref[...] = h_ref[...] + _rms(_dot(y_ref[...], w_ref[...].astype(BF16)), g_ref[...])


def _ev_out(y, w, g, h, layer):
    tm = ROW_TILE
    kdim = y.shape[1]
    return pl.pallas_call(
        _ev_out_kernel,
        grid=(SEQ // tm,),
        in_specs=[
            pl.BlockSpec((tm, kdim), lambda i: (i, 0)),
            pl.BlockSpec((None, kdim, D_MODEL), lambda i: (layer, 0, 0), pipeline_mode=pl.Buffered(1)),
            pl.BlockSpec((1, D_MODEL), lambda i: (0, 0)),
            pl.BlockSpec((tm, D_MODEL), lambda i: (i, 0)),
        ],
        out_specs=pl.BlockSpec((tm, D_MODEL), lambda i: (i, 0)),
        out_shape=jax.ShapeDtypeStruct((SEQ, D_MODEL), F32),
        compiler_params=_params("parallel"),
        name="ev_out",
    )(y, w, g, h)


def _od_in_kernel(h_ref, g_ref, w_ref, ut_ref, utb_ref):
    ut = lax.dot_general(w_ref[...].astype(BF16), _rms(h_ref[...], g_ref[...]).astype(BF16),
                         (((0,), (1,)), ((), ())), preferred_element_type=F32)
    ut_ref[...] = ut
    utb_ref[...] = ut.astype(BF16)


def _od_in(h, g, w, layer):
    tm = ROW_TILE
    return pl.pallas_call(
        _od_in_kernel,
        grid=(SEQ // tm,),
        in_specs=[
            pl.BlockSpec((tm, D_MODEL), lambda i: (i, 0)),
            pl.BlockSpec((1, D_MODEL), lambda i: (0, 0)),
            pl.BlockSpec((None, D_MODEL, C_WIDTH), lambda i: (layer, 0, 0), pipeline_mode=pl.Buffered(1)),
        ],
        out_specs=[
            pl.BlockSpec((C_WIDTH, tm), lambda i: (0, i)),
            pl.BlockSpec((C_WIDTH, tm), lambda i: (0, i)),
        ],
        out_shape=[
            jax.ShapeDtypeStruct((C_WIDTH, SEQ), F32),
            jax.ShapeDtypeStruct((C_WIDTH, SEQ), BF16),
        ],
        compiler_params=_params("parallel"),
        name="od_in",
    )(h, g, w)


def _split3(x):
    hi = x.astype(BF16)
    r1 = x - hi.astype(F32)
    mid = r1.astype(BF16)
    lo = (r1 - mid.astype(F32)).astype(BF16)
    return hi, mid, lo


def _dot_f32(a, b):
    a0, a1, a2 = _split3(a)
    b0, b1, b2 = _split3(b)
    return (_dot(a0, b0) + (_dot(a0, b1) + _dot(a1, b0))
            + (_dot(a0, b2) + _dot(a1, b1) + _dot(a2, b0)))


def _lane_expand(x, e):
    x0, x1, x2 = _split3(x)
    return _dot(x0, e) + _dot(x1, e) + _dot(x2, e)


def _s5_group_operators(ar, ai, bbr, bbi, bbt, crt, cit):
    tau = lax.broadcasted_iota(jnp.int32, (C_STATE, LANES), 1) & (S5_CHUNK - 1)
    expand = (lax.broadcasted_iota(jnp.int32, (C_GROUP, S5_COLS), 1) // S5_CHUNK
              == lax.broadcasted_iota(jnp.int32, (C_GROUP, S5_COLS), 0)).astype(BF16)
    one = jnp.ones((C_STATE, LANES), F32)
    zero = jnp.zeros((C_STATE, LANES), F32)
    pr, pi = one, zero
    rr, ri = one, zero
    fr, fi = ar, ai
    for b in range(S5_CHUNK.bit_length() - 1):
        bit = ((tau >> b) & 1) == 1
        pr, pi = jnp.where(bit, pr * fr - pi * fi, pr), jnp.where(bit, pr * fi + pi * fr, pi)
        rr, ri = jnp.where(bit, rr, rr * fr - ri * fi), jnp.where(bit, ri, rr * fi + ri * fr)
        fr, fi = fr * fr - fi * fi, 2.0 * (fr * fi)
    wide = lambda a: jnp.concatenate([a] * (S5_COLS // LANES), axis=1)
    pr, pi, rr, ri, ar, ai = wide(pr), wide(pi), wide(rr), wide(ri), wide(ar), wide(ai)
    cr = _lane_expand(crt, expand)
    ci = _lane_expand(cit, expand)
    cer = cr * pr - ci * pi
    cei = cr * pi + ci * pr
    kv = _dot_f32(bbt, jnp.concatenate([cer, -cei], axis=0))
    qr = cer * ar - cei * ai
    qi = cer * ai + cei * ar
    q = jnp.concatenate([qr, -qi], axis=0).astype(BF16)
    br = _lane_expand(bbr, expand)
    bi = _lane_expand(bbi, expand)
    ptr = br * rr - bi * ri
    pti = br * ri + bi * rr
    pt = jnp.concatenate([ptr, pti], axis=0).astype(BF16)
    pts = jnp.concatenate([pti, ptr], axis=0).astype(BF16)
    return kv, pt, pts, q


def _s5_kernel(ut2_ref, ar_ref, ai_ref, bbr_ref, bbi_ref, bbt_ref, crt_ref, cit_ref,
               a1_ref, a2_ref, a2s_ref, yt2_ref, ut_ref, yt_ref, u_ref, m_ref, q_ref, inc_ref, incs_ref,
               xs_ref, y_ref):
    gb = u_ref.shape[0]
    w = 2 * C_STATE
    ut_ref[...] = ut2_ref[...].reshape(ut_ref.shape)
    per = LANES // S5_CHUNK
    nsup = S5_NC // per
    keep = ((lax.broadcasted_iota(jnp.int32, (S5_CHUNK, LANES), 1) & (S5_CHUNK - 1))
            >= lax.broadcasted_iota(jnp.int32, (S5_CHUNK, LANES), 0))
    for gi in range(gb):
        for cp in range(C_GROUP):
            for k in range(per):
                u_ref[gi, k * nsup:(k + 1) * nsup, cp * S5_CHUNK:(cp + 1) * S5_CHUNK] = (
                    ut_ref[gi * C_GROUP + cp, :, k * S5_CHUNK:(k + 1) * S5_CHUNK])
        kv, pt, pts, q = _s5_group_operators(ar_ref[gi], ai_ref[gi], bbr_ref[gi], bbi_ref[gi], bbt_ref[gi],
                                             crt_ref[gi], cit_ref[gi])
        q_ref[gi] = q
        for cp in range(C_GROUP):
            taps = jnp.broadcast_to(kv[cp:cp + 1, :], (S5_CHUNK, S5_COLS))
            for v in range(S5_COLS // LANES):
                tile = pltpu.roll(taps[:, v * LANES:(v + 1) * LANES], 0, 1, stride=1, stride_axis=0)
                m_ref[gi, cp * S5_CHUNK:(cp + 1) * S5_CHUNK, v * LANES:(v + 1) * LANES] = (
                    jnp.where(keep, tile, 0.0).astype(BF16))
        u = u_ref[gi]
        inc_ref[:, gi * w:(gi + 1) * w] = _dot_nt(u, pt)
        incs_ref[:, gi * w:(gi + 1) * w] = _dot_nt(u, pts)

    a1 = a1_ref[...]
    a2 = a2_ref[...]
    a2s = a2s_ref[...]

    def step(n, carry):
        x, xs = carry
        row = pl.ds((n & (per - 1)) * nsup + (n >> (per.bit_length() - 1)), 1)
        xs_ref[row, :] = x
        x_new = a1 * x + a2 * xs + inc_ref[row, :]
        xs_new = a1 * xs + a2s * x + incs_ref[row, :]
        return x_new, xs_new

    zero = jnp.zeros((1, gb * w), F32)
    lax.fori_loop(0, S5_NC, step, (zero, zero))

    for gi in range(gb):
        xst = xs_ref[:, gi * w:(gi + 1) * w].astype(BF16)
        y_ref[...] = _dot(u_ref[gi], m_ref[gi]) + _dot(xst, q_ref[gi])
        for c in range(C_GROUP):
            for k in range(per):
                yt_ref[gi * C_GROUP + c, :, k * S5_CHUNK:(k + 1) * S5_CHUNK] = (
                    y_ref[k * nsup:(k + 1) * nsup, c * S5_CHUNK:(c + 1) * S5_CHUNK])
    yt2_ref[...] = yt_ref[...].reshape(yt2_ref.shape)


def _s5(u_tb, prm):
    gb = S5_GB
    w = 2 * C_STATE
    grp3 = lambda i: (i, 0, 0)
    return pl.pallas_call(
        _s5_kernel,
        grid=(C_GROUPS // gb,),
        in_specs=[
            pl.BlockSpec((gb * C_GROUP, SEQ), lambda i: (i, 0)),
            pl.BlockSpec((gb, C_STATE, LANES), grp3),
            pl.BlockSpec((gb, C_STATE, LANES), grp3),
            pl.BlockSpec((gb, C_STATE, C_GROUP), grp3),
            pl.BlockSpec((gb, C_STATE, C_GROUP), grp3),
            pl.BlockSpec((gb, C_GROUP, w), grp3),
            pl.BlockSpec((gb, C_STATE, C_GROUP), grp3),
            pl.BlockSpec((gb, C_STATE, C_GROUP), grp3),
            pl.BlockSpec((1, gb * w), lambda i: (0, i)),
            pl.BlockSpec((1, gb * w), lambda i: (0, i)),
            pl.BlockSpec((1, gb * w), lambda i: (0, i)),
        ],
        out_specs=pl.BlockSpec((gb * C_GROUP, SEQ), lambda i: (i, 0)),
        out_shape=jax.ShapeDtypeStruct((C_WIDTH, SEQ), F32),
        scratch_shapes=[
            pltpu.VMEM((gb * C_GROUP, SEQ // LANES, LANES), BF16),
            pltpu.VMEM((gb * C_GROUP, SEQ // LANES, LANES), F32),
            pltpu.VMEM((gb, S5_NC, S5_COLS), BF16),
            pltpu.VMEM((gb, S5_COLS, S5_COLS), BF16),
            pltpu.VMEM((gb, w, S5_COLS), BF16),
            pltpu.VMEM((S5_NC, gb * w), F32),
            pltpu.VMEM((S5_NC, gb * w), F32),
            pltpu.VMEM((S5_NC, gb * w), F32),
            pltpu.VMEM((S5_NC, S5_COLS), F32),
        ],
        compiler_params=_params("parallel"),
        name="s5",
    )(u_tb, *prm)


def _s5_params(lam_re, lam_im, log_dt, b_re, b_im, c_re, c_im):
    lr = jnp.minimum(lam_re, -1e-4)
    li = lam_im
    dt = jnp.exp(log_dt)[:, None]
    mag = jnp.exp(lr * dt)
    ar = mag * jnp.cos(li * dt)
    ai = mag * jnp.sin(li * dt)
    den = lr * lr + li * li
    nr = ar - 1.0
    cr = (nr * lr + ai * li) / den
    ci = (ai * lr - nr * li) / den
    bbr = cr[..., None] * b_re - ci[..., None] * b_im
    bbi = cr[..., None] * b_im + ci[..., None] * b_re
    bbt = jnp.concatenate([bbr, bbi], axis=1).transpose(0, 2, 1)
    atr, ati = ar, ai
    for _ in range(S5_CHUNK.bit_length() - 1):
        atr, ati = atr * atr - ati * ati, 2.0 * (atr * ati)
    a1 = jnp.concatenate([atr, atr], axis=-1).reshape(1, -1)
    a2 = jnp.concatenate([-ati, ati], axis=-1).reshape(1, -1)
    a2s = jnp.concatenate([ati, -ati], axis=-1).reshape(1, -1)
    bcast = lambda a: jnp.broadcast_to(a[..., None], a.shape + (LANES,))
    return (bcast(ar), bcast(ai), bbr, bbi, bbt, c_re.transpose(0, 2, 1), c_im.transpose(0, 2, 1),
            a1, a2, a2s)


def _od_out_kernel(yt_ref, ut_ref, d_ref, wglu_ref, bglu_ref, wout_ref, g_ref, h_ref, o_ref):
    y = yt_ref[...] + d_ref[...] * ut_ref[...]
    z = jax.nn.gelu(y)
    gate = jax.nn.sigmoid(_dot_tn(wglu_ref[...].astype(BF16), z.astype(BF16)) + bglu_ref[...])
    m = _dot_tn((z * gate).astype(BF16), wout_ref[...].astype(BF16))
    o_ref[...] = h_ref[...] + _rms(m, g_ref[...])


def _od_out(y_t, u_t, d, w_glu, b_glu, w_out, g, h, layer):
    tm = ROW_TILE
    return pl.pallas_call(
        _od_out_kernel,
        grid=(SEQ // tm,),
        in_specs=[
            pl.BlockSpec((C_WIDTH, tm), lambda i: (0, i)),
            pl.BlockSpec((C_WIDTH, tm), lambda i: (0, i)),
            pl.BlockSpec((C_WIDTH, 1), lambda i: (0, 0)),
            pl.BlockSpec((None, C_WIDTH, C_WIDTH), lambda i: (layer, 0, 0), pipeline_mode=pl.Buffered(1)),
            pl.BlockSpec((C_WIDTH, 1), lambda i: (0, 0)),
            pl.BlockSpec((None, C_WIDTH, D_MODEL), lambda i: (layer, 0, 0), pipeline_mode=pl.Buffered(1)),
            pl.BlockSpec((1, D_MODEL), lambda i: (0, 0)),
            pl.BlockSpec((tm, D_MODEL), lambda i: (i, 0)),
        ],
        out_specs=pl.BlockSpec((tm, D_MODEL), lambda i: (i, 0)),
        out_shape=jax.ShapeDtypeStruct((SEQ, D_MODEL), F32),
        compiler_params=_params("parallel"),
        name="od_out",
    )(y_t, u_t, d, w_glu, b_glu, w_out, g, h)


def _row(v):
    return v.reshape(1, -1).astype(F32)


def kernel(x, norm_g, ffn_w_gate, ffn_w_up, ffn_w_down, ev_w_in, ev_ln_g, ev_ln_b, ev_w_s, ev_b_s, ev_w_gate2, ev_b_gate, ev_gla_norm_g, ev_w_out, od_w_in, od_lam_re, od_lam_im, od_log_dt, od_b_re, od_b_im, od_c_re, od_c_im, od_d, od_w_glu, od_b_glu, od_w_out):
    depth = norm_g.shape[0]
    h = x.reshape(SEQ, D_MODEL)
    for l in range(depth):
        i = l // 2
        g = norm_g[l]
        h = _ffn(h, _row(g[0]), _row(g[1]), ffn_w_gate, ffn_w_up, ffn_w_down, l, 0)
        if l % 2 == 0:
            w_g2 = jnp.pad(ev_w_gate2[i], ((0, LANES - B_GATE_RANK), (0, 0))).astype(BF16)
            p, gk = _ev_in(h, _row(g[2]), jnp.swapaxes(ev_w_in, 1, 2), w_g2, _row(ev_b_gate[i]),
                           _row(ev_ln_g[i]), _row(ev_ln_b[i]), i)
            bs_full = jnp.repeat(ev_b_s[i].T, A_HEAD_DIM, axis=1).astype(F32)
            y = _ev_mix(p, gk, ev_w_s[i], bs_full, _row(ev_gla_norm_g[i]))
            h = _ev_out(y, ev_w_out, _row(g[3]), h, i)
        else:
            u_t, u_tb = _od_in(h, _row(g[2]), od_w_in, i)
            prm = _s5_params(od_lam_re[i], od_lam_im[i], od_log_dt[i], od_b_re[i], od_b_im[i],
                             od_c_re[i], od_c_im[i])
            y_t = _s5(u_tb, prm)
            h = _od_out(y_t, u_t, od_d[i].reshape(C_WIDTH, 1), od_w_glu, od_b_glu[i].reshape(C_WIDTH, 1),
                        od_w_out, _row(g[3]), h, i)
        h = _ffn(h, _row(g[4]), _row(g[5]), ffn_w_gate, ffn_w_up, ffn_w_down, l, 1)
    return h.reshape(x.shape)
```

```python
import jax
import jax.numpy as jnp
from jax import lax
from jax.experimental import pallas as pl
from jax.experimental.pallas import tpu as pltpu

F32 = jnp.float32
BF16 = jnp.bfloat16

D_MODEL = 2048
SEQ = 8192
D_FF = 5632
EPS = 1e-6

A_HEADS = 8
A_HEAD_DIM = 128
A_WIDTH = 1024
A_CHUNK = 128
B_HEADS = 4
B_DK = 128
B_DV = 256
B_KEY = 512
B_VAL = 1024
B_GATE_RANK = 16
B_GATE_TAU = 16.0
B_CHUNK = 64
EVEN_MAIN = 2 * A_WIDTH + 2 * B_KEY + 2 * B_VAL
C_WIDTH = 1024
C_GROUP = 16
C_GROUPS = 64
C_STATE = 64

LANES = 128
VMEM_LIMIT = 56 * 1024 * 1024
FFN_VMEM_LIMIT = 60 * 1024 * 1024

ROW_TILE = 512
EV_IN_ROWS = 1024
FFN_ROWS = 1024
FFN_SLABS = 8
FFN_EPI_ROWS = 8
FF_TILE = 512
S5_CHUNK = 32
S5_NC = SEQ // S5_CHUNK
S5_COLS = C_GROUP * S5_CHUNK
S5_GB = 8


def _rms(x, g):
    return x * lax.rsqrt(jnp.mean(x * x, axis=-1, keepdims=True) + EPS) * g


def _dot(a, b):
    return jnp.dot(a, b, preferred_element_type=F32)


def _dot_nt(a, b):
    return lax.dot_general(a, b, (((1,), (1,)), ((), ())), preferred_element_type=F32)


def _dot_tn(a, b):
    return lax.dot_general(a, b, (((0,), (0,)), ((), ())), preferred_element_type=F32)


def _params(*sem, vmem=VMEM_LIMIT):
    return pltpu.CompilerParams(dimension_semantics=sem, vmem_limit_bytes=vmem)


def _ffn_kernel(h_hbm, gpre_ref, gpost_ref, wg_ref, wu_ref, wd_ref, o_hbm,
                h_ref, acc_ref, xn_ref, ms_ref, h_sem, o_sem):
    i, j = pl.program_id(0), pl.program_id(1)
    n_i, n_j = pl.num_programs(0), pl.num_programs(1)
    tm = h_ref.shape[0]
    slab = tm // FFN_SLABS

    def h_copy(tile, r):
        return pltpu.make_async_copy(h_hbm.at[pl.ds(tile * tm + r * slab, slab), :],
                                     h_ref.at[pl.ds(r * slab, slab), :], h_sem.at[r])

    def o_copy(tile, r):
        return pltpu.make_async_copy(acc_ref.at[pl.ds(r * slab, slab), :],
                                     o_hbm.at[pl.ds(tile * tm + r * slab, slab), :], o_sem.at[r])

    @pl.when(j == 0)
    def _():
        @pl.when(i == 0)
        def _():
            for r in range(FFN_SLABS):
                h_copy(0, r).start()

        for r in range(FFN_SLABS):
            rows = pl.ds(r * slab, slab)
            h_copy(i, r).wait()
            xn_ref[rows, :] = _rms(h_ref[rows, :], gpre_ref[...]).astype(BF16)

            @pl.when(i > 0)
            def _():
                o_copy(i - 1, r).wait()
            acc_ref[rows, :] = jnp.zeros((slab, acc_ref.shape[1]), F32)

    def chunk_sum():
        xn = xn_ref[...]
        hf = wg_ref.shape[1] // 2
        gate_a = _dot(xn, wg_ref[:, :hf].astype(BF16))
        up_a = _dot(xn, wu_ref[:, :hf].astype(BF16))
        gate_b = _dot(xn, wg_ref[:, hf:].astype(BF16))
        up_b = _dot(xn, wu_ref[:, hf:].astype(BF16))
        act_a = (jax.nn.silu(gate_a) * up_a).astype(BF16)
        act_b = (jax.nn.silu(gate_b) * up_b).astype(BF16)
        return acc_ref[...] + (_dot(act_a, wd_ref[:hf, :].astype(BF16)) + _dot(act_b, wd_ref[hf:, :].astype(BF16)))

    @pl.when(j < n_j - 1)
    def _():
        acc_ref[...] = chunk_sum()

    @pl.when(j == n_j - 1)
    def _():
        f = chunk_sum()
        acc_ref[...] = f
        ms_ref[...] = jnp.mean(f * f, axis=-1, keepdims=True)

    @pl.when(j == n_j - 1)
    def _():
        for r in range(FFN_SLABS):
            for q in range(slab // FFN_EPI_ROWS):
                rows = pl.ds(r * slab + q * FFN_EPI_ROWS, FFN_EPI_ROWS)
                scale = 0.5 * lax.rsqrt(ms_ref[rows, :] + EPS)
                acc_ref[rows, :] = h_ref[rows, :] + acc_ref[rows, :] * scale * gpost_ref[...]
            o_copy(i, r).start()

            @pl.when(i + 1 < n_i)
            def _():
                h_copy(i + 1, r).start()

        @pl.when(i == n_i - 1)
        def _():
            for r in range(FFN_SLABS):
                o_copy(i, r).wait()


def _ffn(h, g_pre, g_post, wg, wu, wd, layer, half):
    tm, tf = FFN_ROWS, FF_TILE
    return pl.pallas_call(
        _ffn_kernel,
        grid=(SEQ // tm, D_FF // tf),
        in_specs=[
            pl.BlockSpec(memory_space=pl.ANY),
            pl.BlockSpec((1, D_MODEL), lambda i, j: (0, 0)),
            pl.BlockSpec((1, D_MODEL), lambda i, j: (0, 0)),
            pl.BlockSpec((None, None, D_MODEL, tf), lambda i, j: (layer, half, 0, j)),
            pl.BlockSpec((None, None, D_MODEL, tf), lambda i, j: (layer, half, 0, j)),
            pl.BlockSpec((None, None, tf, D_MODEL), lambda i, j: (layer, half, j, 0)),
        ],
        out_specs=pl.BlockSpec(memory_space=pl.ANY),
        out_shape=jax.ShapeDtypeStruct((SEQ, D_MODEL), F32),
        scratch_shapes=[
            pltpu.VMEM((tm, D_MODEL), F32),
            pltpu.VMEM((tm, D_MODEL), F32),
            pltpu.VMEM((tm, D_MODEL), BF16),
            pltpu.VMEM((tm, 1), F32),
            pltpu.SemaphoreType.DMA((FFN_SLABS,)),
            pltpu.SemaphoreType.DMA((FFN_SLABS,)),
        ],
        compiler_params=_params("arbitrary", "arbitrary", vmem=FFN_VMEM_LIMIT),
        name="ffn",
    )(h, g_pre, g_post, wg, wu, wd)


def _ev_in_kernel(h_ref, g_ref, w_ref, wlr_ref, wg2_ref, bg_ref, lng_ref, lnb_ref, p_ref, pb_ref, gk_ref, xn_ref):
    j = pl.program_id(1)

    @pl.when(j == 0)
    def _():
        xn = _rms(h_ref[...], g_ref[...]).astype(BF16)
        xn_ref[...] = xn
        lane = lax.broadcasted_iota(jnp.int32, (xn.shape[0], LANES), 1)
        glr = jnp.where(lane < B_GATE_RANK, _dot_nt(xn, wlr_ref[...].astype(BF16)), 0.0).astype(BF16)
        pre = _dot(glr, wg2_ref[...]) + bg_ref[...]
        gk_ref[...] = jax.nn.log_sigmoid(pre) * (1.0 / B_GATE_TAU)

    def project():
        return _dot_nt(xn_ref[...], w_ref[...].astype(BF16))

    @pl.when(j == 0)
    def _():
        p_ref[...] = jax.nn.gelu(project())

    @pl.when(j == 1)
    def _():
        va = jax.nn.gelu(project())
        vc = va - jnp.mean(va, axis=-1, keepdims=True)
        vln = vc * lax.rsqrt(jnp.mean(vc * vc, axis=-1, keepdims=True) + EPS)
        pb_ref[...] = (vln * lng_ref[...] + lnb_ref[...]).astype(BF16)

    @pl.when(j == 2)
    def _():
        p_ref[...] = project()

    @pl.when(j == 3)
    def _():
        pb_ref[...] = project().astype(BF16)

    @pl.when(j == 4)
    def _():
        p_ref[...] = jax.nn.silu(project())


def _ev_in(h, g, w_in_t, w_g2, b_gate, ln_g, ln_b, layer):
    tm, tn = EV_IN_ROWS, A_WIDTH
    assert tn == 2 * B_KEY == B_VAL, "the kernel's per-block activations assume blocks u | v | q,k | v | r"
    return pl.pallas_call(
        _ev_in_kernel,
        grid=(SEQ // tm, EVEN_MAIN // tn),
        in_specs=[
            pl.BlockSpec((tm, D_MODEL), lambda i, j: (i, 0)),
            pl.BlockSpec((1, D_MODEL), lambda i, j: (0, 0)),
            pl.BlockSpec((None, tn, D_MODEL), lambda i, j: (layer, j, 0)),
            pl.BlockSpec((None, LANES, D_MODEL), lambda i, j: (layer, EVEN_MAIN // LANES, 0)),
            pl.BlockSpec((LANES, B_KEY), lambda i, j: (0, 0)),
            pl.BlockSpec((1, B_KEY), lambda i, j: (0, 0)),
            pl.BlockSpec((1, A_WIDTH), lambda i, j: (0, 0)),
            pl.BlockSpec((1, A_WIDTH), lambda i, j: (0, 0)),
        ],
        out_specs=[
            pl.BlockSpec((tm, tn), lambda i, j: (i, j // 2)),
            pl.BlockSpec((tm, tn), lambda i, j: (i, jnp.maximum(j - 1, 0) // 2)),
            pl.BlockSpec((tm, B_KEY), lambda i, j: (i, 0)),
        ],
        out_shape=[
            jax.ShapeDtypeStruct((SEQ, 3 * tn), F32),
            jax.ShapeDtypeStruct((SEQ, 2 * tn), BF16),
            jax.ShapeDtypeStruct((SEQ, B_KEY), F32),
        ],
        scratch_shapes=[pltpu.VMEM((tm, D_MODEL), BF16)],
        compiler_params=_params("parallel", "arbitrary", vmem=FFN_VMEM_LIMIT),
        name="ev_in",
    )(h, g, w_in_t, w_in_t, w_g2, b_gate, ln_g, ln_b)


def _ev_mix_kernel(u_ref, vln_ref, q_ref, k_ref, v_ref, r_ref, gk_ref, ws_ref, bs_ref,
                   gng_ref, y_ref, st_ref):
    tm = u_ref.shape[0]

    @pl.when(pl.program_id(0) == 0)
    def _():
        st_ref[...] = jnp.zeros_like(st_ref)

    n_a = tm // A_CHUNK
    row = lax.broadcasted_iota(jnp.int32, (A_CHUNK, A_CHUNK), 0)
    col = lax.broadcasted_iota(jnp.int32, (A_CHUNK, A_CHUNK), 1)
    for hd in range(A_HEADS):
        cs = slice(hd * A_HEAD_DIM, (hd + 1) * A_HEAD_DIM)
        w = jnp.where(row >= col, ws_ref[hd], 0.0).astype(BF16)
        rhs = jnp.concatenate(
            [vln_ref[c * A_CHUNK:(c + 1) * A_CHUNK, cs] for c in range(n_a)], axis=1)
        s = _dot(w, rhs)
        for c in range(n_a):
            rs = slice(c * A_CHUNK, (c + 1) * A_CHUNK)
            sc = s[:, c * A_HEAD_DIM:(c + 1) * A_HEAD_DIM] + bs_ref[:, cs]
            y_ref[rs, cs] = (u_ref[rs, cs] * sc).astype(BF16)

    n_b = tm // B_CHUNK
    r64 = lax.broadcasted_iota(jnp.int32, (B_CHUNK, B_CHUNK), 0)
    c64 = lax.broadcasted_iota(jnp.int32, (B_CHUNK, B_CHUNK), 1)
    causal = r64 >= c64
    tri = jnp.where(causal, 1.0, 0.0).astype(BF16)
    scale = B_DK ** -0.5
    for c in range(n_b):
        rs = slice(c * B_CHUNK, (c + 1) * B_CHUNK)
        g = gk_ref[rs, :]
        g_hi = g.astype(BF16)
        g_lo = (g - g_hi.astype(F32)).astype(BF16)
        bcum = _dot(tri, g_hi) + _dot(tri, g_lo)
        b_last = bcum[B_CHUNK - 1:B_CHUNK, :]
        q = q_ref[rs, :] * scale
        k = k_ref[rs, :]
        q_dec = (q * jnp.exp(bcum)).astype(BF16)
        k_inv = (k * jnp.exp(-bcum)).astype(BF16)
        k_end = (k * jnp.exp(b_last - bcum)).astype(BF16)
        decay = jnp.exp(b_last)
        for hd in range(B_HEADS):
            ks = slice(hd * B_DK, (hd + 1) * B_DK)
            vs = slice(hd * B_DV, (hd + 1) * B_DV)
            v = v_ref[rs, vs]
            scores = jnp.where(causal, _dot_nt(q_dec[:, ks], k_inv[:, ks]), 0.0).astype(BF16)
            st = st_ref[hd]
            o = _dot(scores, v) + _dot_nt(q_dec[:, ks], st.astype(BF16))
            st_ref[hd] = decay[:, ks] * st + _dot_tn(v, k_end[:, ks])
            o = o * lax.rsqrt(jnp.mean(o * o, axis=-1, keepdims=True) + EPS) * gng_ref[...]
            y_ref[rs, A_WIDTH + hd * B_DV:A_WIDTH + (hd + 1) * B_DV] = (
                o * r_ref[rs, vs]).astype(BF16)


def _ev_mix(p, pb, gk, w_s, bs_full, gla_norm_g):
    tm = ROW_TILE
    return pl.pallas_call(
        _ev_mix_kernel,
        grid=(SEQ // tm,),
        in_specs=[
            pl.BlockSpec((tm, A_WIDTH), lambda i: (i, 0)),
            pl.BlockSpec((tm, A_WIDTH), lambda i: (i, 0)),
            pl.BlockSpec((tm, B_KEY), lambda i: (i, A_WIDTH // B_KEY)),
            pl.BlockSpec((tm, B_KEY), lambda i: (i, A_WIDTH // B_KEY + 1)),
            pl.BlockSpec((tm, B_VAL), lambda i: (i, 1)),
            pl.BlockSpec((tm, B_VAL), lambda i: (i, 2)),
            pl.BlockSpec((tm, B_KEY), lambda i: (i, 0)),
            pl.BlockSpec((A_HEADS, A_CHUNK, A_CHUNK), lambda i: (0, 0, 0)),
            pl.BlockSpec((A_CHUNK, A_WIDTH), lambda i: (0, 0)),
            pl.BlockSpec((1, B_DV), lambda i: (0, 0)),
        ],
        out_specs=pl.BlockSpec((tm, A_WIDTH + B_VAL), lambda i: (i, 0)),
        out_shape=jax.ShapeDtypeStruct((SEQ, A_WIDTH + B_VAL), BF16),
        scratch_shapes=[pltpu.VMEM((B_HEADS, B_DV, B_DK), F32)],
        compiler_params=_params("arbitrary"),
        name="ev_mix",
    )(p, pb, p, p, pb, p, gk, w_s, bs_full, gla_norm_g)


def _ev_out_kernel(y_ref, w_ref, g_ref, h_ref, o_ref):
    o_ref[...] = h_ref[...] + _rms(_dot(y_ref[...], w_ref[...].astype(BF16)), g_ref[...])


def _ev_out(y, w, g, h, layer):
    tm = ROW_TILE
    kdim = y.shape[1]
    return pl.pallas_call(
        _ev_out_kernel,
        grid=(SEQ // tm,),
        in_specs=[
            pl.BlockSpec((tm, kdim), lambda i: (i, 0)),
            pl.BlockSpec((None, kdim, D_MODEL), lambda i: (layer, 0, 0), pipeline_mode=pl.Buffered(1)),
            pl.BlockSpec((1, D_MODEL), lambda i: (0, 0)),
            pl.BlockSpec((tm, D_MODEL), lambda i: (i, 0)),
        ],
        out_specs=pl.BlockSpec((tm, D_MODEL), lambda i: (i, 0)),
        out_shape=jax.ShapeDtypeStruct((SEQ, D_MODEL), F32),
        compiler_params=_params("parallel"),
        name="ev_out",
    )(y, w, g, h)


def _od_in_kernel(h_ref, g_ref, w_ref, ut_ref, utb_ref):
    ut = lax.dot_general(w_ref[...].astype(BF16), _rms(h_ref[...], g_ref[...]).astype(BF16),
                         (((0,), (1,)), ((), ())), preferred_element_type=F32)
    ut_ref[...] = ut
    utb_ref[...] = ut.astype(BF16)


def _od_in(h, g, w, layer):
    tm = ROW_TILE
    return pl.pallas_call(
        _od_in_kernel,
        grid=(SEQ // tm,),
        in_specs=[
            pl.BlockSpec((tm, D_MODEL), lambda i: (i, 0)),
            pl.BlockSpec((1, D_MODEL), lambda i: (0, 0)),
            pl.BlockSpec((None, D_MODEL, C_WIDTH), lambda i: (layer, 0, 0), pipeline_mode=pl.Buffered(1)),
        ],
        out_specs=[
            pl.BlockSpec((C_WIDTH, tm), lambda i: (0, i)),
            pl.BlockSpec((C_WIDTH, tm), lambda i: (0, i)),
        ],
        out_shape=[
            jax.ShapeDtypeStruct((C_WIDTH, SEQ), F32),
            jax.ShapeDtypeStruct((C_WIDTH, SEQ), BF16),
        ],
        compiler_params=_params("parallel"),
        name="od_in",
    )(h, g, w)


def _split3(x):
    hi = x.astype(BF16)
    r1 = x - hi.astype(F32)
    mid = r1.astype(BF16)
    lo = (r1 - mid.astype(F32)).astype(BF16)
    return hi, mid, lo


def _dot_f32(a, b):
    a0, a1, a2 = _split3(a)
    b0, b1, b2 = _split3(b)
    return (_dot(a0, b0) + (_dot(a0, b1) + _dot(a1, b0))
            + (_dot(a0, b2) + _dot(a1, b1) + _dot(a2, b0)))


def _lane_expand(x, e):
    x0, x1, x2 = _split3(x)
    return _dot(x0, e) + _dot(x1, e) + _dot(x2, e)


def _s5_group_operators(ar, ai, bbr, bbi, bbt, crt, cit):
    tau = lax.broadcasted_iota(jnp.int32, (C_STATE, LANES), 1) & (S5_CHUNK - 1)
    expand = (lax.broadcasted_iota(jnp.int32, (C_GROUP, S5_COLS), 1) // S5_CHUNK
              == lax.broadcasted_iota(jnp.int32, (C_GROUP, S5_COLS), 0)).astype(BF16)
    one = jnp.ones((C_STATE, LANES), F32)
    zero = jnp.zeros((C_STATE, LANES), F32)
    pr, pi = one, zero
    rr, ri = one, zero
    fr, fi = ar, ai
    for b in range(S5_CHUNK.bit_length() - 1):
        bit = ((tau >> b) & 1) == 1
        pr, pi = jnp.where(bit, pr * fr - pi * fi, pr), jnp.where(bit, pr * fi + pi * fr, pi)
        rr, ri = jnp.where(bit, rr, rr * fr - ri * fi), jnp.where(bit, ri, rr * fi + ri * fr)
        fr, fi = fr * fr - fi * fi, 2.0 * (fr * fi)
    wide = lambda a: jnp.concatenate([a] * (S5_COLS // LANES), axis=1)
    pr, pi, rr, ri, ar, ai = wide(pr), wide(pi), wide(rr), wide(ri), wide(ar), wide(ai)
    cr = _lane_expand(crt, expand)
    ci = _lane_expand(cit, expand)
    cer = cr * pr - ci * pi
    cei = cr * pi + ci * pr
    kv = _dot_f32(bbt, jnp.concatenate([cer, -cei], axis=0))
    qr = cer * ar - cei * ai
    qi = cer * ai + cei * ar
    q = jnp.concatenate([qr, -qi], axis=0).astype(BF16)
    br = _lane_expand(bbr, expand)
    bi = _lane_expand(bbi, expand)
    ptr = br * rr - bi * ri
    pti = br * ri + bi * rr
    pt = jnp.concatenate([ptr, pti], axis=0).astype(BF16)
    pts = jnp.concatenate([pti, ptr], axis=0).astype(BF16)
    return kv, pt, pts, q


def _s5_kernel(ut2_ref, ar_ref, ai_ref, bbr_ref, bbi_ref, bbt_ref, crt_ref, cit_ref,
               a1_ref, a2_ref, a2s_ref, yt2_ref, ut_ref, yt_ref, u_ref, m_ref, q_ref, inc_ref, incs_ref,
               xs_ref, y_ref):
    gb = u_ref.shape[0]
    w = 2 * C_STATE
    ut_ref[...] = ut2_ref[...].reshape(ut_ref.shape)
    per = LANES // S5_CHUNK
    nsup = S5_NC // per
    keep = ((lax.broadcasted_iota(jnp.int32, (S5_CHUNK, LANES), 1) & (S5_CHUNK - 1))
            >= lax.broadcasted_iota(jnp.int32, (S5_CHUNK, LANES), 0))
    for gi in range(gb):
        for cp in range(C_GROUP):
            for k in range(per):
                u_ref[gi, k * nsup:(k + 1) * nsup, cp * S5_CHUNK:(cp + 1) * S5_CHUNK] = (
                    ut_ref[gi * C_GROUP + cp, :, k * S5_CHUNK:(k + 1) * S5_CHUNK])
        kv, pt, pts, q = _s5_group_operators(ar_ref[gi], ai_ref[gi], bbr_ref[gi], bbi_ref[gi], bbt_ref[gi],
                                             crt_ref[gi], cit_ref[gi])
        q_ref[gi] = q
        for cp in range(C_GROUP):
            taps = jnp.broadcast_to(kv[cp:cp + 1, :], (S5_CHUNK, S5_COLS))
            for v in range(S5_COLS // LANES):
                tile = pltpu.roll(taps[:, v * LANES:(v + 1) * LANES], 0, 1, stride=1, stride_axis=0)
                m_ref[gi, cp * S5_CHUNK:(cp + 1) * S5_CHUNK, v * LANES:(v + 1) * LANES] = (
                    jnp.where(keep, tile, 0.0).astype(BF16))
        u = u_ref[gi]
        inc_ref[:, gi * w:(gi + 1) * w] = _dot_nt(u, pt)
        incs_ref[:, gi * w:(gi + 1) * w] = _dot_nt(u, pts)

    a1 = a1_ref[...]
    a2 = a2_ref[...]
    a2s = a2s_ref[...]

    def step(n, carry):
        x, xs = carry
        row = pl.ds((n & (per - 1)) * nsup + (n >> (per.bit_length() - 1)), 1)
        xs_ref[row, :] = x
        x_new = a1 * x + a2 * xs + inc_ref[row, :]
        xs_new = a1 * xs + a2s * x + incs_ref[row, :]
        return x_new, xs_new

    zero = jnp.zeros((1, gb * w), F32)
    lax.fori_loop(0, S5_NC, step, (zero, zero))

    for gi in range(gb):
        xst = xs_ref[:, gi * w:(gi + 1) * w].astype(BF16)
        y_ref[...] = _dot(u_ref[gi], m_ref[gi]) + _dot(xst, q_ref[gi])
        for c in range(C_GROUP):
            for k in range(per):
                yt_ref[gi * C_GROUP + c, :, k * S5_CHUNK:(k + 1) * S5_CHUNK] = (
                    y_ref[k * nsup:(k + 1) * nsup, c * S5_CHUNK:(c + 1) * S5_CHUNK])
    yt2_ref[...] = yt_ref[...].reshape(yt2_ref.shape)


def _s5(u_tb, prm):
    gb = S5_GB
    w = 2 * C_STATE
    grp3 = lambda i: (i, 0, 0)
    return pl.pallas_call(
        _s5_kernel,
        grid=(C_GROUPS // gb,),
        in_specs=[
            pl.BlockSpec((gb * C_GROUP, SEQ), lambda i: (i, 0)),
            pl.BlockSpec((gb, C_STATE, LANES), grp3),
            pl.BlockSpec((gb, C_STATE, LANES), grp3),
            pl.BlockSpec((gb, C_STATE, C_GROUP), grp3),
            pl.BlockSpec((gb, C_STATE, C_GROUP), grp3),
            pl.BlockSpec((gb, C_GROUP, w), grp3),
            pl.BlockSpec((gb, C_STATE, C_GROUP), grp3),
            pl.BlockSpec((gb, C_STATE, C_GROUP), grp3),
            pl.BlockSpec((1, gb * w), lambda i: (0, i)),
            pl.BlockSpec((1, gb * w), lambda i: (0, i)),
            pl.BlockSpec((1, gb * w), lambda i: (0, i)),
        ],
        out_specs=pl.BlockSpec((gb * C_GROUP, SEQ), lambda i: (i, 0)),
        out_shape=jax.ShapeDtypeStruct((C_WIDTH, SEQ), F32),
        scratch_shapes=[
            pltpu.VMEM((gb * C_GROUP, SEQ // LANES, LANES), BF16),
            pltpu.VMEM((gb * C_GROUP, SEQ // LANES, LANES), F32),
            pltpu.VMEM((gb, S5_NC, S5_COLS), BF16),
            pltpu.VMEM((gb, S5_COLS, S5_COLS), BF16),
            pltpu.VMEM((gb, w, S5_COLS), BF16),
            pltpu.VMEM((S5_NC, gb * w), F32),
            pltpu.VMEM((S5_NC, gb * w), F32),
            pltpu.VMEM((S5_NC, gb * w), F32),
            pltpu.VMEM((S5_NC, S5_COLS), F32),
        ],
        compiler_params=_params("parallel"),
        name="s5",
    )(u_tb, *prm)


def _s5_params(lam_re, lam_im, log_dt, b_re, b_im, c_re, c_im):
    lr = jnp.minimum(lam_re, -1e-4)
    li = lam_im
    dt = jnp.exp(log_dt)[:, None]
    mag = jnp.exp(lr * dt)
    ar = mag * jnp.cos(li * dt)
    ai = mag * jnp.sin(li * dt)
    den = lr * lr + li * li
    nr = ar - 1.0
    cr = (nr * lr + ai * li) / den
    ci = (ai * lr - nr * li) / den
    bbr = cr[..., None] * b_re - ci[..., None] * b_im
    bbi = cr[..., None] * b_im + ci[..., None] * b_re
    bbt = jnp.concatenate([bbr, bbi], axis=1).transpose(0, 2, 1)
    atr, ati = ar, ai
    for _ in range(S5_CHUNK.bit_length() - 1):
        atr, ati = atr * atr - ati * ati, 2.0 * (atr * ati)
    a1 = jnp.concatenate([atr, atr], axis=-1).reshape(1, -1)
    a2 = jnp.concatenate([-ati, ati], axis=-1).reshape(1, -1)
    a2s = jnp.concatenate([ati, -ati], axis=-1).reshape(1, -1)
    bcast = lambda a: jnp.broadcast_to(a[..., None], a.shape + (LANES,))
    return (bcast(ar), bcast(ai), bbr, bbi, bbt, c_re.transpose(0, 2, 1), c_im.transpose(0, 2, 1),
            a1, a2, a2s)


def _od_out_kernel(yt_ref, ut_ref, d_ref, wglu_ref, bglu_ref, wout_ref, g_ref, h_ref, o_ref):
    y = yt_ref[...] + d_ref[...] * ut_ref[...]
    z = jax.nn.gelu(y)
    gate = jax.nn.sigmoid(_dot_tn(wglu_ref[...].astype(BF16), z.astype(BF16)) + bglu_ref[...])
    m = _dot_tn((z * gate).astype(BF16), wout_ref[...].astype(BF16))
    o_ref[...] = h_ref[...] + _rms(m, g_ref[...])


def _od_out(y_t, u_t, d, w_glu, b_glu, w_out, g, h, layer):
    tm = ROW_TILE
    return pl.pallas_call(
        _od_out_kernel,
        grid=(SEQ // tm,),
        in_specs=[
            pl.BlockSpec((C_WIDTH, tm), lambda i: (0, i)),
            pl.BlockSpec((C_WIDTH, tm), lambda i: (0, i)),
            pl.BlockSpec((C_WIDTH, 1), lambda i: (0, 0)),
            pl.BlockSpec((None, C_WIDTH, C_WIDTH), lambda i: (layer, 0, 0), pipeline_mode=pl.Buffered(1)),
            pl.BlockSpec((C_WIDTH, 1), lambda i: (0, 0)),
            pl.BlockSpec((None, C_WIDTH, D_MODEL), lambda i: (layer, 0, 0), pipeline_mode=pl.Buffered(1)),
            pl.BlockSpec((1, D_MODEL), lambda i: (0, 0)),
            pl.BlockSpec((tm, D_MODEL), lambda i: (i, 0)),
        ],
        out_specs=pl.BlockSpec((tm, D_MODEL), lambda i: (i, 0)),
        out_shape=jax.ShapeDtypeStruct((SEQ, D_MODEL), F32),
        compiler_params=_params("parallel"),
        name="od_out",
    )(y_t, u_t, d, w_glu, b_glu, w_out, g, h)


def _row(v):
    return v.reshape(1, -1).astype(F32)


def kernel(x, norm_g, ffn_w_gate, ffn_w_up, ffn_w_down, ev_w_in, ev_ln_g, ev_ln_b, ev_w_s, ev_b_s, ev_w_gate2, ev_b_gate, ev_gla_norm_g, ev_w_out, od_w_in, od_lam_re, od_lam_im, od_log_dt, od_b_re, od_b_im, od_c_re, od_c_im, od_d, od_w_glu, od_b_glu, od_w_out):
    depth = norm_g.shape[0]
    h = x.reshape(SEQ, D_MODEL)
    for l in range(depth):
        i = l // 2
        g = norm_g[l]
        h = _ffn(h, _row(g[0]), _row(g[1]), ffn_w_gate, ffn_w_up, ffn_w_down, l, 0)
        if l % 2 == 0:
            w_g2 = jnp.pad(ev_w_gate2[i], ((0, LANES - B_GATE_RANK), (0, 0))).astype(BF16)
            p, pb, gk = _ev_in(h, _row(g[2]), jnp.swapaxes(ev_w_in, 1, 2), w_g2, _row(ev_b_gate[i]),
                               _row(ev_ln_g[i]), _row(ev_ln_b[i]), i)
            bs_full = jnp.repeat(ev_b_s[i].T, A_HEAD_DIM, axis=1).astype(F32)
            y = _ev_mix(p, pb, gk, ev_w_s[i], bs_full, _row(ev_gla_norm_g[i]))
            h = _ev_out(y, ev_w_out, _row(g[3]), h, i)
        else:
            u_t, u_tb = _od_in(h, _row(g[2]), od_w_in, i)
            prm = _s5_params(od_lam_re[i], od_lam_im[i], od_log_dt[i], od_b_re[i], od_b_im[i],
                             od_c_re[i], od_c_im[i])
            y_t = _s5(u_tb, prm)
            h = _od_out(y_t, u_t, od_d[i].reshape(C_WIDTH, 1), od_w_glu, od_b_glu[i].reshape(C_WIDTH, 1),
                        od_w_out, _row(g[3]), h, i)
        h = _ffn(h, _row(g[4]), _row(g[5]), ffn_w_gate, ffn_w_up, ffn_w_down, l, 1)
    return h.reshape(x.shape)
```

```python
import functools

import jax
import jax.numpy as jnp
from jax import lax
from jax.experimental import pallas as pl
from jax.experimental.pallas import tpu as pltpu

F32 = jnp.float32
BF16 = jnp.bfloat16

D_MODEL = 2048
SEQ = 8192
D_FF = 5632
EPS = 1e-6

A_HEADS = 8
A_HEAD_DIM = 128
A_WIDTH = 1024
A_CHUNK = 128
B_HEADS = 4
B_DK = 128
B_DV = 256
B_KEY = 512
B_VAL = 1024
B_GATE_RANK = 16
B_GATE_TAU = 16.0
B_CHUNK = 64
EVEN_MAIN = 2 * A_WIDTH + 2 * B_KEY + 2 * B_VAL
C_WIDTH = 1024
C_GROUP = 16
C_GROUPS = 64
C_STATE = 64

LANES = 128
VMEM_LIMIT = 56 * 1024 * 1024
FFN_VMEM_LIMIT = 60 * 1024 * 1024
EV_IN_VMEM_LIMIT = 62 * 1024 * 1024

ROW_TILE = 512
EV_IN_ROWS = 1024
EV_W_SLOTS = 3
FFN_ROWS = 1024
FFN_SLABS = 8
FFN_EPI_ROWS = 8
FF_TILE = 512
S5_CHUNK = 32
S5_NC = SEQ // S5_CHUNK
S5_COLS = C_GROUP * S5_CHUNK
S5_GB = 8


def _rms(x, g):
    return x * lax.rsqrt(jnp.mean(x * x, axis=-1, keepdims=True) + EPS) * g


def _dot(a, b):
    return jnp.dot(a, b, preferred_element_type=F32)


def _dot_nt(a, b):
    return lax.dot_general(a, b, (((1,), (1,)), ((), ())), preferred_element_type=F32)


def _dot_tn(a, b):
    return lax.dot_general(a, b, (((0,), (0,)), ((), ())), preferred_element_type=F32)


def _params(*sem, vmem=VMEM_LIMIT):
    return pltpu.CompilerParams(dimension_semantics=sem, vmem_limit_bytes=vmem)


def _ffn_kernel(h_hbm, gpre_ref, gpost_ref, wg_ref, wu_ref, wd_ref, o_hbm,
                h_ref, acc_ref, xn_ref, h_sem, o_sem):
    i, j = pl.program_id(0), pl.program_id(1)
    n_i, n_j = pl.num_programs(0), pl.num_programs(1)
    tm = h_ref.shape[0]
    slab = tm // FFN_SLABS

    def h_copy(tile, r):
        return pltpu.make_async_copy(h_hbm.at[pl.ds(tile * tm + r * slab, slab), :],
                                     h_ref.at[pl.ds(r * slab, slab), :], h_sem.at[r])

    def o_copy(tile, r):
        return pltpu.make_async_copy(acc_ref.at[pl.ds(r * slab, slab), :],
                                     o_hbm.at[pl.ds(tile * tm + r * slab, slab), :], o_sem.at[r])

    @pl.when(j == 0)
    def _():
        @pl.when(i == 0)
        def _():
            for r in range(FFN_SLABS):
                h_copy(0, r).start()

        for r in range(FFN_SLABS):
            rows = pl.ds(r * slab, slab)
            h_copy(i, r).wait()
            xn_ref[rows, :] = _rms(h_ref[rows, :], gpre_ref[...]).astype(BF16)

            @pl.when(i > 0)
            def _():
                o_copy(i - 1, r).wait()
            acc_ref[rows, :] = jnp.zeros((slab, acc_ref.shape[1]), F32)

    xn = xn_ref[...]
    hf = wg_ref.shape[1] // 2
    gate_a = _dot(xn, wg_ref[:, :hf].astype(BF16))
    up_a = _dot(xn, wu_ref[:, :hf].astype(BF16))
    gate_b = _dot(xn, wg_ref[:, hf:].astype(BF16))
    up_b = _dot(xn, wu_ref[:, hf:].astype(BF16))
    act_a = (jax.nn.silu(gate_a) * up_a).astype(BF16)
    act_b = (jax.nn.silu(gate_b) * up_b).astype(BF16)
    acc_ref[...] += _dot(act_a, wd_ref[:hf, :].astype(BF16)) + _dot(act_b, wd_ref[hf:, :].astype(BF16))

    @pl.when(j == n_j - 1)
    def _():
        for r in range(FFN_SLABS):
            for q in range(slab // FFN_EPI_ROWS):
                rows = pl.ds(r * slab + q * FFN_EPI_ROWS, FFN_EPI_ROWS)
                acc_ref[rows, :] = h_ref[rows, :] + 0.5 * _rms(acc_ref[rows, :], gpost_ref[...])
            o_copy(i, r).start()

            @pl.when(i + 1 < n_i)
            def _():
                h_copy(i + 1, r).start()

        @pl.when(i == n_i - 1)
        def _():
            for r in range(FFN_SLABS):
                o_copy(i, r).wait()


def _ffn(h, g_pre, g_post, wg, wu, wd, layer, half):
    tm, tf = FFN_ROWS, FF_TILE
    return pl.pallas_call(
        _ffn_kernel,
        grid=(SEQ // tm, D_FF // tf),
        in_specs=[
            pl.BlockSpec(memory_space=pl.ANY),
            pl.BlockSpec((1, D_MODEL), lambda i, j: (0, 0)),
            pl.BlockSpec((1, D_MODEL), lambda i, j: (0, 0)),
            pl.BlockSpec((None, None, D_MODEL, tf), lambda i, j: (layer, half, 0, j)),
            pl.BlockSpec((None, None, D_MODEL, tf), lambda i, j: (layer, half, 0, j)),
            pl.BlockSpec((None, None, tf, D_MODEL), lambda i, j: (layer, half, j, 0)),
        ],
        out_specs=pl.BlockSpec(memory_space=pl.ANY),
        out_shape=jax.ShapeDtypeStruct((SEQ, D_MODEL), F32),
        scratch_shapes=[
            pltpu.VMEM((tm, D_MODEL), F32),
            pltpu.VMEM((tm, D_MODEL), F32),
            pltpu.VMEM((tm, D_MODEL), BF16),
            pltpu.SemaphoreType.DMA((FFN_SLABS,)),
            pltpu.SemaphoreType.DMA((FFN_SLABS,)),
        ],
        compiler_params=_params("arbitrary", "arbitrary", vmem=FFN_VMEM_LIMIT),
        name="ffn",
    )(h, g_pre, g_post, wg, wu, wd)


def _ev_in_kernel(h_ref, g_ref, w_hbm, wlr_ref, wg2_ref, bg_ref, lng_ref, lnb_ref, p_ref, gk_ref,
                  xn_ref, w_ring, w_sem, *, layer):
    i, j = pl.program_id(0), pl.program_id(1)
    n_j = pl.num_programs(1)
    step = i * n_j + j
    n_steps = pl.num_programs(0) * n_j
    tn = w_ring.shape[1]

    def w_copy(s):
        blk = lax.rem(s, n_j)
        slot = lax.rem(s, EV_W_SLOTS)
        return pltpu.make_async_copy(w_hbm.at[layer, pl.ds(blk * tn, tn), :], w_ring.at[slot], w_sem.at[slot])

    @pl.when(step == 0)
    def _():
        for s in range(EV_W_SLOTS - 1):
            w_copy(s).start()

    @pl.when(step + (EV_W_SLOTS - 1) < n_steps)
    def _():
        w_copy(step + (EV_W_SLOTS - 1)).start()

    w_copy(step).wait()

    @pl.when(j == 0)
    def _():
        xn = _rms(h_ref[...], g_ref[...]).astype(BF16)
        xn_ref[...] = xn
        lane = lax.broadcasted_iota(jnp.int32, (xn.shape[0], LANES), 1)
        glr = jnp.where(lane < B_GATE_RANK, _dot_nt(xn, wlr_ref[...].astype(BF16)), 0.0).astype(BF16)
        pre = _dot(glr, wg2_ref[...]) + bg_ref[...]
        gk_ref[...] = jax.nn.log_sigmoid(pre) * (1.0 / B_GATE_TAU)

    def project():
        return _dot_nt(xn_ref[...], w_ring[lax.rem(step, EV_W_SLOTS)].astype(BF16))

    @pl.when(j == 0)
    def _():
        p_ref[...] = jax.nn.gelu(project())

    @pl.when(j == 1)
    def _():
        va = jax.nn.gelu(project())
        vc = va - jnp.mean(va, axis=-1, keepdims=True)
        vln = vc * lax.rsqrt(jnp.mean(vc * vc, axis=-1, keepdims=True) + EPS)
        p_ref[...] = vln * lng_ref[...] + lnb_ref[...]

    @pl.when(jnp.logical_or(j == 2, j == 3))
    def _():
        p_ref[...] = project()

    @pl.when(j == 4)
    def _():
        p_ref[...] = jax.nn.silu(project())


def _ev_in(h, g, w_in_t, w_g2, b_gate, ln_g, ln_b, layer):
    tm, tn = EV_IN_ROWS, A_WIDTH
    assert tn == 2 * B_KEY == B_VAL, "the kernel's per-block activations assume blocks u | v | q,k | v | r"
    return pl.pallas_call(
        functools.partial(_ev_in_kernel, layer=layer),
        grid=(SEQ // tm, EVEN_MAIN // tn),
        in_specs=[
            pl.BlockSpec((tm, D_MODEL), lambda i, j: (i, 0)),
            pl.BlockSpec((1, D_MODEL), lambda i, j: (0, 0)),
            pl.BlockSpec(memory_space=pl.ANY),
            pl.BlockSpec((None, LANES, D_MODEL), lambda i, j: (layer, EVEN_MAIN // LANES, 0),
                         pipeline_mode=pl.Buffered(1)),
            pl.BlockSpec((LANES, B_KEY), lambda i, j: (0, 0)),
            pl.BlockSpec((1, B_KEY), lambda i, j: (0, 0)),
            pl.BlockSpec((1, A_WIDTH), lambda i, j: (0, 0)),
            pl.BlockSpec((1, A_WIDTH), lambda i, j: (0, 0)),
        ],
        out_specs=[
            pl.BlockSpec((tm, tn), lambda i, j: (i, j)),
            pl.BlockSpec((tm, B_KEY), lambda i, j: (i, 0), pipeline_mode=pl.Buffered(1)),
        ],
        out_shape=[
            jax.ShapeDtypeStruct((SEQ, EVEN_MAIN), F32),
            jax.ShapeDtypeStruct((SEQ, B_KEY), F32),
        ],
        scratch_shapes=[
            pltpu.VMEM((tm, D_MODEL), BF16),
            pltpu.VMEM((EV_W_SLOTS, tn, D_MODEL), F32),
            pltpu.SemaphoreType.DMA((EV_W_SLOTS,)),
        ],
        compiler_params=_params("arbitrary", "arbitrary", vmem=EV_IN_VMEM_LIMIT),
        name="ev_in",
    )(h, g, w_in_t, w_in_t, w_g2, b_gate, ln_g, ln_b)


def _ev_mix_kernel(za_ref, q_ref, k_ref, v_ref, r_ref, gk_ref, ws_ref, bs_ref,
                   gng_ref, y_ref, vln_ref, st_ref):
    tm = za_ref.shape[0]

    @pl.when(pl.program_id(0) == 0)
    def _():
        st_ref[...] = jnp.zeros_like(st_ref)

    vln_ref[...] = za_ref[:, A_WIDTH:].astype(BF16)

    n_a = tm // A_CHUNK
    row = lax.broadcasted_iota(jnp.int32, (A_CHUNK, A_CHUNK), 0)
    col = lax.broadcasted_iota(jnp.int32, (A_CHUNK, A_CHUNK), 1)
    for hd in range(A_HEADS):
        cs = slice(hd * A_HEAD_DIM, (hd + 1) * A_HEAD_DIM)
        w = jnp.where(row >= col, ws_ref[hd], 0.0).astype(BF16)
        rhs = jnp.concatenate(
            [vln_ref[c * A_CHUNK:(c + 1) * A_CHUNK, cs] for c in range(n_a)], axis=1)
        s = _dot(w, rhs)
        for c in range(n_a):
            rs = slice(c * A_CHUNK, (c + 1) * A_CHUNK)
            sc = s[:, c * A_HEAD_DIM:(c + 1) * A_HEAD_DIM] + bs_ref[:, cs]
            y_ref[rs, cs] = (za_ref[rs, cs] * sc).astype(BF16)

    n_b = tm // B_CHUNK
    r64 = lax.broadcasted_iota(jnp.int32, (B_CHUNK, B_CHUNK), 0)
    c64 = lax.broadcasted_iota(jnp.int32, (B_CHUNK, B_CHUNK), 1)
    causal = r64 >= c64
    tri = jnp.where(causal, 1.0, 0.0).astype(BF16)
    scale = B_DK ** -0.5
    for c in range(n_b):
        rs = slice(c * B_CHUNK, (c + 1) * B_CHUNK)
        g = gk_ref[rs, :]
        g_hi = g.astype(BF16)
        g_lo = (g - g_hi.astype(F32)).astype(BF16)
        bcum = _dot(tri, g_hi) + _dot(tri, g_lo)
        b_last = bcum[B_CHUNK - 1:B_CHUNK, :]
        q = q_ref[rs, :] * scale
        k = k_ref[rs, :]
        q_dec = (q * jnp.exp(bcum)).astype(BF16)
        k_inv = (k * jnp.exp(-bcum)).astype(BF16)
        k_end = (k * jnp.exp(b_last - bcum)).astype(BF16)
        decay = jnp.exp(b_last)
        for hd in range(B_HEADS):
            ks = slice(hd * B_DK, (hd + 1) * B_DK)
            vs = slice(hd * B_DV, (hd + 1) * B_DV)
            v = v_ref[rs, vs].astype(BF16)
            scores = jnp.where(causal, _dot_nt(q_dec[:, ks], k_inv[:, ks]), 0.0).astype(BF16)
            st = st_ref[hd]
            o = _dot(scores, v) + _dot_nt(q_dec[:, ks], st.astype(BF16))
            st_ref[hd] = decay[:, ks] * st + _dot_tn(v, k_end[:, ks])
            o = o * lax.rsqrt(jnp.mean(o * o, axis=-1, keepdims=True) + EPS) * gng_ref[...]
            y_ref[rs, A_WIDTH + hd * B_DV:A_WIDTH + (hd + 1) * B_DV] = (
                o * r_ref[rs, vs]).astype(BF16)


def _ev_mix(p, gk, w_s, bs_full, gla_norm_g):
    tm = ROW_TILE
    return pl.pallas_call(
        _ev_mix_kernel,
        grid=(SEQ // tm,),
        in_specs=[
            pl.BlockSpec((tm, 2 * A_WIDTH), lambda i: (i, 0)),
            pl.BlockSpec((tm, B_KEY), lambda i: (i, 2 * A_WIDTH // B_KEY)),
            pl.BlockSpec((tm, B_KEY), lambda i: (i, 2 * A_WIDTH // B_KEY + 1)),
            pl.BlockSpec((tm, B_VAL), lambda i: (i, (2 * A_WIDTH + 2 * B_KEY) // B_VAL)),
            pl.BlockSpec((tm, B_VAL), lambda i: (i, (2 * A_WIDTH + 2 * B_KEY) // B_VAL + 1)),
            pl.BlockSpec((tm, B_KEY), lambda i: (i, 0)),
            pl.BlockSpec((A_HEADS, A_CHUNK, A_CHUNK), lambda i: (0, 0, 0)),
            pl.BlockSpec((A_CHUNK, A_WIDTH), lambda i: (0, 0)),
            pl.BlockSpec((1, B_DV), lambda i: (0, 0)),
        ],
        out_specs=pl.BlockSpec((tm, A_WIDTH + B_VAL), lambda i: (i, 0)),
        out_shape=jax.ShapeDtypeStruct((SEQ, A_WIDTH + B_VAL), BF16),
        scratch_shapes=[
            pltpu.VMEM((tm, A_WIDTH), BF16),
            pltpu.VMEM((B_HEADS, B_DV, B_DK), F32),
        ],
        compiler_params=_params("arbitrary"),
        name="ev_mix",
    )(p, p, p, p, p, gk, w_s, bs_full, gla_norm_g)


def _ev_out_kernel(y_ref, w_ref, g_ref, h_ref, o_ref):
    o_ref[...] = h_ref[...] + _rms(_dot(y_ref[...], w_ref[...].astype(BF16)), g_ref[...])


def _ev_out(y, w, g, h, layer):
    tm = ROW_TILE
    kdim = y.shape[1]
    return pl.pallas_call(
        _ev_out_kernel,
        grid=(SEQ // tm,),
        in_specs=[
            pl.BlockSpec((tm, kdim), lambda i: (i, 0)),
            pl.BlockSpec((None, kdim, D_MODEL), lambda i: (layer, 0, 0), pipeline_mode=pl.Buffered(1)),
            pl.BlockSpec((1, D_MODEL), lambda i: (0, 0)),
            pl.BlockSpec((tm, D_MODEL), lambda i: (i, 0)),
        ],
        out_specs=pl.BlockSpec((tm, D_MODEL), lambda i: (i, 0)),
        out_shape=jax.ShapeDtypeStruct((SEQ, D_MODEL), F32),
        compiler_params=_params("parallel"),
        name="ev_out",
    )(y, w, g, h)


def _od_in_kernel(h_ref, g_ref, w_ref, ut_ref, utb_ref):
    ut = lax.dot_general(w_ref[...].astype(BF16), _rms(h_ref[...], g_ref[...]).astype(BF16),
                         (((0,), (1,)), ((), ())), preferred_element_type=F32)
    ut_ref[...] = ut
    utb_ref[...] = ut.astype(BF16)


def _od_in(h, g, w, layer):
    tm = ROW_TILE
    return pl.pallas_call(
        _od_in_kernel,
        grid=(SEQ // tm,),
        in_specs=[
            pl.BlockSpec((tm, D_MODEL), lambda i: (i, 0)),
            pl.BlockSpec((1, D_MODEL), lambda i: (0, 0)),
            pl.BlockSpec((None, D_MODEL, C_WIDTH), lambda i: (layer, 0, 0), pipeline_mode=pl.Buffered(1)),
        ],
        out_specs=[
            pl.BlockSpec((C_WIDTH, tm), lambda i: (0, i)),
            pl.BlockSpec((C_WIDTH, tm), lambda i: (0, i)),
        ],
        out_shape=[
            jax.ShapeDtypeStruct((C_WIDTH, SEQ), F32),
            jax.ShapeDtypeStruct((C_WIDTH, SEQ), BF16),
        ],
        compiler_params=_params("parallel"),
        name="od_in",
    )(h, g, w)


def _split3(x):
    hi = x.astype(BF16)
    r1 = x - hi.astype(F32)
    mid = r1.astype(BF16)
    lo = (r1 - mid.astype(F32)).astype(BF16)
    return hi, mid, lo


def _dot_f32(a, b):
    a0, a1, a2 = _split3(a)
    b0, b1, b2 = _split3(b)
    return (_dot(a0, b0) + (_dot(a0, b1) + _dot(a1, b0))
            + (_dot(a0, b2) + _dot(a1, b1) + _dot(a2, b0)))


def _lane_expand(x, e):
    x0, x1, x2 = _split3(x)
    return _dot(x0, e) + _dot(x1, e) + _dot(x2, e)


def _s5_group_operators(ar, ai, bbr, bbi, bbt, crt, cit):
    tau = lax.broadcasted_iota(jnp.int32, (C_STATE, LANES), 1) & (S5_CHUNK - 1)
    expand = (lax.broadcasted_iota(jnp.int32, (C_GROUP, S5_COLS), 1) // S5_CHUNK
              == lax.broadcasted_iota(jnp.int32, (C_GROUP, S5_COLS), 0)).astype(BF16)
    one = jnp.ones((C_STATE, LANES), F32)
    zero = jnp.zeros((C_STATE, LANES), F32)
    pr, pi = one, zero
    rr, ri = one, zero
    fr, fi = ar, ai
    for b in range(S5_CHUNK.bit_length() - 1):
        bit = ((tau >> b) & 1) == 1
        pr, pi = jnp.where(bit, pr * fr - pi * fi, pr), jnp.where(bit, pr * fi + pi * fr, pi)
        rr, ri = jnp.where(bit, rr, rr * fr - ri * fi), jnp.where(bit, ri, rr * fi + ri * fr)
        fr, fi = fr * fr - fi * fi, 2.0 * (fr * fi)
    wide = lambda a: jnp.concatenate([a] * (S5_COLS // LANES), axis=1)
    pr, pi, rr, ri, ar, ai = wide(pr), wide(pi), wide(rr), wide(ri), wide(ar), wide(ai)
    cr = _lane_expand(crt, expand)
    ci = _lane_expand(cit, expand)
    cer = cr * pr - ci * pi
    cei = cr * pi + ci * pr
    kv = _dot_f32(bbt, jnp.concatenate([cer, -cei], axis=0))
    qr = cer * ar - cei * ai
    qi = cer * ai + cei * ar
    q = jnp.concatenate([qr, -qi], axis=0).astype(BF16)
    br = _lane_expand(bbr, expand)
    bi = _lane_expand(bbi, expand)
    ptr = br * rr - bi * ri
    pti = br * ri + bi * rr
    pt = jnp.concatenate([ptr, pti], axis=0).astype(BF16)
    pts = jnp.concatenate([pti, ptr], axis=0).astype(BF16)
    return kv, pt, pts, q


def _s5_kernel(ut2_ref, ar_ref, ai_ref, bbr_ref, bbi_ref, bbt_ref, crt_ref, cit_ref,
               a1_ref, a2_ref, a2s_ref, yt2_ref, ut_ref, yt_ref, u_ref, m_ref, q_ref, inc_ref, incs_ref,
               xs_ref, y_ref):
    gb = u_ref.shape[0]
    w = 2 * C_STATE
    ut_ref[...] = ut2_ref[...].reshape(ut_ref.shape)
    per = LANES // S5_CHUNK
    nsup = S5_NC // per
    keep = ((lax.broadcasted_iota(jnp.int32, (S5_CHUNK, LANES), 1) & (S5_CHUNK - 1))
            >= lax.broadcasted_iota(jnp.int32, (S5_CHUNK, LANES), 0))
    for gi in range(gb):
        for cp in range(C_GROUP):
            for k in range(per):
                u_ref[gi, k * nsup:(k + 1) * nsup, cp * S5_CHUNK:(cp + 1) * S5_CHUNK] = (
                    ut_ref[gi * C_GROUP + cp, :, k * S5_CHUNK:(k + 1) * S5_CHUNK])
        kv, pt, pts, q = _s5_group_operators(ar_ref[gi], ai_ref[gi], bbr_ref[gi], bbi_ref[gi], bbt_ref[gi],
                                             crt_ref[gi], cit_ref[gi])
        q_ref[gi] = q
        for cp in range(C_GROUP):
            taps = jnp.broadcast_to(kv[cp:cp + 1, :], (S5_CHUNK, S5_COLS))
            for v in range(S5_COLS // LANES):
                tile = pltpu.roll(taps[:, v * LANES:(v + 1) * LANES], 0, 1, stride=1, stride_axis=0)
                m_ref[gi, cp * S5_CHUNK:(cp + 1) * S5_CHUNK, v * LANES:(v + 1) * LANES] = (
                    jnp.where(keep, tile, 0.0).astype(BF16))
        u = u_ref[gi]
        inc_ref[:, gi * w:(gi + 1) * w] = _dot_nt(u, pt)
        incs_ref[:, gi * w:(gi + 1) * w] = _dot_nt(u, pts)

    a1 = a1_ref[...]
    a2 = a2_ref[...]
    a2s = a2s_ref[...]

    def step(n, carry):
        x, xs = carry
        row = pl.ds((n & (per - 1)) * nsup + (n >> (per.bit_length() - 1)), 1)
        xs_ref[row, :] = x
        x_new = a1 * x + a2 * xs + inc_ref[row, :]
        xs_new = a1 * xs + a2s * x + incs_ref[row, :]
        return x_new, xs_new

    zero = jnp.zeros((1, gb * w), F32)
    lax.fori_loop(0, S5_NC, step, (zero, zero))

    for gi in range(gb):
        xst = xs_ref[:, gi * w:(gi + 1) * w].astype(BF16)
        y_ref[...] = _dot(u_ref[gi], m_ref[gi]) + _dot(xst, q_ref[gi])
        for c in range(C_GROUP):
            for k in range(per):
                yt_ref[gi * C_GROUP + c, :, k * S5_CHUNK:(k + 1) * S5_CHUNK] = (
                    y_ref[k * nsup:(k + 1) * nsup, c * S5_CHUNK:(c + 1) * S5_CHUNK])
    yt2_ref[...] = yt_ref[...].reshape(yt2_ref.shape)


def _s5(u_tb, prm):
    gb = S5_GB
    w = 2 * C_STATE
    grp3 = lambda i: (i, 0, 0)
    return pl.pallas_call(
        _s5_kernel,
        grid=(C_GROUPS // gb,),
        in_specs=[
            pl.BlockSpec((gb * C_GROUP, SEQ), lambda i: (i, 0)),
            pl.BlockSpec((gb, C_STATE, LANES), grp3),
            pl.BlockSpec((gb, C_STATE, LANES), grp3),
            pl.BlockSpec((gb, C_STATE, C_GROUP), grp3),
            pl.BlockSpec((gb, C_STATE, C_GROUP), grp3),
            pl.BlockSpec((gb, C_GROUP, w), grp3),
            pl.BlockSpec((gb, C_STATE, C_GROUP), grp3),
            pl.BlockSpec((gb, C_STATE, C_GROUP), grp3),
            pl.BlockSpec((1, gb * w), lambda i: (0, i)),
            pl.BlockSpec((1, gb * w), lambda i: (0, i)),
            pl.BlockSpec((1, gb * w), lambda i: (0, i)),
        ],
        out_specs=pl.BlockSpec((gb * C_GROUP, SEQ), lambda i: (i, 0)),
        out_shape=jax.ShapeDtypeStruct((C_WIDTH, SEQ), F32),
        scratch_shapes=[
            pltpu.VMEM((gb * C_GROUP, SEQ // LANES, LANES), BF16),
            pltpu.VMEM((gb * C_GROUP, SEQ // LANES, LANES), F32),
            pltpu.VMEM((gb, S5_NC, S5_COLS), BF16),
            pltpu.VMEM((gb, S5_COLS, S5_COLS), BF16),
            pltpu.VMEM((gb, w, S5_COLS), BF16),
            pltpu.VMEM((S5_NC, gb * w), F32),
            pltpu.VMEM((S5_NC, gb * w), F32),
            pltpu.VMEM((S5_NC, gb * w), F32),
            pltpu.VMEM((S5_NC, S5_COLS), F32),
        ],
        compiler_params=_params("parallel"),
        name="s5",
    )(u_tb, *prm)


def _s5_params(lam_re, lam_im, log_dt, b_re, b_im, c_re, c_im):
    lr = jnp.minimum(lam_re, -1e-4)
    li = lam_im
    dt = jnp.exp(log_dt)[:, None]
    mag = jnp.exp(lr * dt)
    ar = mag * jnp.cos(li * dt)
    ai = mag * jnp.sin(li * dt)
    den = lr * lr + li * li
    nr = ar - 1.0
    cr = (nr * lr + ai * li) / den
    ci = (ai * lr - nr * li) / den
    bbr = cr[..., None] * b_re - ci[..., None] * b_im
    bbi = cr[..., None] * b_im + ci[..., None] * b_re
    bbt = jnp.concatenate([bbr, bbi], axis=1).transpose(0, 2, 1)
    atr, ati = ar, ai
    for _ in range(S5_CHUNK.bit_length() - 1):
        atr, ati = atr * atr - ati * ati, 2.0 * (atr * ati)
    a1 = jnp.concatenate([atr, atr], axis=-1).reshape(1, -1)
    a2 = jnp.concatenate([-ati, ati], axis=-1).reshape(1, -1)
    a2s = jnp.concatenate([ati, -ati], axis=-1).reshape(1, -1)
    bcast = lambda a: jnp.broadcast_to(a[..., None], a.shape + (LANES,))
    return (bcast(ar), bcast(ai), bbr, bbi, bbt, c_re.transpose(0, 2, 1), c_im.transpose(0, 2, 1),
            a1, a2, a2s)


def _od_out_kernel(yt_ref, ut_ref, d_ref, wglu_ref, bglu_ref, wout_ref, g_ref, h_ref, o_ref):
    y = yt_ref[...] + d_ref[...] * ut_ref[...]
    z = jax.nn.gelu(y)
    gate = jax.nn.sigmoid(_dot_tn(wglu_ref[...].astype(BF16), z.astype(BF16)) + bglu_ref[...])
    m = _dot_tn((z * gate).astype(BF16), wout_ref[...].astype(BF16))
    o_ref[...] = h_ref[...] + _rms(m, g_ref[...])


def _od_out(y_t, u_t, d, w_glu, b_glu, w_out, g, h, layer):
    tm = ROW_TILE
    return pl.pallas_call(
        _od_out_kernel,
        grid=(SEQ // tm,),
        in_specs=[
            pl.BlockSpec((C_WIDTH, tm), lambda i: (0, i)),
            pl.BlockSpec((C_WIDTH, tm), lambda i: (0, i)),
            pl.BlockSpec((C_WIDTH, 1), lambda i: (0, 0)),
            pl.BlockSpec((None, C_WIDTH, C_WIDTH), lambda i: (layer, 0, 0), pipeline_mode=pl.Buffered(1)),
            pl.BlockSpec((C_WIDTH, 1), lambda i: (0, 0)),
            pl.BlockSpec((None, C_WIDTH, D_MODEL), lambda i: (layer, 0, 0), pipeline_mode=pl.Buffered(1)),
            pl.BlockSpec((1, D_MODEL), lambda i: (0, 0)),
            pl.BlockSpec((tm, D_MODEL), lambda i: (i, 0)),
        ],
        out_specs=pl.BlockSpec((tm, D_MODEL), lambda i: (i, 0)),
        out_shape=jax.ShapeDtypeStruct((SEQ, D_MODEL), F32),
        compiler_params=_params("parallel"),
        name="od_out",
    )(y_t, u_t, d, w_glu, b_glu, w_out, g, h)


def _row(v):
    return v.reshape(1, -1).astype(F32)


def kernel(x, norm_g, ffn_w_gate, ffn_w_up, ffn_w_down, ev_w_in, ev_ln_g, ev_ln_b, ev_w_s, ev_b_s, ev_w_gate2, ev_b_gate, ev_gla_norm_g, ev_w_out, od_w_in, od_lam_re, od_lam_im, od_log_dt, od_b_re, od_b_im, od_c_re, od_c_im, od_d, od_w_glu, od_b_glu, od_w_out):
    depth = norm_g.shape[0]
    h = x.reshape(SEQ, D_MODEL)
    for l in range(depth):
        i = l // 2
        g = norm_g[l]
        h = _ffn(h, _row(g[0]), _row(g[1]), ffn_w_gate, ffn_w_up, ffn_w_down, l, 0)
        if l % 2 == 0:
            w_g2 = jnp.pad(ev_w_gate2[i], ((0, LANES - B_GATE_RANK), (0, 0))).astype(BF16)
            p, gk = _ev_in(h, _row(g[2]), jnp.swapaxes(ev_w_in, 1, 2), w_g2, _row(ev_b_gate[i]),
                           _row(ev_ln_g[i]), _row(ev_ln_b[i]), i)
            bs_full = jnp.repeat(ev_b_s[i].T, A_HEAD_DIM, axis=1).astype(F32)
            y = _ev_mix(p, gk, ev_w_s[i], bs_full, _row(ev_gla_norm_g[i]))
            h = _ev_out(y, ev_w_out, _row(g[3]), h, i)
        else:
            u_t, u_tb = _od_in(h, _row(g[2]), od_w_in, i)
            prm = _s5_params(od_lam_re[i], od_lam_im[i], od_log_dt[i], od_b_re[i], od_b_im[i],
                             od_c_re[i], od_c_im[i])
            y_t = _s5(u_tb, prm)
            h = _od_out(y_t, u_t, od_d[i].reshape(C_WIDTH, 1), od_w_glu, od_b_glu[i].reshape(C_WIDTH, 1),
                        od_w_out, _row(g[3]), h, i)
        h = _ffn(h, _row(g[4]), _row(g[5]), ffn_w_gate, ffn_w_up, ffn_w_down, l, 1)
    return h.reshape(x.shape)
```

```python
import jax
import jax.numpy as jnp
from jax import lax
from jax.experimental import pallas as pl
from jax.experimental.pallas import tpu as pltpu

F32 = jnp.float32
BF16 = jnp.bfloat16

D_MODEL = 2048
SEQ = 8192
D_FF = 5632
EPS = 1e-6

A_HEADS = 8
A_HEAD_DIM = 128
A_WIDTH = 1024
A_CHUNK = 128
B_HEADS = 4
B_DK = 128
B_DV = 256
B_KEY = 512
B_VAL = 1024
B_GATE_RANK = 16
B_GATE_TAU = 16.0
B_CHUNK = 64
EVEN_MAIN = 2 * A_WIDTH + 2 * B_KEY + 2 * B_VAL
C_WIDTH = 1024
C_GROUP = 16
C_GROUPS = 64
C_STATE = 64

LANES = 128
VMEM_LIMIT = 56 * 1024 * 1024
FFN_VMEM_LIMIT = 60 * 1024 * 1024

ROW_TILE = 512
EV_IN_ROWS = 1024
FFN_ROWS = 1024
FFN_SLABS = 8
FFN_EPI_ROWS = 8
FF_TILE = 512
S5_CHUNK = 32
S5_NC = SEQ // S5_CHUNK
S5_COLS = C_GROUP * S5_CHUNK
S5_GB = 8


def _rms(x, g):
    return x * lax.rsqrt(jnp.mean(x * x, axis=-1, keepdims=True) + EPS) * g


def _dot(a, b):
    return jnp.dot(a, b, preferred_element_type=F32)


def _dot_nt(a, b):
    return lax.dot_general(a, b, (((1,), (1,)), ((), ())), preferred_element_type=F32)


def _dot_tn(a, b):
    return lax.dot_general(a, b, (((0,), (0,)), ((), ())), preferred_element_type=F32)


def _params(*sem, vmem=VMEM_LIMIT):
    return pltpu.CompilerParams(dimension_semantics=sem, vmem_limit_bytes=vmem)


def _ffn_kernel(h_hbm, gpre_ref, gpost_ref, wg_ref, wu_ref, wd_ref, o_hbm,
                h_ref, acc_ref, xn_ref, h_sem, o_sem):
    i, j = pl.program_id(0), pl.program_id(1)
    n_i, n_j = pl.num_programs(0), pl.num_programs(1)
    tm = h_ref.shape[0]
    slab = tm // FFN_SLABS

    def h_copy(tile, r):
        return pltpu.make_async_copy(h_hbm.at[pl.ds(tile * tm + r * slab, slab), :],
                                     h_ref.at[pl.ds(r * slab, slab), :], h_sem.at[r])

    def o_copy(tile, r):
        return pltpu.make_async_copy(acc_ref.at[pl.ds(r * slab, slab), :],
                                     o_hbm.at[pl.ds(tile * tm + r * slab, slab), :], o_sem.at[r])

    @pl.when(j == 0)
    def _():
        @pl.when(i == 0)
        def _():
            for r in range(FFN_SLABS):
                h_copy(0, r).start()

        for r in range(FFN_SLABS):
            rows = pl.ds(r * slab, slab)
            h_copy(i, r).wait()
            xn_ref[rows, :] = _rms(h_ref[rows, :], gpre_ref[...]).astype(BF16)

            @pl.when(i > 0)
            def _():
                o_copy(i - 1, r).wait()
            acc_ref[rows, :] = jnp.zeros((slab, acc_ref.shape[1]), F32)

    xn = xn_ref[...]
    hf = wg_ref.shape[1] // 2
    gate_a = _dot(xn, wg_ref[:, :hf].astype(BF16))
    up_a = _dot(xn, wu_ref[:, :hf].astype(BF16))
    gate_b = _dot(xn, wg_ref[:, hf:].astype(BF16))
    up_b = _dot(xn, wu_ref[:, hf:].astype(BF16))
    act_a = (jax.nn.silu(gate_a) * up_a).astype(BF16)
    act_b = (jax.nn.silu(gate_b) * up_b).astype(BF16)
    acc_ref[...] += _dot(act_a, wd_ref[:hf, :].astype(BF16)) + _dot(act_b, wd_ref[hf:, :].astype(BF16))

    @pl.when(j == n_j - 1)
    def _():
        for r in range(FFN_SLABS):
            for q in range(slab // FFN_EPI_ROWS):
                rows = pl.ds(r * slab + q * FFN_EPI_ROWS, FFN_EPI_ROWS)
                acc_ref[rows, :] = h_ref[rows, :] + 0.5 * _rms(acc_ref[rows, :], gpost_ref[...])
            o_copy(i, r).start()

            @pl.when(i + 1 < n_i)
            def _():
                h_copy(i + 1, r).start()

        @pl.when(i == n_i - 1)
        def _():
            for r in range(FFN_SLABS):
                o_copy(i, r).wait()


def _ffn(h, g_pre, g_post, wg, wu, wd, layer, half):
    tm, tf = FFN_ROWS, FF_TILE
    return pl.pallas_call(
        _ffn_kernel,
        grid=(SEQ // tm, D_FF // tf),
        in_specs=[
            pl.BlockSpec(memory_space=pl.ANY),
            pl.BlockSpec((1, D_MODEL), lambda i, j: (0, 0)),
            pl.BlockSpec((1, D_MODEL), lambda i, j: (0, 0)),
            pl.BlockSpec((None, None, D_MODEL, tf), lambda i, j: (layer, half, 0, j)),
            pl.BlockSpec((None, None, D_MODEL, tf), lambda i, j: (layer, half, 0, j)),
            pl.BlockSpec((None, None, tf, D_MODEL), lambda i, j: (layer, half, j, 0)),
        ],
        out_specs=pl.BlockSpec(memory_space=pl.ANY),
        out_shape=jax.ShapeDtypeStruct((SEQ, D_MODEL), F32),
        scratch_shapes=[
            pltpu.VMEM((tm, D_MODEL), F32),
            pltpu.VMEM((tm, D_MODEL), F32),
            pltpu.VMEM((tm, D_MODEL), BF16),
            pltpu.SemaphoreType.DMA((FFN_SLABS,)),
            pltpu.SemaphoreType.DMA((FFN_SLABS,)),
        ],
        compiler_params=_params("arbitrary", "arbitrary", vmem=FFN_VMEM_LIMIT),
        name="ffn",
    )(h, g_pre, g_post, wg, wu, wd)


def _ev_in_kernel(h_ref, g_ref, w_ref, wlr_ref, wg2_ref, bg_ref, lng_ref, lnb_ref, p_ref, gk_ref, xn_ref):
    j = pl.program_id(1)

    @pl.when(j == 0)
    def _():
        xn = _rms(h_ref[...], g_ref[...]).astype(BF16)
        xn_ref[...] = xn
        lane = lax.broadcasted_iota(jnp.int32, (xn.shape[0], LANES), 1)
        glr = jnp.where(lane < B_GATE_RANK, _dot_nt(xn, wlr_ref[...].astype(BF16)), 0.0).astype(BF16)
        pre = _dot(glr, wg2_ref[...]) + bg_ref[...]
        gk_ref[...] = jax.nn.log_sigmoid(pre) * (1.0 / B_GATE_TAU)

    def project():
        return _dot_nt(xn_ref[...], w_ref[...].astype(BF16))

    @pl.when(j == 0)
    def _():
        p_ref[...] = jax.nn.gelu(project())

    @pl.when(j == 1)
    def _():
        va = jax.nn.gelu(project())
        vc = va - jnp.mean(va, axis=-1, keepdims=True)
        vln = vc * lax.rsqrt(jnp.mean(vc * vc, axis=-1, keepdims=True) + EPS)
        p_ref[...] = vln * lng_ref[...] + lnb_ref[...]

    @pl.when(jnp.logical_or(j == 2, j == 3))
    def _():
        p_ref[...] = project()

    @pl.when(j == 4)
    def _():
        p_ref[...] = jax.nn.silu(project())


def _ev_in(h, g, w_in_t, w_g2, b_gate, ln_g, ln_b, layer):
    tm, tn = EV_IN_ROWS, A_WIDTH
    assert tn == 2 * B_KEY == B_VAL, "the kernel's per-block activations assume blocks u | v | q,k | v | r"
    return pl.pallas_call(
        _ev_in_kernel,
        grid=(SEQ // tm, EVEN_MAIN // tn),
        in_specs=[
            pl.BlockSpec((tm, D_MODEL), lambda i, j: (i, 0)),
            pl.BlockSpec((1, D_MODEL), lambda i, j: (0, 0)),
            pl.BlockSpec((None, tn, D_MODEL), lambda i, j: (layer, j, 0)),
            pl.BlockSpec((None, LANES, D_MODEL), lambda i, j: (layer, EVEN_MAIN // LANES, 0)),
            pl.BlockSpec((LANES, B_KEY), lambda i, j: (0, 0)),
            pl.BlockSpec((1, B_KEY), lambda i, j: (0, 0)),
            pl.BlockSpec((1, A_WIDTH), lambda i, j: (0, 0)),
            pl.BlockSpec((1, A_WIDTH), lambda i, j: (0, 0)),
        ],
        out_specs=[
            pl.BlockSpec((tm, tn), lambda i, j: (i, j)),
            pl.BlockSpec((tm, B_KEY), lambda i, j: (i, 0)),
        ],
        out_shape=[
            jax.ShapeDtypeStruct((SEQ, EVEN_MAIN), F32),
            jax.ShapeDtypeStruct((SEQ, B_KEY), F32),
        ],
        scratch_shapes=[pltpu.VMEM((tm, D_MODEL), BF16)],
        compiler_params=_params("parallel", "arbitrary"),
        name="ev_in",
    )(h, g, w_in_t, w_in_t, w_g2, b_gate, ln_g, ln_b)


def _ev_mix_kernel(za_ref, q_ref, k_ref, v_ref, r_ref, gk_ref, ws_ref, bs_ref,
                   gng_ref, y_ref, vln_ref, st_ref):
    tm = za_ref.shape[0]

    @pl.when(pl.program_id(0) == 0)
    def _():
        st_ref[...] = jnp.zeros_like(st_ref)

    vln_ref[...] = za_ref[:, A_WIDTH:].astype(BF16)

    n_a = tm // A_CHUNK
    row = lax.broadcasted_iota(jnp.int32, (A_CHUNK, A_CHUNK), 0)
    col = lax.broadcasted_iota(jnp.int32, (A_CHUNK, A_CHUNK), 1)
    for hd in range(A_HEADS):
        cs = slice(hd * A_HEAD_DIM, (hd + 1) * A_HEAD_DIM)
        w = jnp.where(row >= col, ws_ref[hd], 0.0).astype(BF16)
        rhs = jnp.concatenate(
            [vln_ref[c * A_CHUNK:(c + 1) * A_CHUNK, cs] for c in range(n_a)], axis=1)
        s = _dot(w, rhs)
        for c in range(n_a):
            rs = slice(c * A_CHUNK, (c + 1) * A_CHUNK)
            sc = s[:, c * A_HEAD_DIM:(c + 1) * A_HEAD_DIM] + bs_ref[:, cs]
            y_ref[rs, cs] = (za_ref[rs, cs] * sc).astype(BF16)

    n_b = tm // B_CHUNK
    r64 = lax.broadcasted_iota(jnp.int32, (B_CHUNK, B_CHUNK), 0)
    c64 = lax.broadcasted_iota(jnp.int32, (B_CHUNK, B_CHUNK), 1)
    causal = r64 >= c64
    tri = jnp.where(causal, 1.0, 0.0).astype(BF16)
    scale = B_DK ** -0.5
    for c in range(n_b):
        rs = slice(c * B_CHUNK, (c + 1) * B_CHUNK)
        g = gk_ref[rs, :]
        g_hi = g.astype(BF16)
        g_lo = (g - g_hi.astype(F32)).astype(BF16)
        bcum = _dot(tri, g_hi) + _dot(tri, g_lo)
        b_last = bcum[B_CHUNK - 1:B_CHUNK, :]
        q = q_ref[rs, :] * scale
        k = k_ref[rs, :]
        q_dec = (q * jnp.exp(bcum)).astype(BF16)
        k_inv = (k * jnp.exp(-bcum)).astype(BF16)
        k_end = (k * jnp.exp(b_last - bcum)).astype(BF16)
        decay = jnp.exp(b_last)
        for hd in range(B_HEADS):
            ks = slice(hd * B_DK, (hd + 1) * B_DK)
            vs = slice(hd * B_DV, (hd + 1) * B_DV)
            v = v_ref[rs, vs].astype(BF16)
            scores = jnp.where(causal, _dot_nt(q_dec[:, ks], k_inv[:, ks]), 0.0).astype(BF16)
            st = st_ref[hd]
            o = _dot(scores, v) + _dot_nt(q_dec[:, ks], st.astype(BF16))
            st_ref[hd] = decay[:, ks] * st + _dot_tn(v, k_end[:, ks])
            o = o * lax.rsqrt(jnp.mean(o * o, axis=-1, keepdims=True) + EPS) * gng_ref[...]
            y_ref[rs, A_WIDTH + hd * B_DV:A_WIDTH + (hd + 1) * B_DV] = (
                o * r_ref[rs, vs]).astype(BF16)


def _ev_mix(p, gk, w_s, bs_full, gla_norm_g):
    tm = ROW_TILE
    return pl.pallas_call(
        _ev_mix_kernel,
        grid=(SEQ // tm,),
        in_specs=[
            pl.BlockSpec((tm, 2 * A_WIDTH), lambda i: (i, 0)),
            pl.BlockSpec((tm, B_KEY), lambda i: (i, 2 * A_WIDTH // B_KEY)),
            pl.BlockSpec((tm, B_KEY), lambda i: (i, 2 * A_WIDTH // B_KEY + 1)),
            pl.BlockSpec((tm, B_VAL), lambda i: (i, (2 * A_WIDTH + 2 * B_KEY) // B_VAL)),
            pl.BlockSpec((tm, B_VAL), lambda i: (i, (2 * A_WIDTH + 2 * B_KEY) // B_VAL + 1)),
            pl.BlockSpec((tm, B_KEY), lambda i: (i, 0)),
            pl.BlockSpec((A_HEADS, A_CHUNK, A_CHUNK), lambda i: (0, 0, 0)),
            pl.BlockSpec((A_CHUNK, A_WIDTH), lambda i: (0, 0)),
            pl.BlockSpec((1, B_DV), lambda i: (0, 0)),
        ],
        out_specs=pl.BlockSpec((tm, A_WIDTH + B_VAL), lambda i: (i, 0)),
        out_shape=jax.ShapeDtypeStruct((SEQ, A_WIDTH + B_VAL), BF16),
        scratch_shapes=[
            pltpu.VMEM((tm, A_WIDTH), BF16),
            pltpu.VMEM((B_HEADS, B_DV, B_DK), F32),
        ],
        compiler_params=_params("arbitrary"),
        name="ev_mix",
    )(p, p, p, p, p, gk, w_s, bs_full, gla_norm_g)


def _ev_out_kernel(y_ref, w_ref, g_ref, h_ref, o_ref):
    o_ref[...] = h_ref[...] + _rms(_dot(y_ref[...], w_ref[...].astype(BF16)), g_ref[...])


def _ev_out(y, w, g, h, layer):
    tm = ROW_TILE
    kdim = y.shape[1]
    return pl.pallas_call(
        _ev_out_kernel,
        grid=(SEQ // tm,),
        in_specs=[
            pl.BlockSpec((tm, kdim), lambda i: (i, 0)),
            pl.BlockSpec((None, kdim, D_MODEL), lambda i: (layer, 0, 0), pipeline_mode=pl.Buffered(1)),
            pl.BlockSpec((1, D_MODEL), lambda i: (0, 0)),
            pl.BlockSpec((tm, D_MODEL), lambda i: (i, 0)),
        ],
        out_specs=pl.BlockSpec((tm, D_MODEL), lambda i: (i, 0)),
        out_shape=jax.ShapeDtypeStruct((SEQ, D_MODEL), F32),
        compiler_params=_params("parallel"),
        name="ev_out",
    )(y, w, g, h)


def _od_in_kernel(h_ref, g_ref, w_ref, ut_ref, utb_ref):
    ut = lax.dot_general(w_ref[...].astype(BF16), _rms(h_ref[...], g_ref[...]).astype(BF16),
                         (((0,), (1,)), ((), ())), preferred_element_type=F32)
    ut_ref[...] = ut
    utb_ref[...] = ut.astype(BF16)


def _od_in(h, g, w, layer):
    tm = ROW_TILE
    return pl.pallas_call(
        _od_in_kernel,
        grid=(SEQ // tm,),
        in_specs=[
            pl.BlockSpec((tm, D_MODEL), lambda i: (i, 0)),
            pl.BlockSpec((1, D_MODEL), lambda i: (0, 0)),
            pl.BlockSpec((None, D_MODEL, C_WIDTH), lambda i: (layer, 0, 0), pipeline_mode=pl.Buffered(1)),
        ],
        out_specs=[
            pl.BlockSpec((C_WIDTH, tm), lambda i: (0, i)),
            pl.BlockSpec((C_WIDTH, tm), lambda i: (0, i)),
        ],
        out_shape=[
            jax.ShapeDtypeStruct((C_WIDTH, SEQ), F32),
            jax.ShapeDtypeStruct((C_WIDTH, SEQ), BF16),
        ],
        compiler_params=_params("parallel"),
        name="od_in",
    )(h, g, w)


def _split3(x):
    hi = x.astype(BF16)
    r1 = x - hi.astype(F32)
    mid = r1.astype(BF16)
    lo = (r1 - mid.astype(F32)).astype(BF16)
    return hi, mid, lo


def _dot_f32(a, b):
    a0, a1, a2 = _split3(a)
    b0, b1, b2 = _split3(b)
    return (_dot(a0, b0) + (_dot(a0, b1) + _dot(a1, b0))
            + (_dot(a0, b2) + _dot(a1, b1) + _dot(a2, b0)))


def _lane_expand(x, e):
    x0, x1, x2 = _split3(x)
    return _dot(x0, e) + _dot(x1, e) + _dot(x2, e)


def _s5_group_operators(ar, ai, bbr, bbi, bbt, crt, cit):
    tau = lax.broadcasted_iota(jnp.int32, (C_STATE, LANES), 1) & (S5_CHUNK - 1)
    expand = (lax.broadcasted_iota(jnp.int32, (C_GROUP, S5_COLS), 1) // S5_CHUNK
              == lax.broadcasted_iota(jnp.int32, (C_GROUP, S5_COLS), 0)).astype(BF16)
    one = jnp.ones((C_STATE, LANES), F32)
    zero = jnp.zeros((C_STATE, LANES), F32)
    pr, pi = one, zero
    rr, ri = one, zero
    fr, fi = ar, ai
    for b in range(S5_CHUNK.bit_length() - 1):
        bit = ((tau >> b) & 1) == 1
        pr, pi = jnp.where(bit, pr * fr - pi * fi, pr), jnp.where(bit, pr * fi + pi * fr, pi)
        rr, ri = jnp.where(bit, rr, rr * fr - ri * fi), jnp.where(bit, ri, rr * fi + ri * fr)
        fr, fi = fr * fr - fi * fi, 2.0 * (fr * fi)
    wide = lambda a: jnp.concatenate([a] * (S5_COLS // LANES), axis=1)
    pr, pi, rr, ri, ar, ai = wide(pr), wide(pi), wide(rr), wide(ri), wide(ar), wide(ai)
    cr = _lane_expand(crt, expand)
    ci = _lane_expand(cit, expand)
    cer = cr * pr - ci * pi
    cei = cr * pi + ci * pr
    kv = _dot_f32(bbt, jnp.concatenate([cer, -cei], axis=0))
    qr = cer * ar - cei * ai
    qi = cer * ai + cei * ar
    q = jnp.concatenate([qr, -qi], axis=0).astype(BF16)
    br = _lane_expand(bbr, expand)
    bi = _lane_expand(bbi, expand)
    ptr = br * rr - bi * ri
    pti = br * ri + bi * rr
    pt = jnp.concatenate([ptr, pti], axis=0).astype(BF16)
    pts = jnp.concatenate([pti, ptr], axis=0).astype(BF16)
    return kv, pt, pts, q


def _s5_kernel(ut2_ref, ar_ref, ai_ref, bbr_ref, bbi_ref, bbt_ref, crt_ref, cit_ref,
               a1_ref, a2_ref, a2s_ref, yt2_ref, ut_ref, yt_ref, u_ref, m_ref, q_ref, inc_ref, incs_ref,
               xs_ref, y_ref):
    gb = u_ref.shape[0]
    w = 2 * C_STATE
    ut_ref[...] = ut2_ref[...].reshape(ut_ref.shape)
    per = LANES // S5_CHUNK
    nsup = S5_NC // per
    keep = ((lax.broadcasted_iota(jnp.int32, (S5_CHUNK, LANES), 1) & (S5_CHUNK - 1))
            >= lax.broadcasted_iota(jnp.int32, (S5_CHUNK, LANES), 0))
    for gi in range(gb):
        for cp in range(C_GROUP):
            for k in range(per):
                u_ref[gi, k * nsup:(k + 1) * nsup, cp * S5_CHUNK:(cp + 1) * S5_CHUNK] = (
                    ut_ref[gi * C_GROUP + cp, :, k * S5_CHUNK:(k + 1) * S5_CHUNK])
        kv, pt, pts, q = _s5_group_operators(ar_ref[gi], ai_ref[gi], bbr_ref[gi], bbi_ref[gi], bbt_ref[gi],
                                             crt_ref[gi], cit_ref[gi])
        q_ref[gi] = q
        for cp in range(C_GROUP):
            taps = jnp.broadcast_to(kv[cp:cp + 1, :], (S5_CHUNK, S5_COLS))
            for v in range(S5_COLS // LANES):
                tile = pltpu.roll(taps[:, v * LANES:(v + 1) * LANES], 0, 1, stride=1, stride_axis=0)
                m_ref[gi, cp * S5_CHUNK:(cp + 1) * S5_CHUNK, v * LANES:(v + 1) * LANES] = (
                    jnp.where(keep, tile, 0.0).astype(BF16))
        u = u_ref[gi]
        inc_ref[:, gi * w:(gi + 1) * w] = _dot_nt(u, pt)
        incs_ref[:, gi * w:(gi + 1) * w] = _dot_nt(u, pts)

    a1 = a1_ref[...]
    a2 = a2_ref[...]
    a2s = a2s_ref[...]

    def step(n, carry):
        x, xs = carry
        row = pl.ds((n & (per - 1)) * nsup + (n >> (per.bit_length() - 1)), 1)
        xs_ref[row, :] = x
        x_new = a1 * x + a2 * xs + inc_ref[row, :]
        xs_new = a1 * xs + a2s * x + incs_ref[row, :]
        return x_new, xs_new

    zero = jnp.zeros((1, gb * w), F32)
    lax.fori_loop(0, S5_NC, step, (zero, zero), unroll=LANES // S5_CHUNK)

    for gi in range(gb):
        xst = xs_ref[:, gi * w:(gi + 1) * w].astype(BF16)
        y_ref[...] = _dot(u_ref[gi], m_ref[gi]) + _dot(xst, q_ref[gi])
        for c in range(C_GROUP):
            for k in range(per):
                yt_ref[gi * C_GROUP + c, :, k * S5_CHUNK:(k + 1) * S5_CHUNK] = (
                    y_ref[k * nsup:(k + 1) * nsup, c * S5_CHUNK:(c + 1) * S5_CHUNK])
    yt2_ref[...] = yt_ref[...].reshape(yt2_ref.shape)


def _s5(u_tb, prm):
    gb = S5_GB
    w = 2 * C_STATE
    grp3 = lambda i: (i, 0, 0)
    return pl.pallas_call(
        _s5_kernel,
        grid=(C_GROUPS // gb,),
        in_specs=[
            pl.BlockSpec((gb * C_GROUP, SEQ), lambda i: (i, 0)),
            pl.BlockSpec((gb, C_STATE, LANES), grp3),
            pl.BlockSpec((gb, C_STATE, LANES), grp3),
            pl.BlockSpec((gb, C_STATE, C_GROUP), grp3),
            pl.BlockSpec((gb, C_STATE, C_GROUP), grp3),
            pl.BlockSpec((gb, C_GROUP, w), grp3),
            pl.BlockSpec((gb, C_STATE, C_GROUP), grp3),
            pl.BlockSpec((gb, C_STATE, C_GROUP), grp3),
            pl.BlockSpec((1, gb * w), lambda i: (0, i)),
            pl.BlockSpec((1, gb * w), lambda i: (0, i)),
            pl.BlockSpec((1, gb * w), lambda i: (0, i)),
        ],
        out_specs=pl.BlockSpec((gb * C_GROUP, SEQ), lambda i: (i, 0)),
        out_shape=jax.ShapeDtypeStruct((C_WIDTH, SEQ), F32),
        scratch_shapes=[
            pltpu.VMEM((gb * C_GROUP, SEQ // LANES, LANES), BF16),
            pltpu.VMEM((gb * C_GROUP, SEQ // LANES, LANES), F32),
            pltpu.VMEM((gb, S5_NC, S5_COLS), BF16),
            pltpu.VMEM((gb, S5_COLS, S5_COLS), BF16),
            pltpu.VMEM((gb, w, S5_COLS), BF16),
            pltpu.VMEM((S5_NC, gb * w), F32),
            pltpu.VMEM((S5_NC, gb * w), F32),
            pltpu.VMEM((S5_NC, gb * w), F32),
            pltpu.VMEM((S5_NC, S5_COLS), F32),
        ],
        compiler_params=_params("parallel"),
        name="s5",
    )(u_tb, *prm)


def _s5_params(lam_re, lam_im, log_dt, b_re, b_im, c_re, c_im):
    lr = jnp.minimum(lam_re, -1e-4)
    li = lam_im
    dt = jnp.exp(log_dt)[:, None]
    mag = jnp.exp(lr * dt)
    ar = mag * jnp.cos(li * dt)
    ai = mag * jnp.sin(li * dt)
    den = lr * lr + li * li
    nr = ar - 1.0
    cr = (nr * lr + ai * li) / den
    ci = (ai * lr - nr * li) / den
    bbr = cr[..., None] * b_re - ci[..., None] * b_im
    bbi = cr[..., None] * b_im + ci[..., None] * b_re
    bbt = jnp.concatenate([bbr, bbi], axis=1).transpose(0, 2, 1)
    atr, ati = ar, ai
    for _ in range(S5_CHUNK.bit_length() - 1):
        atr, ati = atr * atr - ati * ati, 2.0 * (atr * ati)
    a1 = jnp.concatenate([atr, atr], axis=-1).reshape(1, -1)
    a2 = jnp.concatenate([-ati, ati], axis=-1).reshape(1, -1)
    a2s = jnp.concatenate([ati, -ati], axis=-1).reshape(1, -1)
    bcast = lambda a: jnp.broadcast_to(a[..., None], a.shape + (LANES,))
    return (bcast(ar), bcast(ai), bbr, bbi, bbt, c_re.transpose(0, 2, 1), c_im.transpose(0, 2, 1),
            a1, a2, a2s)


def _od_out_kernel(yt_ref, ut_ref, d_ref, wglu_ref, bglu_ref, wout_ref, g_ref, h_ref, o_ref):
    y = yt_ref[...] + d_ref[...] * ut_ref[...]
    z = jax.nn.gelu(y)
    gate = jax.nn.sigmoid(_dot_tn(wglu_ref[...].astype(BF16), z.astype(BF16)) + bglu_ref[...])
    m = _dot_tn((z * gate).astype(BF16), wout_ref[...].astype(BF16))
    o_ref[...] = h_ref[...] + _rms(m, g_ref[...])


def _od_out(y_t, u_t, d, w_glu, b_glu, w_out, g, h, layer):
    tm = ROW_TILE
    return pl.pallas_call(
        _od_out_kernel,
        grid=(SEQ // tm,),
        in_specs=[
            pl.BlockSpec((C_WIDTH, tm), lambda i: (0, i)),
            pl.BlockSpec((C_WIDTH, tm), lambda i: (0, i)),
            pl.BlockSpec((C_WIDTH, 1), lambda i: (0, 0)),
            pl.BlockSpec((None, C_WIDTH, C_WIDTH), lambda i: (layer, 0, 0), pipeline_mode=pl.Buffered(1)),
            pl.BlockSpec((C_WIDTH, 1), lambda i: (0, 0)),
            pl.BlockSpec((None, C_WIDTH, D_MODEL), lambda i: (layer, 0, 0), pipeline_mode=pl.Buffered(1)),
            pl.BlockSpec((1, D_MODEL), lambda i: (0, 0)),
            pl.BlockSpec((tm, D_MODEL), lambda i: (i, 0)),
        ],
        out_specs=pl.BlockSpec((tm, D_MODEL), lambda i: (i, 0)),
        out_shape=jax.ShapeDtypeStruct((SEQ, D_MODEL), F32),
        compiler_params=_params("parallel"),
        name="od_out",
    )(y_t, u_t, d, w_glu, b_glu, w_out, g, h)


def _row(v):
    return v.reshape(1, -1).astype(F32)


def kernel(x, norm_g, ffn_w_gate, ffn_w_up, ffn_w_down, ev_w_in, ev_ln_g, ev_ln_b, ev_w_s, ev_b_s, ev_w_gate2, ev_b_gate, ev_gla_norm_g, ev_w_out, od_w_in, od_lam_re, od_lam_im, od_log_dt, od_b_re, od_b_im, od_c_re, od_c_im, od_d, od_w_glu, od_b_glu, od_w_out):
    depth = norm_g.shape[0]
    h = x.reshape(SEQ, D_MODEL)
    for l in range(depth):
        i = l // 2
        g = norm_g[l]
        h = _ffn(h, _row(g[0]), _row(g[1]), ffn_w_gate, ffn_w_up, ffn_w_down, l, 0)
        if l % 2 == 0:
            w_g2 = jnp.pad(ev_w_gate2[i], ((0, LANES - B_GATE_RANK), (0, 0))).astype(BF16)
            p, gk = _ev_in(h, _row(g[2]), jnp.swapaxes(ev_w_in, 1, 2), w_g2, _row(ev_b_gate[i]),
                           _row(ev_ln_g[i]), _row(ev_ln_b[i]), i)
            bs_full = jnp.repeat(ev_b_s[i].T, A_HEAD_DIM, axis=1).astype(F32)
            y = _ev_mix(p, gk, ev_w_s[i], bs_full, _row(ev_gla_norm_g[i]))
            h = _ev_out(y, ev_w_out, _row(g[3]), h, i)
        else:
            u_t, u_tb = _od_in(h, _row(g[2]), od_w_in, i)
            prm = _s5_params(od_lam_re[i], od_lam_im[i], od_log_dt[i], od_b_re[i], od_b_im[i],
                             od_c_re[i], od_c_im[i])
            y_t = _s5(u_tb, prm)
            h = _od_out(y_t, u_t, od_d[i].reshape(C_WIDTH, 1), od_w_glu, od_b_glu[i].reshape(C_WIDTH, 1),
                        od_w_out, _row(g[3]), h, i)
        h = _ffn(h, _row(g[4]), _row(g[5]), ffn_w_gate, ffn_w_up, ffn_w_down, l, 1)
    return h.reshape(x.shape)
```
